```python
import jax
import jax.numpy as jnp
from jax import lax
import numpy as np

D_MODEL = 2048
BATCH = 8
SEQ = 4096
DEPTH = 2

CHUNK = 64
Q_BLOCK = 128
NORM_EPS = 1e-6
LN_EPS = 1e-5
D_FF = 4 * D_MODEL

RWKV_HEAD = 64
RWKV_DIM = D_MODEL // 4
RWKV_HEADS = RWKV_DIM // RWKV_HEAD
DECAY_LORA = 32
ICLR_LORA = 32
VRES_LORA = 32
GATE_LORA = 96
RWKV_GN_EPS = 64e-5

CONV_DIM = D_MODEL // 4
CONV_WIDTH = 31

MLA_DIM = D_MODEL // 2
V_HEAD = 128
MLA_HEADS = MLA_DIM // V_HEAD
QK_NOPE = 128
QK_ROPE = 64
Q_LORA = 512
KV_LORA = 256
ROPE_THETA = 10000.0

MIX_DIM = RWKV_DIM + CONV_DIM + MLA_DIM
RWKV_IN = 3 * RWKV_DIM + DECAY_LORA + ICLR_LORA + GATE_LORA
CONV_IN = 2 * CONV_DIM
MLA_IN = Q_LORA + KV_LORA + QK_ROPE
N_IN = RWKV_IN + CONV_IN + MLA_IN

kernel_name = "hybrid_rwkv7_conformer_mla_block"


def rms_norm(x, g):
    xf = x.astype(jnp.float32)
    y = xf * lax.rsqrt(jnp.mean(xf * xf, axis=-1, keepdims=True) + NORM_EPS)
    return (y * g.astype(jnp.float32)).astype(x.dtype)


def layer_norm(x, g, b, eps):
    xf = x.astype(jnp.float32)
    xc = xf - jnp.mean(xf, axis=-1, keepdims=True)
    var = jnp.mean(xc * xc, axis=-1, keepdims=True)
    return (xc * lax.rsqrt(var + eps) * g + b).astype(x.dtype)


def token_shift_mix(p, mu):
    prev = jnp.pad(p, ((0, 0), (1, 0), (0, 0)))[:, :-1]
    return p + (prev - p) * mu


def rope_tables(positions):
    inv_freq = ROPE_THETA ** (-jnp.arange(0, QK_ROPE, 2, dtype=jnp.float32) / QK_ROPE)
    ang = positions.astype(jnp.float32)[..., None] * inv_freq
    return jnp.cos(ang), jnp.sin(ang)


def apply_rope(x, cos, sin):
    half = x.shape[-1] // 2
    xf = x.astype(jnp.float32)
    x1, x2 = xf[..., :half], xf[..., half:]
    return jnp.concatenate([x1 * cos - x2 * sin, x1 * sin + x2 * cos], axis=-1).astype(x.dtype)


def wkv7_scan(r, w, k, v, a, b):
    dt = r.dtype
    bsz, _, h, n = r.shape
    xs = tuple(jnp.moveaxis(t.astype(jnp.float32), 1, 0) for t in (r, w, k, v, a, b))

    def step(state, inp):
        r_t, w_t, k_t, v_t, a_t, b_t = inp
        sa = jnp.einsum('bhij,bhj->bhi', state, a_t)
        state = (state * w_t[:, :, None, :] + sa[..., None] * b_t[:, :, None, :]
                 + v_t[..., None] * k_t[:, :, None, :])
        return state, jnp.einsum('bhij,bhj->bhi', state, r_t)

    s0 = jnp.zeros((bsz, h, n, n), jnp.float32)
    _, ys = lax.scan(step, s0, xs)
    return jnp.moveaxis(ys, 0, 1).astype(dt)


def rwkv7_mixer(p, mu, w0, w_up, a0, a_up, g_up, k_k, k_a, r_k, lnx_g, lnx_b, v_first, vres):
    bsz, s, _ = p.shape
    h, n = RWKV_HEADS, RWKV_HEAD
    p = token_shift_mix(p, mu)
    o1, o2, o3 = RWKV_DIM, 2 * RWKV_DIM, 3 * RWKV_DIM
    o4 = o3 + DECAY_LORA
    o5 = o4 + ICLR_LORA
    r, k, v = p[..., :o1], p[..., o1:o2], p[..., o2:o3]
    wd, ad, gd = p[..., o3:o4], p[..., o4:o5], p[..., o5:]
    w = -jax.nn.softplus(-(w0 + jnp.tanh(wd) @ w_up)) - 0.5
    decay = jnp.exp(-jnp.exp(w.astype(jnp.float32)))
    a = jax.nn.sigmoid(a0 + ad @ a_up)
    g = jax.nn.sigmoid(gd) @ g_up
    if vres is None:
        v_first = v
    else:
        pv, v_mu, v0, v_up = vres
        v = v + (v_first - v) * jax.nn.sigmoid(v0 + token_shift_mix(pv, v_mu) @ v_up)
    kk = (k * k_k).reshape(bsz, s, h, n).astype(jnp.float32)
    kk = kk / jnp.maximum(jnp.sqrt(jnp.sum(kk * kk, axis=-1, keepdims=True)), 1e-12)
    k = k * (1 + (a - 1) * k_a)
    rh, kh, vh, ah = [t.reshape(bsz, s, h, n) for t in (r, k, v, a)]
    o = wkv7_scan(rh, decay.reshape(bsz, s, h, n), kh, vh, -kk, kk * ah)
    o = layer_norm(o, lnx_g.reshape(h, n), lnx_b.reshape(h, n), RWKV_GN_EPS)
    o = o + jnp.sum(rh * kh * r_k, axis=-1, keepdims=True) * vh
    return o.reshape(bsz, s, RWKV_DIM) * g, v_first


def conformer_conv_mixer(p, conv_w, conv_b, ln_g, ln_b):
    u = p[..., :CONV_DIM] * jax.nn.sigmoid(p[..., CONV_DIM:])
    u = lax.conv_general_dilated(
        u, conv_w[:, None, :].astype(u.dtype), (1,), [(CONV_WIDTH - 1, 0)],
        dimension_numbers=('NWC', 'WIO', 'NWC'), feature_group_count=CONV_DIM) + conv_b
    u = layer_norm(u, ln_g, ln_b, LN_EPS)
    return jax.nn.silu(u)


def mla_mixer(p, cos, sin, q_norm_g, w_qb, kv_norm_g, w_kvb):
    bsz, s, _ = p.shape
    h = MLA_HEADS
    q_c = rms_norm(p[..., :Q_LORA], q_norm_g)
    kv_c = rms_norm(p[..., Q_LORA:Q_LORA + KV_LORA], kv_norm_g)
    k_rope = apply_rope(p[..., Q_LORA + KV_LORA:], cos, sin)
    q = (q_c @ w_qb).reshape(bsz, s, h, QK_NOPE + QK_ROPE)
    q_nope = q[..., :QK_NOPE]
    q_rope = apply_rope(q[..., QK_NOPE:], cos[:, :, None, :], sin[:, :, None, :])
    kv = (kv_c @ w_kvb).reshape(bsz, s, h, QK_NOPE + V_HEAD)
    k_nope, v = kv[..., :QK_NOPE], kv[..., QK_NOPE:]
    scale = (QK_NOPE + QK_ROPE) ** -0.5
    n_blk = s // Q_BLOCK
    qn_b = q_nope.reshape(bsz, n_blk, Q_BLOCK, h, QK_NOPE).swapaxes(0, 1)
    qr_b = q_rope.reshape(bsz, n_blk, Q_BLOCK, h, QK_ROPE).swapaxes(0, 1)
    key_chunk = jnp.arange(s) // CHUNK

    def attend(args):
        qn, qr, blk = args
        sc = (jnp.einsum('bqhd,bkhd->bhqk', qn, k_nope)
              + jnp.einsum('bqhd,bkd->bhqk', qr, k_rope)).astype(jnp.float32) * scale
        q_chunk = (blk * Q_BLOCK + jnp.arange(Q_BLOCK)) // CHUNK
        mask = key_chunk[None, :] <= q_chunk[:, None]
        sc = jnp.where(mask, sc, -jnp.inf)
        pr = jax.nn.softmax(sc, axis=-1).astype(v.dtype)
        return jnp.einsum('bhqk,bkhd->bqhd', pr, v)

    o = lax.map(attend, (qn_b, qr_b, jnp.arange(n_blk)))
    return o.swapaxes(0, 1).reshape(bsz, s, MLA_DIM)


def setup_inputs(seed: int = 0) -> dict:
    key = jax.random.key(seed)
    ks = iter(jax.random.split(key, 48))
    f32 = jnp.float32
    L, Lv = DEPTH, DEPTH - 1

    def nrm(shape, scale):
        return jax.random.normal(next(ks), shape, f32) * scale

    def unif(shape, lo, hi):
        return jax.random.uniform(next(ks), shape, f32, lo, hi)

    x = nrm((BATCH, SEQ, D_MODEL), 1.0)
    c = nrm((BATCH, D_MODEL), 1.0)
    offset = jax.random.randint(next(ks), (BATCH, 1), 0, 1024) * CHUNK
    positions = (offset + jnp.arange(SEQ)[None, :]).astype(jnp.int32)
    return {
        "x": x,
        "c": c,
        "positions": positions,
        "ada_w": nrm((L, D_MODEL, 6 * D_MODEL), D_MODEL ** -0.5),
        "ada_b": nrm((L, 6 * D_MODEL), 0.01),
        "norm1_g": 1.0 + nrm((L, D_MODEL), 0.02),
        "norm2_g": 1.0 + nrm((L, D_MODEL), 0.02),
        "final_g": 1.0 + nrm((D_MODEL,), 0.02),
        "w_in": nrm((L, D_MODEL, N_IN), D_MODEL ** -0.5),
        "w_in_vres": nrm((Lv, D_MODEL, VRES_LORA), D_MODEL ** -0.5),
        "rwkv_mu": unif((L, RWKV_IN), 0.0, 1.0),
        "vres_mu": unif((Lv, VRES_LORA), 0.0, 1.0),
        "decay_w0": unif((L, RWKV_DIM), -6.0, -1.0),
        "decay_up": nrm((L, DECAY_LORA, RWKV_DIM), 0.1 * DECAY_LORA ** -0.5),
        "iclr_a0": nrm((L, RWKV_DIM), 0.1),
        "iclr_up": nrm((L, ICLR_LORA, RWKV_DIM), 0.3 * ICLR_LORA ** -0.5),
        "gate_up": nrm((L, GATE_LORA, RWKV_DIM), GATE_LORA ** -0.5),
        "vres_v0": 1.0 + nrm((Lv, RWKV_DIM), 0.1),
        "vres_up": nrm((Lv, VRES_LORA, RWKV_DIM), 0.3 * VRES_LORA ** -0.5),
        "k_k": 0.85 + nrm((L, RWKV_DIM), 0.02),
        "k_a": 1.0 + nrm((L, RWKV_DIM), 0.02),
        "r_k": nrm((L, RWKV_HEADS, RWKV_HEAD), 0.1),
        "lnx_g": 1.0 + nrm((L, RWKV_DIM), 0.02),
        "lnx_b": nrm((L, RWKV_DIM), 0.01),
        "conv_w": nrm((L, CONV_WIDTH, CONV_DIM), CONV_WIDTH ** -0.5),
        "conv_b": nrm((L, CONV_DIM), 0.01),
        "conv_ln_g": 1.0 + nrm((L, CONV_DIM), 0.02),
        "conv_ln_b": nrm((L, CONV_DIM), 0.01),
        "q_a_norm_g": 1.0 + nrm((L, Q_LORA), 0.02),
        "w_qb": nrm((L, Q_LORA, MLA_HEADS * (QK_NOPE + QK_ROPE)), Q_LORA ** -0.5),
        "kv_a_norm_g": 1.0 + nrm((L, KV_LORA), 0.02),
        "w_kvb": nrm((L, KV_LORA, MLA_HEADS * (QK_NOPE + V_HEAD)), KV_LORA ** -0.5),
        "w_out": nrm((L, MIX_DIM, D_MODEL), MIX_DIM ** -0.5),
        "mlp_w1": nrm((L, D_MODEL, D_FF), D_MODEL ** -0.5),
        "mlp_w2": nrm((L, D_FF, D_MODEL), D_FF ** -0.5),
    }


def reference(x, c, positions, ada_w, ada_b, norm1_g, norm2_g, final_g, w_in, w_in_vres,
              rwkv_mu, vres_mu, decay_w0, decay_up, iclr_a0, iclr_up, gate_up, vres_v0,
              vres_up, k_k, k_a, r_k, lnx_g, lnx_b, conv_w, conv_b, conv_ln_g, conv_ln_b,
              q_a_norm_g, w_qb, kv_a_norm_g, w_kvb, w_out, mlp_w1, mlp_w2):
    cos, sin = rope_tables(positions)
    c_act = jax.nn.silu(c)
    v_first = None
    for l in range(DEPTH):
        mod = c_act @ ada_w[l] + ada_b[l]
        sh1, sc1, gt1, sh2, sc2, gt2 = jnp.split(mod[:, None, :], 6, axis=-1)
        h = rms_norm(x, norm1_g[l]) * (1 + sc1) + sh1
        w_comb = w_in[l] if l == 0 else jnp.concatenate([w_in[l], w_in_vres[l - 1]], axis=1)
        p = h @ w_comb
        p_rwkv = p[..., :RWKV_IN]
        p_conv = p[..., RWKV_IN:RWKV_IN + CONV_IN]
        p_mla = p[..., RWKV_IN + CONV_IN:N_IN]
        if l == 0:
            vres = None
        else:
            vres = (p[..., N_IN:], vres_mu[l - 1], vres_v0[l - 1], vres_up[l - 1])
        y_a, v_first = rwkv7_mixer(p_rwkv, rwkv_mu[l], decay_w0[l], decay_up[l], iclr_a0[l],
                                   iclr_up[l], gate_up[l], k_k[l], k_a[l], r_k[l],
                                   lnx_g[l], lnx_b[l], v_first, vres)
        y_b = conformer_conv_mixer(p_conv, conv_w[l], conv_b[l], conv_ln_g[l], conv_ln_b[l])
        y_c = mla_mixer(p_mla, cos, sin, q_a_norm_g[l], w_qb[l], kv_a_norm_g[l], w_kvb[l])
        y = jnp.concatenate([y_a, y_b, y_c], axis=-1) @ w_out[l]
        x = x + gt1 * y
        h = rms_norm(x, norm2_g[l]) * (1 + sc2) + sh2
        x = x + gt2 * (jnp.square(jax.nn.relu(h @ mlp_w1[l])) @ mlp_w2[l])
    return rms_norm(x, final_g)
```

```python
import functools

import jax
import jax.numpy as jnp
from jax import lax
from jax.experimental import pallas as pl
from jax.experimental.pallas import tpu as pltpu

F32 = jnp.float32
BF16 = jnp.bfloat16

D_MODEL = 2048
DEPTH = 2
CHUNK = 64
NORM_EPS = 1e-6
LN_EPS = 1e-5
D_FF = 4 * D_MODEL

RWKV_HEAD = 64
RWKV_DIM = 512
RWKV_HEADS = 8
RWKV_PAIRS = RWKV_HEADS // 2
DECAY_LORA = 32
ICLR_LORA = 32
VRES_LORA = 32
GATE_LORA = 96
RWKV_GN_EPS = 64e-5
LORA_PAD = 256

CONV_DIM = 512
CONV_WIDTH = 31
CONV_HALO = 32

MLA_DIM = 1024
V_HEAD = 128
MLA_HEADS = 8
QK_NOPE = 128
QK_ROPE = 64
Q_LORA = 512
KV_LORA = 256
ROPE_THETA = 10000.0
QK_PAD = 256
MLA_IN_PAD = 1024

RWKV_IN = 3 * RWKV_DIM + DECAY_LORA + ICLR_LORA + GATE_LORA
CONV_IN = 2 * CONV_DIM
N_IN = RWKV_IN + CONV_IN + Q_LORA + KV_LORA + QK_ROPE

LANE = 128
V7X_VMEM_LIMIT = 56 * 1024 * 1024
MASK_VALUE = -1e30


def _params(semantics, vmem=V7X_VMEM_LIMIT):
    return pltpu.CompilerParams(dimension_semantics=semantics, vmem_limit_bytes=vmem)


def _resident(shape):
    nd = len(shape)
    return pl.BlockSpec(shape, lambda *_: (0,) * nd, pipeline_mode=pl.Buffered(1))


def _dot(a, b):
    return jnp.dot(a, b, preferred_element_type=F32)


def _dot_nt(a, b):
    return lax.dot_general(a, b, (((1,), (1,)), ((), ())), preferred_element_type=F32)


def _split(x):
    hi = x.astype(BF16)
    lo = (x - hi.astype(F32)).astype(BF16)
    return hi, lo


def _dot3(a, b):
    ah, al = _split(a)
    bh, bl = _split(b)
    return _dot(jnp.concatenate([ah, al, ah], axis=1), jnp.concatenate([bh, bh, bl], axis=0))


def _dot3_nt(a, b):
    ah, al = _split(a)
    bh, bl = _split(b)
    return _dot_nt(jnp.concatenate([ah, al, ah], axis=1), jnp.concatenate([bh, bh, bl], axis=1))


def _dot2_exact_rhs(a, b_bf16):
    ah = a.astype(BF16)
    r1 = a - ah.astype(F32)
    am = r1.astype(BF16)
    al = (r1 - am.astype(F32)).astype(BF16)
    return _dot(jnp.concatenate([ah, am, al], axis=1), jnp.concatenate([b_bf16] * 3, axis=0))


def _rms(x, eps):
    return x * lax.rsqrt(jnp.mean(x * x, axis=-1, keepdims=True) + eps)


def _sigmoid(x):
    return 1.0 / (1.0 + jnp.exp(-x))


def _iota(shape, dim):
    return lax.broadcasted_iota(jnp.int32, shape, dim)


def _ada_kernel(c_ref, w_ref, b_ref, o_ref):
    c = c_ref[...]
    o_ref[...] = _dot3(c * _sigmoid(c), w_ref[...]) + b_ref[...]


def _ada_call(c, ada_w, ada_b):
    depth, d, n = ada_w.shape
    bsz = c.shape[0]
    tn = 512
    return pl.pallas_call(
        _ada_kernel,
        out_shape=jax.ShapeDtypeStruct((depth, bsz, n), F32),
        grid=(depth, n // tn),
        in_specs=[
            pl.BlockSpec((bsz, d), lambda l, j: (0, 0)),
            pl.BlockSpec((None, d, tn), lambda l, j: (l, 0, j)),
            pl.BlockSpec((None, 1, tn), lambda l, j: (l, 0, j)),
        ],
        out_specs=pl.BlockSpec((None, bsz, tn), lambda l, j: (l, 0, j)),
        compiler_params=_params(("parallel", "parallel")),
        name="ada_mod",
    )(c, ada_w, ada_b.reshape(depth, 1, n))


def _rope_kernel(pos_ref, invf_ref, cos_ref, sin_ref):
    ang = pos_ref[...] * invf_ref[...]
    cos_ref[...] = jnp.cos(ang)
    sin_ref[...] = jnp.sin(ang)


def _rope_call(positions):
    bsz, s = positions.shape
    ts = min(s, 1024)
    inv_freq = ROPE_THETA ** (-jnp.arange(0, QK_ROPE, 2, dtype=F32) / QK_ROPE)
    invf = jnp.tile(inv_freq, LANE // (QK_ROPE // 2)).reshape(1, LANE)
    pos = positions.astype(F32).reshape(bsz, s, 1)
    shp = jax.ShapeDtypeStruct((bsz, s, LANE), F32)
    return pl.pallas_call(
        _rope_kernel,
        out_shape=(shp, shp),
        grid=(bsz, s // ts),
        in_specs=[
            pl.BlockSpec((None, ts, 1), lambda b, i: (b, i, 0)),
            pl.BlockSpec((1, LANE), lambda b, i: (0, 0)),
        ],
        out_specs=(
            pl.BlockSpec((None, ts, LANE), lambda b, i: (b, i, 0)),
            pl.BlockSpec((None, ts, LANE), lambda b, i: (b, i, 0)),
        ),
        compiler_params=_params(("parallel", "parallel")),
        name="rope_tables",
    )(pos, invf)


def _inproj_kernel(x_ref, g_ref, sc_ref, sh_ref, w_rkv, w_lora, w_conv, w_mla,
                   o_rkv, o_lora, o_conv, o_mla):
    x = x_ref[...]
    h = _rms(x, NORM_EPS) * g_ref[...] * (1.0 + sc_ref[...]) + sh_ref[...]
    hb = h.astype(BF16)
    o_rkv[...] = _dot(hb, w_rkv[...])
    o_lora[...] = _dot(hb, w_lora[...])
    o_conv[...] = _dot(hb, w_conv[...])
    o_mla[...] = _dot(hb, w_mla[...])


def _inproj_call(x, g, sc, sh, w_rkv, w_lora, w_conv, w_mla):
    bsz, s, d = x.shape
    tm = min(s, 256)
    widths = (w_rkv.shape[1], w_lora.shape[1], w_conv.shape[1], w_mla.shape[1])
    tok = lambda n: pl.BlockSpec((None, tm, n), lambda b, i: (b, i, 0))
    per_b = pl.BlockSpec((None, 1, d), lambda b, i: (b, 0, 0))
    return pl.pallas_call(
        _inproj_kernel,
        out_shape=tuple(jax.ShapeDtypeStruct((bsz, s, n), F32) for n in widths),
        grid=(bsz, s // tm),
        in_specs=[tok(d), _resident((1, d)), per_b, per_b,
                  _resident(w_rkv.shape), _resident(w_lora.shape),
                  _resident(w_conv.shape), _resident(w_mla.shape)],
        out_specs=tuple(tok(n) for n in widths),
        compiler_params=_params(("parallel", "parallel")),
        name="norm_inproj",
    )(x, g, sc, sh, w_rkv, w_lora, w_conv, w_mla)


def _block_diag(y):
    lane = _iota(y.shape, 1)
    zero = jnp.zeros_like(y)
    return jnp.concatenate([jnp.where(lane < RWKV_HEAD, y, zero),
                            jnp.where(lane >= RWKV_HEAD, y, zero)], axis=0)


def _packed_mm(x, y):
    xh, xl = _split(x)
    yh, yl = _split(y)
    lhs = jnp.concatenate([xh, xl, xh], axis=1)
    rhs = jnp.concatenate([_block_diag(yh), _block_diag(yh), _block_diag(yl)], axis=0)
    return _dot(lhs, rhs)


def _head_sum_matrix(scale):
    r = _iota((LANE, LANE), 0) // RWKV_HEAD
    c = _iota((LANE, LANE), 1) // RWKV_HEAD
    return jnp.where(r == c, scale, 0.0).astype(BF16)


def _rwkv_kernel(*refs, has_vres, tt):
    if has_vres:
        (prkv_ref, plora_ref, vfirst_ref, mu_rkv, mu_lora, wc_ref, w0_ref, a0_ref, v0_ref,
         kk_ref, ka_ref, rk_ref, lng_ref, lnb_ref, y_ref,
         prev_rkv, prev_lora, h_scr, at_s, rt_s, bt_s, kt_s, bp_s, kp_s, v_s, pf_s, o_s) = refs
    else:
        (prkv_ref, plora_ref, mu_rkv, mu_lora, wc_ref, w0_ref, a0_ref,
         kk_ref, ka_ref, rk_ref, lng_ref, lnb_ref, y_ref, vout_ref,
         prev_rkv, prev_lora, h_scr, at_s, rt_s, bt_s, kt_s, bp_s, kp_s, v_s, pf_s, o_s) = refs
    n_chunks = tt // CHUNK

    @pl.when(pl.program_id(1) == 0)
    def _():
        prev_rkv[...] = jnp.zeros_like(prev_rkv)
        prev_lora[...] = jnp.zeros_like(prev_lora)
        h_scr[...] = jnp.zeros_like(h_scr)

    def shift_mix(p_ref, prev_ref, mu_ref):
        p = p_ref[...]
        row = _iota(p.shape, 0)
        prev = jnp.where(row == 0, prev_ref[0:1, :], pltpu.roll(p, 1, 0))
        prev_ref[0:1, :] = p[tt - 1:tt, :]
        return p + (prev - p) * mu_ref[...]

    xs = shift_mix(prkv_ref, prev_rkv, mu_rkv)
    xl = shift_mix(plora_ref, prev_lora, mu_lora)
    r = xs[:, 0:RWKV_DIM]
    k = xs[:, RWKV_DIM:2 * RWKV_DIM]
    v = xs[:, 2 * RWKV_DIM:3 * RWKV_DIM]

    lane = _iota(xl.shape, 1)
    o_w, o_a, o_g = DECAY_LORA, DECAY_LORA + ICLR_LORA, DECAY_LORA + ICLR_LORA + GATE_LORA
    act = jnp.where(lane < o_w, jnp.tanh(xl),
                    jnp.where((lane >= o_a) & (lane < o_g), _sigmoid(xl), xl))
    lora = _dot3(act, wc_ref[...])
    z = w0_ref[...] + lora[:, 0:RWKV_DIM]
    w_log = -(jnp.maximum(-z, 0.0) + jnp.log(1.0 + jnp.exp(-jnp.abs(z)))) - 0.5
    lw = -jnp.exp(w_log)
    a_ic = _sigmoid(a0_ref[...] + lora[:, RWKV_DIM:2 * RWKV_DIM])
    gate = lora[:, 2 * RWKV_DIM:3 * RWKV_DIM]
    if has_vres:
        v = v + (vfirst_ref[...] - v) * _sigmoid(v0_ref[...] + lora[:, 3 * RWKV_DIM:4 * RWKV_DIM])
    else:
        vout_ref[...] = v

    ones_bd = _head_sum_matrix(1.0)
    mean_bd = _head_sum_matrix(1.0 / RWKV_HEAD)

    def per_head(x, mat):
        return jnp.concatenate(
            [_dot2_exact_rhs(x[:, p * LANE:(p + 1) * LANE], mat) for p in range(RWKV_PAIRS)], axis=1)

    kk = k * kk_ref[...]
    kk = kk / jnp.maximum(jnp.sqrt(per_head(kk * kk, ones_bd)), 1e-12)
    k2 = k * (1.0 + (a_ic - 1.0) * ka_ref[...])
    a_vec = -kk
    b_vec = kk * a_ic
    bonus = per_head(r * k2 * rk_ref[...], ones_bd) * v

    tr = _iota((tt, tt), 0)
    tc = _iota((tt, tt), 1)
    tri = jnp.where((tr // CHUNK == tc // CHUNK) & (tc <= tr), 1.0, 0.0).astype(BF16)
    cum = _dot2_exact_rhs_left(tri, lw)
    cum3 = cum.reshape(n_chunks, CHUNK, RWKV_DIM)
    cum_end = jnp.broadcast_to(cum3[:, CHUNK - 1:CHUNK, :], cum3.shape).reshape(tt, RWKV_DIM)
    pf = jnp.exp(cum)
    pinv = jnp.exp(-cum)
    pprev = jnp.exp(cum - lw)
    pend = jnp.exp(cum_end - cum)

    def put(dst, val):
        for p in range(RWKV_PAIRS):
            dst[p] = val[:, p * LANE:(p + 1) * LANE]

    put(at_s, a_vec * pprev)
    put(rt_s, r * pf)
    put(bt_s, b_vec * pinv)
    put(kt_s, k2 * pinv)
    put(bp_s, b_vec * pend)
    put(kp_s, k2 * pend)
    put(v_s, v)
    put(pf_s, pf)

    t_loc = _iota((CHUNK, LANE), 0)
    s_loc = _iota((CHUNK, LANE), 1) % RWKV_HEAD
    strict = s_loc < t_loc
    incl = s_loc <= t_loc
    eye_p = jnp.where(s_loc == t_loc, 1.0, 0.0)
    rr = _iota((LANE, LANE), 0)
    cc = _iota((LANE, LANE), 1)
    same_head = (rr // RWKV_HEAD) == (cc // RWKV_HEAD)
    diag = rr == cc

    def chunk_body(c, carry):
        rows = pl.ds(pl.multiple_of(c * CHUNK, CHUNK), CHUNK)
        last = pl.ds(c * CHUNK + CHUNK - 1, 1)
        for p in range(RWKV_PAIRS):
            at, rt = at_s[p, rows, :], rt_s[p, rows, :]
            bt, kt = bt_s[p, rows, :], kt_s[p, rows, :]
            bp, kp = bp_s[p, rows, :], kp_s[p, rows, :]
            vv = v_s[p, rows, :]
            p_end = pf_s[p, last, :]
            g = _dot3_nt(jnp.concatenate([at, rt], axis=0),
                         jnp.concatenate([_block_diag(bt), _block_diag(kt)], axis=0))
            aab = jnp.where(strict, g[0:CHUNK, 0:LANE], 0.0)
            aak = jnp.where(strict, g[0:CHUNK, LANE:2 * LANE], 0.0)
            arb = jnp.where(incl, g[CHUNK:2 * CHUNK, 0:LANE], 0.0)
            ark = jnp.where(incl, g[CHUNK:2 * CHUNK, LANE:2 * LANE], 0.0)
            npow = aab
            tinv = eye_p + aab
            for _ in range(5):
                npow = _packed_mm(npow, npow)
                tinv = tinv + _packed_mm(tinv, npow)
            w_m = _packed_mm(tinv, at)
            u0 = _packed_mm(tinv, _packed_mm(aak, vv))
            q_m = rt + _packed_mm(arb, w_m)
            y0 = _packed_mm(arb, u0) + _packed_mm(ark, vv)
            bk_t = jnp.concatenate([bp, kp], axis=0).T
            upd = _dot3(bk_t, jnp.concatenate(
                [jnp.concatenate([w_m, u0], axis=1),
                 jnp.concatenate([jnp.zeros_like(vv), vv], axis=1)], axis=0))
            m_bd = jnp.where(same_head, upd[:, 0:LANE], 0.0) + jnp.where(diag, p_end, 0.0)
            h_add = jnp.where(same_head, upd[:, LANE:2 * LANE], 0.0)
            h0 = h_scr[p]
            o_s[rows, p * LANE:(p + 1) * LANE] = _dot3(q_m, h0) + y0
            h_scr[p] = _dot3(m_bd, h0) + h_add
        return carry

    lax.fori_loop(0, n_chunks, chunk_body, 0)

    o = o_s[...]
    mean = per_head(o, mean_bd)
    oc = o - mean
    var = per_head(oc * oc, mean_bd)
    on = oc * lax.rsqrt(var + RWKV_GN_EPS) * lng_ref[...] + lnb_ref[...]
    y_ref[...] = ((on + bonus) * gate).astype(y_ref.dtype)


def _dot2_exact_rhs_left(m_bf16, x):
    xh = x.astype(BF16)
    r1 = x - xh.astype(F32)
    xm = r1.astype(BF16)
    xl = (r1 - xm.astype(F32)).astype(BF16)
    return _dot(jnp.concatenate([m_bf16] * 3, axis=1), jnp.concatenate([xh, xm, xl], axis=0))


def _rwkv_call(p_rkv, p_lora, v_first, prm):
    bsz, s, _ = p_rkv.shape
    tt = min(s, 256)
    has_vres = v_first is not None
    tok = lambda n: pl.BlockSpec((None, tt, n), lambda b, i: (b, i, 0))
    row = lambda n: _resident((1, n))
    in_specs = [tok(3 * RWKV_DIM), tok(LORA_PAD)]
    args = [p_rkv, p_lora]
    if has_vres:
        in_specs.append(tok(RWKV_DIM))
        args.append(v_first)
    in_specs += [row(3 * RWKV_DIM), row(LORA_PAD), _resident((LORA_PAD, 4 * RWKV_DIM)),
                 row(RWKV_DIM), row(RWKV_DIM)]
    args += [prm["mu_rkv"], prm["mu_lora"], prm["w_lora_up"], prm["w0"], prm["a0"]]
    if has_vres:
        in_specs.append(row(RWKV_DIM))
        args.append(prm["v0"])
    in_specs += [row(RWKV_DIM)] * 5
    args += [prm["k_k"], prm["k_a"], prm["r_k"], prm["lnx_g"], prm["lnx_b"]]
    y_shape = jax.ShapeDtypeStruct((bsz, s, RWKV_DIM), BF16)
    if has_vres:
        out_shape, out_specs = y_shape, tok(RWKV_DIM)
    else:
        out_shape = (y_shape, jax.ShapeDtypeStruct((bsz, s, RWKV_DIM), F32))
        out_specs = (tok(RWKV_DIM), tok(RWKV_DIM))
    pair_tile = pltpu.VMEM((RWKV_PAIRS, tt, LANE), F32)
    out = pl.pallas_call(
        functools.partial(_rwkv_kernel, has_vres=has_vres, tt=tt),
        out_shape=out_shape,
        grid=(bsz, s // tt),
        in_specs=in_specs,
        out_specs=out_specs,
        scratch_shapes=[
            pltpu.VMEM((8, 3 * RWKV_DIM), F32),
            pltpu.VMEM((8, LORA_PAD), F32),
            pltpu.VMEM((RWKV_PAIRS, LANE, LANE), F32),
        ] + [pair_tile] * 8 + [pltpu.VMEM((tt, RWKV_DIM), F32)],
        compiler_params=_params(("parallel", "arbitrary")),
        name="rwkv7_mixer",
    )(*args)
    return out if has_vres else out


def _conv_kernel(p_ref, w_ref, b_ref, g_ref, be_ref, o_ref, ubuf, *, tt):
    @pl.when(pl.program_id(1) == 0)
    def _():
        ubuf[0:CONV_HALO, :] = jnp.zeros((CONV_HALO, CONV_DIM), F32)

    p = p_ref[...]
    ubuf[CONV_HALO:CONV_HALO + tt, :] = p[:, 0:CONV_DIM] * _sigmoid(p[:, CONV_DIM:2 * CONV_DIM])
    acc = jnp.zeros((tt, CONV_DIM), F32) + b_ref[...]
    base = CONV_HALO - (CONV_WIDTH - 1)
    for j in range(CONV_WIDTH):
        acc = acc + ubuf[base + j:base + j + tt, :] * w_ref[j:j + 1, :]
    ubuf[0:CONV_HALO, :] = ubuf[tt:tt + CONV_HALO, :]
    mean = jnp.mean(acc, axis=-1, keepdims=True)
    xc = acc - mean
    var = jnp.mean(xc * xc, axis=-1, keepdims=True)
    u = xc * lax.rsqrt(var + LN_EPS) * g_ref[...] + be_ref[...]
    o_ref[...] = (u * _sigmoid(u)).astype(o_ref.dtype)


def _conv_call(p_conv, conv_w, conv_b, ln_g, ln_b):
    bsz, s, _ = p_conv.shape
    tt = min(s, 512)
    wpad = jnp.zeros((CONV_HALO, CONV_DIM), F32).at[:CONV_WIDTH].set(conv_w)
    row = _resident((1, CONV_DIM))
    return pl.pallas_call(
        functools.partial(_conv_kernel, tt=tt),
        out_shape=jax.ShapeDtypeStruct((bsz, s, CONV_DIM), BF16),
        grid=(bsz, s // tt),
        in_specs=[pl.BlockSpec((None, tt, 2 * CONV_DIM), lambda b, i: (b, i, 0)),
                  _resident((CONV_HALO, CONV_DIM)), row, row, row],
        out_specs=pl.BlockSpec((None, tt, CONV_DIM), lambda b, i: (b, i, 0)),
        scratch_shapes=[pltpu.VMEM((tt + CONV_HALO, CONV_DIM), F32)],
        compiler_params=_params(("parallel", "arbitrary")),
        name="conformer_conv",
    )(p_conv, wpad, conv_b.reshape(1, -1), ln_g.reshape(1, -1), ln_b.reshape(1, -1))


def _mla_prep_kernel(p_ref, cos_ref, sin_ref, qg_ref, kvg_ref, wq_ref, wqs_ref, wkv_ref,
                     q_ref, k_ref, v_ref):
    p = p_ref[...]
    cos = cos_ref[...]
    sin = sin_ref[...]
    qc = (_rms(p[:, 0:Q_LORA], NORM_EPS) * qg_ref[...]).astype(BF16)
    kvc = (_rms(p[:, Q_LORA:Q_LORA + KV_LORA], NORM_EPS) * kvg_ref[...]).astype(BF16)
    q = _dot(qc, wq_ref[...])
    qs = _dot(qc, wqs_ref[...])
    kv = _dot(kvc, wkv_ref[...])
    o_kr = Q_LORA + KV_LORA
    k_rope = (p[:, o_kr:o_kr + LANE] * cos + p[:, o_kr + LANE:o_kr + 2 * LANE] * sin).astype(BF16)
    for h in range(MLA_HEADS):
        b0 = h * QK_PAD
        q_ref[:, b0:b0 + LANE] = q[:, b0:b0 + LANE].astype(BF16)
        q_ref[:, b0 + LANE:b0 + 2 * LANE] = (
            q[:, b0 + LANE:b0 + 2 * LANE] * cos + qs[:, h * LANE:(h + 1) * LANE] * sin).astype(BF16)
        k_ref[:, b0:b0 + LANE] = kv[:, b0:b0 + LANE].astype(BF16)
        k_ref[:, b0 + LANE:b0 + 2 * LANE] = k_rope
        v_ref[:, h * V_HEAD:(h + 1) * V_HEAD] = kv[:, b0 + LANE:b0 + 2 * LANE].astype(BF16)


def _mla_prep_call(p_mla, cos, sin, qg, kvg, wq, wqs, wkv):
    bsz, s, _ = p_mla.shape
    tm = min(s, 256)
    tok = lambda n: pl.BlockSpec((None, tm, n), lambda b, i: (b, i, 0))
    return pl.pallas_call(
        _mla_prep_kernel,
        out_shape=(jax.ShapeDtypeStruct((bsz, s, MLA_HEADS * QK_PAD), BF16),
                   jax.ShapeDtypeStruct((bsz, s, MLA_HEADS * QK_PAD), BF16),
                   jax.ShapeDtypeStruct((bsz, s, MLA_DIM), BF16)),
        grid=(bsz, s // tm),
        in_specs=[tok(MLA_IN_PAD), tok(LANE), tok(LANE),
                  _resident((1, Q_LORA)), _resident((1, KV_LORA)),
                  _resident(wq.shape), _resident(wqs.shape), _resident(wkv.shape)],
        out_specs=(tok(MLA_HEADS * QK_PAD), tok(MLA_HEADS * QK_PAD), tok(MLA_DIM)),
        compiler_params=_params(("parallel", "parallel")),
        name="mla_prep",
    )(p_mla, cos, sin, qg, kvg, wq, wqs, wkv)


def _attn_kernel(q_ref, k_ref, v_ref, o_ref, *, tq):
    i = pl.program_id(2)
    q = q_ref[...]

    def step(j, carry, masked):
        m, l, acc = carry
        rows = pl.ds(pl.multiple_of(j * tq, tq), tq)
        sc = _dot_nt(q, k_ref[rows, :])
        if masked:
            q_chunk = _iota((tq, tq), 0) // CHUNK
            k_chunk = _iota((tq, tq), 1) // CHUNK
            sc = jnp.where(k_chunk <= q_chunk, sc, MASK_VALUE)
        m_new = jnp.maximum(m, jnp.max(sc, axis=-1, keepdims=True))
        alpha = jnp.exp(m - m_new)
        pr = jnp.exp(sc - m_new)
        l = alpha * l + jnp.sum(pr, axis=-1, keepdims=True)
        acc = alpha * acc + _dot(pr.astype(BF16), v_ref[rows, :])
        return m_new, l, acc

    carry = (jnp.full((tq, 1), MASK_VALUE, F32), jnp.zeros((tq, 1), F32),
             jnp.zeros((tq, V_HEAD), F32))
    carry = lax.fori_loop(0, i, lambda j, c: step(j, c, False), carry)
    _, l, acc = step(i, carry, True)
    o_ref[...] = (acc / l).astype(o_ref.dtype)


def _attn_call(q, k, v):
    bsz, s, _ = v.shape
    tq = min(s, 256)
    return pl.pallas_call(
        functools.partial(_attn_kernel, tq=tq),
        out_shape=jax.ShapeDtypeStruct((bsz, s, MLA_DIM), BF16),
        grid=(bsz, MLA_HEADS, s // tq),
        in_specs=[pl.BlockSpec((None, tq, QK_PAD), lambda b, h, i: (b, i, h)),
                  pl.BlockSpec((None, s, QK_PAD), lambda b, h, i: (b, 0, h)),
                  pl.BlockSpec((None, s, V_HEAD), lambda b, h, i: (b, 0, h))],
        out_specs=pl.BlockSpec((None, tq, V_HEAD), lambda b, h, i: (b, i, h)),
        compiler_params=_params(("parallel", "parallel", "arbitrary")),
        name="mla_attention",
    )(q, k, v)


def _outproj_kernel(ya_ref, yb_ref, yc_ref, x_ref, wo_ref, gt_ref, g_ref, sc_ref, sh_ref,
                    x1_ref, h2_ref):
    o1, o2 = RWKV_DIM, RWKV_DIM + CONV_DIM
    y = (_dot(ya_ref[...], wo_ref[0:o1, :]) + _dot(yb_ref[...], wo_ref[o1:o2, :])
         + _dot(yc_ref[...], wo_ref[o2:o2 + MLA_DIM, :]))
    x1 = x_ref[...] + gt_ref[...] * y
    x1_ref[...] = x1
    h2_ref[...] = (_rms(x1, NORM_EPS) * g_ref[...] * (1.0 + sc_ref[...]) + sh_ref[...]).astype(BF16)


def _outproj_call(ya, yb, yc, x, wo, gt, g, sc, sh):
    bsz, s, d = x.shape
    tm = min(s, 512)
    tok = lambda n: pl.BlockSpec((None, tm, n), lambda b, i: (b, i, 0))
    per_b = pl.BlockSpec((None, 1, d), lambda b, i: (b, 0, 0))
    return pl.pallas_call(
        _outproj_kernel,
        out_shape=(jax.ShapeDtypeStruct((bsz, s, d), F32), jax.ShapeDtypeStruct((bsz, s, d), BF16)),
        grid=(bsz, s // tm),
        in_specs=[tok(RWKV_DIM), tok(CONV_DIM), tok(MLA_DIM), tok(d), _resident(wo.shape),
                  per_b, _resident((1, d)), per_b, per_b],
        out_specs=(tok(d), tok(d)),
        compiler_params=_params(("parallel", "parallel")),
        name="outproj_norm2",
    )(ya, yb, yc, x, wo, gt, g, sc, sh)


def _mlp_kernel(h_ref, x_ref, w1_ref, w2_ref, gt_ref, fg_ref, o_ref, acc_ref, *, final):
    f = pl.program_id(2)

    @pl.when(f == 0)
    def _():
        acc_ref[...] = jnp.zeros_like(acc_ref)

    a = jnp.maximum(_dot(h_ref[...], w1_ref[...]), 0.0)
    acc_ref[...] += _dot((a * a).astype(BF16), w2_ref[...])

    @pl.when(f == pl.num_programs(2) - 1)
    def _():
        xo = x_ref[...] + gt_ref[...] * acc_ref[...]
        if final:
            xo = _rms(xo, NORM_EPS) * fg_ref[...]
        o_ref[...] = xo


def _mlp_call(h2, x1, w1, w2, gt, fg, final):
    bsz, s, d = x1.shape
    tm = min(s, 512)
    tf = 512
    tok = pl.BlockSpec((None, tm, d), lambda b, i, f: (b, i, 0))
    return pl.pallas_call(
        functools.partial(_mlp_kernel, final=final),
        out_shape=jax.ShapeDtypeStruct((bsz, s, d), F32),
        grid=(bsz, s // tm, D_FF // tf),
        in_specs=[tok, tok,
                  pl.BlockSpec((d, tf), lambda b, i, f: (0, f)),
                  pl.BlockSpec((tf, d), lambda b, i, f: (f, 0)),
                  pl.BlockSpec((None, 1, d), lambda b, i, f: (b, 0, 0)),
                  pl.BlockSpec((1, d), lambda b, i, f: (0, 0))],
        out_specs=tok,
        scratch_shapes=[pltpu.VMEM((tm, d), F32)],
        compiler_params=_params(("parallel", "parallel", "arbitrary")),
        name="relu2_mlp",
    )(h2, x1, w1, w2, gt, fg)


def _pad_cols(w, n):
    return jnp.pad(w, ((0, 0), (0, n - w.shape[1])))


def _rotate_half_cols(w):
    half = w.shape[-1] // 2
    return jnp.concatenate([-w[..., half:], w[..., :half]], axis=-1)


def _layer_weights(l, w_in, w_in_vres, rwkv_mu, vres_mu, decay_up, iclr_up, gate_up, vres_up,
                   w_qb, w_kvb):
    d = D_MODEL
    wl = w_in[l]
    o_lora = 3 * RWKV_DIM
    o_conv = RWKV_IN
    o_mla = RWKV_IN + CONV_IN
    has_vres = l > 0
    lora_cols = [wl[:, o_lora:o_conv]]
    mu_cols = [rwkv_mu[l, o_lora:o_conv]]
    if has_vres:
        lora_cols.append(w_in_vres[l - 1])
        mu_cols.append(vres_mu[l - 1])
    w_lora = _pad_cols(jnp.concatenate(lora_cols, axis=1), LORA_PAD)
    mu_lora = jnp.pad(jnp.concatenate(mu_cols), (0, LORA_PAD - sum(m.shape[0] for m in mu_cols)))
    mla = wl[:, o_mla:N_IN]
    kr = mla[:, Q_LORA + KV_LORA:]
    zpad = jnp.zeros((d, LANE - QK_ROPE), F32)
    w_mla = jnp.concatenate([mla[:, :Q_LORA + KV_LORA], kr, zpad, _rotate_half_cols(kr), zpad], axis=1)

    up = jnp.zeros((LORA_PAD, 4 * RWKV_DIM), F32)
    o_a, o_g = DECAY_LORA, DECAY_LORA + ICLR_LORA
    o_v = o_g + GATE_LORA
    up = up.at[0:o_a, 0:RWKV_DIM].set(decay_up[l])
    up = up.at[o_a:o_g, RWKV_DIM:2 * RWKV_DIM].set(iclr_up[l])
    up = up.at[o_g:o_v, 2 * RWKV_DIM:3 * RWKV_DIM].set(gate_up[l])
    if has_vres:
        up = up.at[o_v:o_v + VRES_LORA, 3 * RWKV_DIM:].set(vres_up[l - 1])

    scale = (QK_NOPE + QK_ROPE) ** -0.5
    wq = (w_qb[l] * scale).reshape(Q_LORA, MLA_HEADS, QK_NOPE + QK_ROPE)
    nope, rope = wq[:, :, :QK_NOPE], wq[:, :, QK_NOPE:]
    z = jnp.zeros((Q_LORA, MLA_HEADS, LANE - QK_ROPE), F32)
    wq_main = jnp.concatenate([nope, rope, z], axis=2).reshape(Q_LORA, MLA_HEADS * QK_PAD)
    wq_rot = jnp.concatenate([_rotate_half_cols(rope), z], axis=2).reshape(Q_LORA, MLA_HEADS * LANE)
    return dict(
        w_rkv=wl[:, :o_lora].astype(BF16), w_lora=w_lora.astype(BF16),
        w_conv=wl[:, o_conv:o_mla].astype(BF16), w_mla=w_mla.astype(BF16),
        mu_rkv=rwkv_mu[l, :o_lora].reshape(1, -1), mu_lora=mu_lora.reshape(1, -1),
        w_lora_up=up, wq=wq_main.astype(BF16), wq_rot=wq_rot.astype(BF16),
        wkv=w_kvb[l].astype(BF16))


def kernel(x, c, positions, ada_w, ada_b, norm1_g, norm2_g, final_g, w_in, w_in_vres, rwkv_mu,
           vres_mu, decay_w0, decay_up, iclr_a0, iclr_up, gate_up, vres_v0, vres_up, k_k, k_a, r_k,
           lnx_g, lnx_b, conv_w, conv_b, conv_ln_g, conv_ln_b, q_a_norm_g, w_qb, kv_a_norm_g, w_kvb,
           w_out, mlp_w1, mlp_w2):
    bsz, s, d = x.shape
    mod = _ada_call(c, ada_w, ada_b).reshape(DEPTH, bsz, 6, 1, d)
    cos, sin = _rope_call(positions)
    row = lambda a: a.reshape(1, -1)
    v_first = None
    for l in range(DEPTH):
        sh1, sc1, gt1, sh2, sc2, gt2 = (mod[l, :, j] for j in range(6))
        lw = _layer_weights(l, w_in, w_in_vres, rwkv_mu, vres_mu, decay_up, iclr_up, gate_up,
                            vres_up, w_qb, w_kvb)
        p_rkv, p_lora, p_conv, p_mla = _inproj_call(
            x, row(norm1_g[l]), sc1, sh1, lw["w_rkv"], lw["w_lora"], lw["w_conv"], lw["w_mla"])
        prm = dict(mu_rkv=lw["mu_rkv"], mu_lora=lw["mu_lora"], w_lora_up=lw["w_lora_up"],
                   w0=row(decay_w0[l]), a0=row(iclr_a0[l]), k_k=row(k_k[l]), k_a=row(k_a[l]),
                   r_k=row(r_k[l]), lnx_g=row(lnx_g[l]), lnx_b=row(lnx_b[l]))
        if l == 0:
            y_a, v_first = _rwkv_call(p_rkv, p_lora, None, prm)
        else:
            prm["v0"] = row(vres_v0[l - 1])
            y_a = _rwkv_call(p_rkv, p_lora, v_first, prm)
        y_b = _conv_call(p_conv, conv_w[l], conv_b[l], conv_ln_g[l], conv_ln_b[l])
        q, k, v = _mla_prep_call(p_mla, cos, sin, row(q_a_norm_g[l]), row(kv_a_norm_g[l]),
                                 lw["wq"], lw["wq_rot"], lw["wkv"])
        y_c = _attn_call(q, k, v)
        x1, h2 = _outproj_call(y_a, y_b, y_c, x, w_out[l].astype(BF16), gt1,
                               row(norm2_g[l]), sc2, sh2)
        x = _mlp_call(h2, x1, mlp_w1[l].astype(BF16), mlp_w2[l].astype(BF16), gt2,
                      row(final_g), final=(l == DEPTH - 1))
    return x
```

```python
import functools

import jax
import jax.numpy as jnp
from jax import lax
from jax.experimental import pallas as pl
from jax.experimental.pallas import tpu as pltpu

F32 = jnp.float32
BF16 = jnp.bfloat16

D_MODEL = 2048
DEPTH = 2
CHUNK = 64
NORM_EPS = 1e-6
LN_EPS = 1e-5
D_FF = 4 * D_MODEL

RWKV_HEAD = 64
RWKV_DIM = 512
RWKV_HEADS = 8
RWKV_PAIRS = RWKV_HEADS // 2
DECAY_LORA = 32
ICLR_LORA = 32
VRES_LORA = 32
GATE_LORA = 96
RWKV_GN_EPS = 64e-5
LORA_PAD = 256
RWKV_DECAY_SCALE = 0.6065306597126334
RWKV_CHUNKS_PER_ITER = 2

CONV_DIM = 512
CONV_WIDTH = 31
CONV_HALO = 32

MLA_DIM = 1024
V_HEAD = 128
MLA_HEADS = 8
QK_NOPE = 128
QK_ROPE = 64
Q_LORA = 512
KV_LORA = 256
ROPE_THETA = 10000.0
QK_PAD = 256
ATTN_HEADS_PER_STEP = 2
MLA_IN_PAD = 1024

RWKV_IN = 3 * RWKV_DIM + DECAY_LORA + ICLR_LORA + GATE_LORA
CONV_IN = 2 * CONV_DIM
N_IN = RWKV_IN + CONV_IN + Q_LORA + KV_LORA + QK_ROPE

LANE = 128
V7X_VMEM_LIMIT = 56 * 1024 * 1024
MASK_VALUE = -1e30
LOG2_E = 1.4426950408889634


def _params(semantics, vmem=V7X_VMEM_LIMIT):
    return pltpu.CompilerParams(dimension_semantics=semantics, vmem_limit_bytes=vmem)


def _resident(shape):
    nd = len(shape)
    return pl.BlockSpec(shape, lambda *_: (0,) * nd, pipeline_mode=pl.Buffered(1))


def _dot(a, b):
    return jnp.dot(a, b, preferred_element_type=F32)


def _dot_nt(a, b):
    return lax.dot_general(a, b, (((1,), (1,)), ((), ())), preferred_element_type=F32)


def _split(x):
    hi = x.astype(BF16)
    lo = (x - hi.astype(F32)).astype(BF16)
    return hi, lo


def _dot3(a, b):
    ah, al = _split(a)
    bh, bl = _split(b)
    return _dot(jnp.concatenate([ah, al, ah], axis=1), jnp.concatenate([bh, bh, bl], axis=0))


def _dot3_nt(a, b):
    ah, al = _split(a)
    bh, bl = _split(b)
    return _dot_nt(jnp.concatenate([ah, al, ah], axis=1), jnp.concatenate([bh, bh, bl], axis=1))


def _dot2_exact_rhs(a, b_bf16):
    ah, al = _split(a)
    return _dot(jnp.concatenate([ah, al], axis=1), jnp.concatenate([b_bf16] * 2, axis=0))


def _rms(x, eps):
    return x * lax.rsqrt(jnp.mean(x * x, axis=-1, keepdims=True) + eps)


def _sigmoid(x):
    return 1.0 / (1.0 + jnp.exp(-x))


def _iota(shape, dim):
    return lax.broadcasted_iota(jnp.int32, shape, dim)


def _ada_kernel(c_ref, w_ref, b_ref, o_ref):
    c = c_ref[...]
    o_ref[...] = _dot3(c * _sigmoid(c), w_ref[...]) + b_ref[...]


def _ada_call(c, ada_w, ada_b):
    depth, d, n = ada_w.shape
    bsz = c.shape[0]
    tn = 512
    return pl.pallas_call(
        _ada_kernel,
        out_shape=jax.ShapeDtypeStruct((depth, bsz, n), F32),
        grid=(depth, n // tn),
        in_specs=[
            pl.BlockSpec((bsz, d), lambda l, j: (0, 0)),
            pl.BlockSpec((None, d, tn), lambda l, j: (l, 0, j)),
            pl.BlockSpec((None, 1, tn), lambda l, j: (l, 0, j)),
        ],
        out_specs=pl.BlockSpec((None, bsz, tn), lambda l, j: (l, 0, j)),
        compiler_params=_params(("parallel", "parallel")),
        name="ada_mod",
    )(c, ada_w, ada_b.reshape(depth, 1, n))


def _rope_kernel(pos_ref, invf_ref, cos_ref, sin_ref):
    ang = pos_ref[...] * invf_ref[...]
    cos_ref[...] = jnp.cos(ang)
    sin_ref[...] = jnp.sin(ang)


def _rope_call(positions):
    bsz, s = positions.shape
    ts = min(s, 1024)
    inv_freq = ROPE_THETA ** (-jnp.arange(0, QK_ROPE, 2, dtype=F32) / QK_ROPE)
    invf = jnp.tile(inv_freq, LANE // (QK_ROPE // 2)).reshape(1, LANE)
    pos = positions.astype(F32).reshape(bsz, s, 1)
    shp = jax.ShapeDtypeStruct((bsz, s, LANE), F32)
    return pl.pallas_call(
        _rope_kernel,
        out_shape=(shp, shp),
        grid=(bsz, s // ts),
        in_specs=[
            pl.BlockSpec((None, ts, 1), lambda b, i: (b, i, 0)),
            pl.BlockSpec((1, LANE), lambda b, i: (0, 0)),
        ],
        out_specs=(
            pl.BlockSpec((None, ts, LANE), lambda b, i: (b, i, 0)),
            pl.BlockSpec((None, ts, LANE), lambda b, i: (b, i, 0)),
        ),
        compiler_params=_params(("parallel", "parallel")),
        name="rope_tables",
    )(pos, invf)


def _inproj_kernel(x_ref, g_ref, sc_ref, sh_ref, w_rkv, w_lora, w_conv, w_mla,
                   o_rkv, o_lora, o_conv, o_mla):
    x = x_ref[...]
    h = _rms(x, NORM_EPS) * g_ref[...] * (1.0 + sc_ref[...]) + sh_ref[...]
    hb = h.astype(BF16)
    o_rkv[...] = _dot(hb, w_rkv[...])
    o_lora[...] = _dot(hb, w_lora[...])
    o_conv[...] = _dot(hb, w_conv[...])
    o_mla[...] = _dot(hb, w_mla[...])


def _inproj_call(x, g, sc, sh, w_rkv, w_lora, w_conv, w_mla):
    bsz, s, d = x.shape
    tm = min(s, 256)
    widths = (w_rkv.shape[1], w_lora.shape[1], w_conv.shape[1], w_mla.shape[1])
    tok = lambda n: pl.BlockSpec((None, tm, n), lambda b, i: (b, i, 0))
    per_b = pl.BlockSpec((None, 1, d), lambda b, i: (b, 0, 0))
    return pl.pallas_call(
        _inproj_kernel,
        out_shape=tuple(jax.ShapeDtypeStruct((bsz, s, n), F32) for n in widths),
        grid=(bsz, s // tm),
        in_specs=[tok(d), _resident((1, d)), per_b, per_b,
                  _resident(w_rkv.shape), _resident(w_lora.shape),
                  _resident(w_conv.shape), _resident(w_mla.shape)],
        out_specs=tuple(tok(n) for n in widths),
        compiler_params=_params(("parallel", "parallel")),
        name="norm_inproj",
    )(x, g, sc, sh, w_rkv, w_lora, w_conv, w_mla)


def _block_diag(y):
    lane = _iota(y.shape, 1)
    zero = jnp.zeros_like(y)
    return jnp.concatenate([jnp.where(lane < RWKV_HEAD, y, zero),
                            jnp.where(lane >= RWKV_HEAD, y, zero)], axis=0)


def _packed_mm(x, ys):
    xh, xl = _split(x)
    lhs = jnp.concatenate([xh, xl, xh], axis=1)
    cols = []
    for y in ys:
        yh, yl = _split(y)
        cols.append(jnp.concatenate([_block_diag(yh), _block_diag(yh), _block_diag(yl)], axis=0))
    out = _dot(lhs, cols[0] if len(cols) == 1 else jnp.concatenate(cols, axis=1))
    return [out[:, i * LANE:(i + 1) * LANE] for i in range(len(ys))]


def _head_sum_matrix(scale):
    r = _iota((LANE, LANE), 0) // RWKV_HEAD
    c = _iota((LANE, LANE), 1) // RWKV_HEAD
    return jnp.where(r == c, scale, 0.0).astype(BF16)


def _rwkv_kernel(*refs, has_vres, tt):
    if has_vres:
        (prkv_ref, plora_ref, vfirst_ref, mu_rkv, mu_lora, wc_ref, w0_ref, a0_ref, v0_ref,
         kk_ref, ka_ref, rk_ref, lng_ref, lnb_ref, y_ref,
         prev_rkv, prev_lora, h_scr, at_s, rt_s, bt_s, kt_s, bp_s, kp_s, v_s, pf_s, o_s) = refs
    else:
        (prkv_ref, plora_ref, mu_rkv, mu_lora, wc_ref, w0_ref, a0_ref,
         kk_ref, ka_ref, rk_ref, lng_ref, lnb_ref, y_ref, vout_ref,
         prev_rkv, prev_lora, h_scr, at_s, rt_s, bt_s, kt_s, bp_s, kp_s, v_s, pf_s, o_s) = refs
    n_chunks = tt // CHUNK

    @pl.when(pl.program_id(1) == 0)
    def _():
        prev_rkv[...] = jnp.zeros_like(prev_rkv)
        prev_lora[...] = jnp.zeros_like(prev_lora)
        h_scr[...] = jnp.zeros_like(h_scr)

    def shift_mix(p_ref, prev_ref, mu_ref):
        p = p_ref[...]
        row = _iota(p.shape, 0)
        prev = jnp.where(row == 0, prev_ref[0:1, :], pltpu.roll(p, 1, 0))
        prev_ref[0:1, :] = p[tt - 1:tt, :]
        return p + (prev - p) * mu_ref[...]

    xs = shift_mix(prkv_ref, prev_rkv, mu_rkv)
    xl = shift_mix(plora_ref, prev_lora, mu_lora)
    r = xs[:, 0:RWKV_DIM]
    k = xs[:, RWKV_DIM:2 * RWKV_DIM]
    v = xs[:, 2 * RWKV_DIM:3 * RWKV_DIM]

    lane = _iota(xl.shape, 1)
    o_w, o_a, o_g = DECAY_LORA, DECAY_LORA + ICLR_LORA, DECAY_LORA + ICLR_LORA + GATE_LORA
    act = jnp.where(lane < o_w, jnp.tanh(xl),
                    jnp.where((lane >= o_a) & (lane < o_g), _sigmoid(xl), xl))
    act_h, act_l = _split(act)
    lora = _dot(jnp.concatenate([act_h, act_l, act_h], axis=1), wc_ref[...])
    z = w0_ref[...] + lora[:, 0:RWKV_DIM]
    lw = -RWKV_DECAY_SCALE * _sigmoid(z)
    a_ic = _sigmoid(a0_ref[...] + lora[:, RWKV_DIM:2 * RWKV_DIM])
    gate = lora[:, 2 * RWKV_DIM:3 * RWKV_DIM]
    if has_vres:
        v = v + (vfirst_ref[...] - v) * _sigmoid(v0_ref[...] + lora[:, 3 * RWKV_DIM:4 * RWKV_DIM])
    else:
        vout_ref[...] = v

    ones_bd = _head_sum_matrix(1.0)
    mean_bd = _head_sum_matrix(1.0 / RWKV_HEAD)

    def per_head(x, mat):
        return jnp.concatenate(
            [_dot2_exact_rhs(x[:, p * LANE:(p + 1) * LANE], mat) for p in range(RWKV_PAIRS)], axis=1)

    kk = k * kk_ref[...]
    kk = kk * lax.rsqrt(jnp.maximum(per_head(kk * kk, ones_bd), 1e-24))
    k2 = k * (1.0 + (a_ic - 1.0) * ka_ref[...])
    a_vec = -kk
    b_vec = kk * a_ic
    bonus = per_head(r * k2 * rk_ref[...], ones_bd) * v

    tr = _iota((tt, tt), 0)
    tc = _iota((tt, tt), 1)
    tri = jnp.where((tr // CHUNK == tc // CHUNK) & (tc <= tr), 1.0, 0.0).astype(BF16)
    cum = _dot2_exact_rhs_left(tri, lw)
    cum3 = cum.reshape(n_chunks, CHUNK, RWKV_DIM)
    cum_end = jnp.broadcast_to(cum3[:, CHUNK - 1:CHUNK, :], cum3.shape).reshape(tt, RWKV_DIM)
    pf = jnp.exp(cum)
    pinv = jnp.exp(-cum)
    pprev = jnp.exp(cum - lw)
    pend = jnp.exp(cum_end - cum)

    def put(dst, val):
        for p in range(RWKV_PAIRS):
            dst[p] = val[:, p * LANE:(p + 1) * LANE]

    put(at_s, a_vec * pprev)
    put(rt_s, r * pf)
    put(bt_s, b_vec * pinv)
    put(kt_s, k2 * pinv)
    put(bp_s, b_vec * pend)
    put(kp_s, k2 * pend)
    put(v_s, v)
    put(pf_s, pf)

    t_loc = _iota((CHUNK, LANE), 0)
    s_loc = _iota((CHUNK, LANE), 1) % RWKV_HEAD
    strict = s_loc < t_loc
    incl = s_loc <= t_loc
    eye_p = jnp.where(s_loc == t_loc, 1.0, 0.0)
    rr = _iota((LANE, LANE), 0)
    cc = _iota((LANE, LANE), 1)
    same_head = (rr // RWKV_HEAD) == (cc // RWKV_HEAD)
    diag = rr == cc

    group = min(n_chunks, RWKV_CHUNKS_PER_ITER)

    def chunk_group(cg, carry):
        streams = []
        for ci in range(group):
            c = cg * group + ci
            rows = pl.ds(pl.multiple_of(c * CHUNK, CHUNK), CHUNK)
            last = pl.ds(c * CHUNK + CHUNK - 1, 1)
            streams += [(p, rows, last) for p in range(RWKV_PAIRS)]
        ns = range(len(streams))
        load = lambda ref: [ref[p, rows, :] for (p, rows, _) in streams]
        at, rt, bt, kt, bp, kp, vv = (load(r) for r in (at_s, rt_s, bt_s, kt_s, bp_s, kp_s, v_s))
        g = [_dot3_nt(jnp.concatenate([at[i], rt[i]], axis=0),
                      jnp.concatenate([_block_diag(bt[i]), _block_diag(kt[i])], axis=0)) for i in ns]
        aab = [jnp.where(strict, g[i][0:CHUNK, 0:LANE], 0.0) for i in ns]
        aak = [jnp.where(strict, g[i][0:CHUNK, LANE:2 * LANE], 0.0) for i in ns]
        arb = [jnp.where(incl, g[i][CHUNK:2 * CHUNK, 0:LANE], 0.0) for i in ns]
        ark = [jnp.where(incl, g[i][CHUNK:2 * CHUNK, LANE:2 * LANE], 0.0) for i in ns]
        npow = [_packed_mm(aab[i], [aab[i]])[0] for i in ns]
        tinv = [eye_p + aab[i] for i in ns]
        akv = [_packed_mm(aak[i], [vv[i]])[0] for i in ns]
        arkv = [_packed_mm(ark[i], [vv[i]])[0] for i in ns]
        for _ in range(4):
            res = [_packed_mm(npow[i], [npow[i], tinv[i]]) for i in ns]
            npow = [res[i][0] for i in ns]
            tinv = [tinv[i] + res[i][1] for i in ns]
        tinv = [tinv[i] + _packed_mm(npow[i], [tinv[i]])[0] for i in ns]
        wu = [_packed_mm(tinv[i], [at[i], akv[i]]) for i in ns]
        ab = [_packed_mm(arb[i], wu[i]) for i in ns]
        q_m = [rt[i] + ab[i][0] for i in ns]
        y0 = [ab[i][1] + arkv[i] for i in ns]
        upd = [_dot3(jnp.concatenate([bp[i], kp[i]], axis=0).T,
                     jnp.concatenate([jnp.concatenate(wu[i], axis=1),
                                      jnp.concatenate([jnp.zeros_like(vv[i]), vv[i]], axis=1)], axis=0))
               for i in ns]
        h = [h_scr[p] for p in range(RWKV_PAIRS)]
        for i in ns:
            p, rows, last = streams[i]
            m_bd = jnp.where(same_head, upd[i][:, 0:LANE], 0.0) + jnp.where(diag, pf_s[p, last, :], 0.0)
            qh_mh = _dot3(jnp.concatenate([q_m[i], m_bd], axis=0), h[p])
            o_s[rows, p * LANE:(p + 1) * LANE] = qh_mh[0:CHUNK] + y0[i]
            h[p] = qh_mh[CHUNK:CHUNK + LANE] + jnp.where(same_head, upd[i][:, LANE:2 * LANE], 0.0)
        for p in range(RWKV_PAIRS):
            h_scr[p] = h[p]
        return carry

    lax.fori_loop(0, n_chunks // group, chunk_group, 0)

    o = o_s[...]
    mean = per_head(o, mean_bd)
    oc = o - mean
    var = per_head(oc * oc, mean_bd)
    on = oc * lax.rsqrt(var + RWKV_GN_EPS) * lng_ref[...] + lnb_ref[...]
    y_ref[...] = ((on + bonus) * gate).astype(y_ref.dtype)


def _dot2_exact_rhs_left(m_bf16, x):
    xh = x.astype(BF16)
    r1 = x - xh.astype(F32)
    xm = r1.astype(BF16)
    xl = (r1 - xm.astype(F32)).astype(BF16)
    return _dot(jnp.concatenate([m_bf16] * 3, axis=1), jnp.concatenate([xh, xm, xl], axis=0))


def _rwkv_call(p_rkv, p_lora, v_first, prm):
    bsz, s, _ = p_rkv.shape
    tt = min(s, 256)
    has_vres = v_first is not None
    tok = lambda n: pl.BlockSpec((None, tt, n), lambda b, i: (b, i, 0))
    row = lambda n: _resident((1, n))
    in_specs = [tok(3 * RWKV_DIM), tok(LORA_PAD)]
    args = [p_rkv, p_lora]
    if has_vres:
        in_specs.append(tok(RWKV_DIM))
        args.append(v_first)
    in_specs += [row(3 * RWKV_DIM), row(LORA_PAD), _resident((3 * LORA_PAD, 4 * RWKV_DIM)),
                 row(RWKV_DIM), row(RWKV_DIM)]
    args += [prm["mu_rkv"], prm["mu_lora"], prm["w_lora_up"], prm["w0"], prm["a0"]]
    if has_vres:
        in_specs.append(row(RWKV_DIM))
        args.append(prm["v0"])
    in_specs += [row(RWKV_DIM)] * 5
    args += [prm["k_k"], prm["k_a"], prm["r_k"], prm["lnx_g"], prm["lnx_b"]]
    y_shape = jax.ShapeDtypeStruct((bsz, s, RWKV_DIM), BF16)
    if has_vres:
        out_shape, out_specs = y_shape, tok(RWKV_DIM)
    else:
        out_shape = (y_shape, jax.ShapeDtypeStruct((bsz, s, RWKV_DIM), F32))
        out_specs = (tok(RWKV_DIM), tok(RWKV_DIM))
    pair_tile = pltpu.VMEM((RWKV_PAIRS, tt, LANE), F32)
    out = pl.pallas_call(
        functools.partial(_rwkv_kernel, has_vres=has_vres, tt=tt),
        out_shape=out_shape,
        grid=(bsz, s // tt),
        in_specs=in_specs,
        out_specs=out_specs,
        scratch_shapes=[
            pltpu.VMEM((8, 3 * RWKV_DIM), F32),
            pltpu.VMEM((8, LORA_PAD), F32),
            pltpu.VMEM((RWKV_PAIRS, LANE, LANE), F32),
        ] + [pair_tile] * 8 + [pltpu.VMEM((tt, RWKV_DIM), F32)],
        compiler_params=_params(("parallel", "arbitrary")),
        name="rwkv7_mixer",
    )(*args)
    return out if has_vres else out


def _conv_kernel(p_ref, w_ref, b_ref, g_ref, be_ref, o_ref, ubuf, *, tt):
    @pl.when(pl.program_id(1) == 0)
    def _():
        ubuf[0:CONV_HALO, :] = jnp.zeros((CONV_HALO, CONV_DIM), F32)

    p = p_ref[...]
    ubuf[CONV_HALO:CONV_HALO + tt, :] = p[:, 0:CONV_DIM] * _sigmoid(p[:, CONV_DIM:2 * CONV_DIM])
    acc = jnp.zeros((tt, CONV_DIM), F32) + b_ref[...]
    base = CONV_HALO - (CONV_WIDTH - 1)
    for j in range(CONV_WIDTH):
        acc = acc + ubuf[base + j:base + j + tt, :] * w_ref[j:j + 1, :]
    ubuf[0:CONV_HALO, :] = ubuf[tt:tt + CONV_HALO, :]
    mean = jnp.mean(acc, axis=-1, keepdims=True)
    xc = acc - mean
    var = jnp.mean(xc * xc, axis=-1, keepdims=True)
    u = xc * lax.rsqrt(var + LN_EPS) * g_ref[...] + be_ref[...]
    o_ref[...] = (u * _sigmoid(u)).astype(o_ref.dtype)


def _conv_call(p_conv, conv_w, conv_b, ln_g, ln_b):
    bsz, s, _ = p_conv.shape
    tt = min(s, 512)
    wpad = jnp.zeros((CONV_HALO, CONV_DIM), F32).at[:CONV_WIDTH].set(conv_w)
    row = _resident((1, CONV_DIM))
    return pl.pallas_call(
        functools.partial(_conv_kernel, tt=tt),
        out_shape=jax.ShapeDtypeStruct((bsz, s, CONV_DIM), BF16),
        grid=(bsz, s // tt),
        in_specs=[pl.BlockSpec((None, tt, 2 * CONV_DIM), lambda b, i: (b, i, 0)),
                  _resident((CONV_HALO, CONV_DIM)), row, row, row],
        out_specs=pl.BlockSpec((None, tt, CONV_DIM), lambda b, i: (b, i, 0)),
        scratch_shapes=[pltpu.VMEM((tt + CONV_HALO, CONV_DIM), F32)],
        compiler_params=_params(("parallel", "arbitrary")),
        name="conformer_conv",
    )(p_conv, wpad, conv_b.reshape(1, -1), ln_g.reshape(1, -1), ln_b.reshape(1, -1))


def _mla_prep_kernel(p_ref, cos_ref, sin_ref, qg_ref, kvg_ref, wq_ref, wqs_ref, wkv_ref,
                     q_ref, k_ref, v_ref):
    p = p_ref[...]
    cos = cos_ref[...]
    sin = sin_ref[...]
    qc = (_rms(p[:, 0:Q_LORA], NORM_EPS) * qg_ref[...]).astype(BF16)
    kvc = (_rms(p[:, Q_LORA:Q_LORA + KV_LORA], NORM_EPS) * kvg_ref[...]).astype(BF16)
    q = _dot(qc, wq_ref[...])
    qs = _dot(qc, wqs_ref[...])
    kv = _dot(kvc, wkv_ref[...])
    o_kr = Q_LORA + KV_LORA
    k_rope = (p[:, o_kr:o_kr + LANE] * cos + p[:, o_kr + LANE:o_kr + 2 * LANE] * sin).astype(BF16)
    for h in range(MLA_HEADS):
        b0 = h * QK_PAD
        q_ref[:, b0:b0 + LANE] = q[:, b0:b0 + LANE].astype(BF16)
        q_ref[:, b0 + LANE:b0 + 2 * LANE] = (
            q[:, b0 + LANE:b0 + 2 * LANE] * cos + qs[:, h * LANE:(h + 1) * LANE] * sin).astype(BF16)
        k_ref[:, b0:b0 + LANE] = kv[:, b0:b0 + LANE].astype(BF16)
        k_ref[:, b0 + LANE:b0 + 2 * LANE] = k_rope
        v_ref[:, h * V_HEAD:(h + 1) * V_HEAD] = kv[:, b0 + LANE:b0 + 2 * LANE].astype(BF16)


def _mla_prep_call(p_mla, cos, sin, qg, kvg, wq, wqs, wkv):
    bsz, s, _ = p_mla.shape
    tm = min(s, 256)
    tok = lambda n: pl.BlockSpec((None, tm, n), lambda b, i: (b, i, 0))
    return pl.pallas_call(
        _mla_prep_kernel,
        out_shape=(jax.ShapeDtypeStruct((bsz, s, MLA_HEADS * QK_PAD), BF16),
                   jax.ShapeDtypeStruct((bsz, s, MLA_HEADS * QK_PAD), BF16),
                   jax.ShapeDtypeStruct((bsz, s, MLA_DIM), BF16)),
        grid=(bsz, s // tm),
        in_specs=[tok(MLA_IN_PAD), tok(LANE), tok(LANE),
                  _resident((1, Q_LORA)), _resident((1, KV_LORA)),
                  _resident(wq.shape), _resident(wqs.shape), _resident(wkv.shape)],
        out_specs=(tok(MLA_HEADS * QK_PAD), tok(MLA_HEADS * QK_PAD), tok(MLA_DIM)),
        compiler_params=_params(("parallel", "parallel")),
        name="mla_prep",
    )(p_mla, cos, sin, qg, kvg, wq, wqs, wkv)


def _attn_kernel(q_ref, k_ref, v_ref, o_ref, *, tq, nh):
    i = pl.program_id(2)
    heads = range(nh)
    q = [q_ref[:, h * QK_PAD:(h + 1) * QK_PAD] for h in heads]

    def step(j, carry, masked):
        rows = pl.ds(pl.multiple_of(j * tq, tq), tq)
        sc = [_dot_nt(q[h], k_ref[rows, h * QK_PAD:(h + 1) * QK_PAD]) for h in heads]
        if masked:
            visible = (_iota((tq, tq), 1) // CHUNK) <= (_iota((tq, tq), 0) // CHUNK)
            sc = [jnp.where(visible, s, MASK_VALUE) for s in sc]
        m_new = [jnp.maximum(carry[h][0], jnp.max(sc[h], axis=-1, keepdims=True)) for h in heads]
        alpha = [jnp.exp2(carry[h][0] - m_new[h]) for h in heads]
        pr = [jnp.exp2(sc[h] - m_new[h]) for h in heads]
        l_new = [alpha[h] * carry[h][1] + jnp.sum(pr[h], axis=-1, keepdims=True) for h in heads]
        pv = [_dot(pr[h].astype(BF16), v_ref[rows, h * V_HEAD:(h + 1) * V_HEAD]) for h in heads]
        return tuple((m_new[h], l_new[h], alpha[h] * carry[h][2] + pv[h]) for h in heads)

    init = (jnp.full((tq, 1), MASK_VALUE, F32), jnp.zeros((tq, 1), F32), jnp.zeros((tq, V_HEAD), F32))
    carry = lax.fori_loop(0, i, lambda j, c: step(j, c, False), (init,) * nh)
    carry = step(i, carry, True)
    for h in heads:
        o_ref[:, h * V_HEAD:(h + 1) * V_HEAD] = (carry[h][2] / carry[h][1]).astype(o_ref.dtype)


def _attn_call(q, k, v):
    bsz, s, _ = v.shape
    tq = min(s, 512)
    nh = ATTN_HEADS_PER_STEP
    return pl.pallas_call(
        functools.partial(_attn_kernel, tq=tq, nh=nh),
        out_shape=jax.ShapeDtypeStruct((bsz, s, MLA_DIM), BF16),
        grid=(bsz, MLA_HEADS // nh, s // tq),
        in_specs=[pl.BlockSpec((None, tq, nh * QK_PAD), lambda b, h, i: (b, i, h)),
                  pl.BlockSpec((None, s, nh * QK_PAD), lambda b, h, i: (b, 0, h)),
                  pl.BlockSpec((None, s, nh * V_HEAD), lambda b, h, i: (b, 0, h))],
        out_specs=pl.BlockSpec((None, tq, nh * V_HEAD), lambda b, h, i: (b, i, h)),
        compiler_params=_params(("parallel", "parallel", "arbitrary")),
        name="mla_attention",
    )(q, k, v)


def _outproj_kernel(ya_ref, yb_ref, yc_ref, x_ref, wo_ref, gt_ref, g_ref, sc_ref, sh_ref,
                    x1_ref, h2_ref):
    o1, o2 = RWKV_DIM, RWKV_DIM + CONV_DIM
    y = (_dot(ya_ref[...], wo_ref[0:o1, :]) + _dot(yb_ref[...], wo_ref[o1:o2, :])
         + _dot(yc_ref[...], wo_ref[o2:o2 + MLA_DIM, :]))
    x1 = x_ref[...] + gt_ref[...] * y
    x1_ref[...] = x1
    h2_ref[...] = (_rms(x1, NORM_EPS) * g_ref[...] * (1.0 + sc_ref[...]) + sh_ref[...]).astype(BF16)


def _outproj_call(ya, yb, yc, x, wo, gt, g, sc, sh):
    bsz, s, d = x.shape
    tm = min(s, 512)
    tok = lambda n: pl.BlockSpec((None, tm, n), lambda b, i: (b, i, 0))
    per_b = pl.BlockSpec((None, 1, d), lambda b, i: (b, 0, 0))
    return pl.pallas_call(
        _outproj_kernel,
        out_shape=(jax.ShapeDtypeStruct((bsz, s, d), F32), jax.ShapeDtypeStruct((bsz, s, d), BF16)),
        grid=(bsz, s // tm),
        in_specs=[tok(RWKV_DIM), tok(CONV_DIM), tok(MLA_DIM), tok(d), _resident(wo.shape),
                  per_b, _resident((1, d)), per_b, per_b],
        out_specs=(tok(d), tok(d)),
        compiler_params=_params(("parallel", "parallel")),
        name="outproj_norm2",
    )(ya, yb, yc, x, wo, gt, g, sc, sh)


def _mlp_kernel(h_ref, x_ref, w1_ref, w2_ref, gt_ref, fg_ref, o_ref, acc_ref, *, final):
    f = pl.program_id(2)

    @pl.when(f == 0)
    def _():
        acc_ref[...] = jnp.zeros_like(acc_ref)

    a = jnp.maximum(_dot(h_ref[...], w1_ref[...]), 0.0)
    acc_ref[...] += _dot((a * a).astype(BF16), w2_ref[...])

    @pl.when(f == pl.num_programs(2) - 1)
    def _():
        xo = x_ref[...] + gt_ref[...] * acc_ref[...]
        if final:
            xo = _rms(xo, NORM_EPS) * fg_ref[...]
        o_ref[...] = xo


def _mlp_call(h2, x1, w1, w2, gt, fg, final):
    bsz, s, d = x1.shape
    tm = min(s, 512)
    tf = 512
    tok = pl.BlockSpec((None, tm, d), lambda b, i, f: (b, i, 0))
    return pl.pallas_call(
        functools.partial(_mlp_kernel, final=final),
        out_shape=jax.ShapeDtypeStruct((bsz, s, d), F32),
        grid=(bsz, s // tm, D_FF // tf),
        in_specs=[tok, tok,
                  pl.BlockSpec((d, tf), lambda b, i, f: (0, f)),
                  pl.BlockSpec((tf, d), lambda b, i, f: (f, 0)),
                  pl.BlockSpec((None, 1, d), lambda b, i, f: (b, 0, 0)),
                  pl.BlockSpec((1, d), lambda b, i, f: (0, 0))],
        out_specs=tok,
        scratch_shapes=[pltpu.VMEM((tm, d), F32)],
        compiler_params=_params(("parallel", "parallel", "arbitrary")),
        name="relu2_mlp",
    )(h2, x1, w1, w2, gt, fg)


def _pad_cols(w, n):
    return jnp.pad(w, ((0, 0), (0, n - w.shape[1])))


def _rotate_half_cols(w):
    half = w.shape[-1] // 2
    return jnp.concatenate([-w[..., half:], w[..., :half]], axis=-1)


def _layer_weights(l, w_in, w_in_vres, rwkv_mu, vres_mu, decay_up, iclr_up, gate_up, vres_up,
                   w_qb, w_kvb):
    d = D_MODEL
    wl = w_in[l]
    o_lora = 3 * RWKV_DIM
    o_conv = RWKV_IN
    o_mla = RWKV_IN + CONV_IN
    has_vres = l > 0
    lora_cols = [wl[:, o_lora:o_conv]]
    mu_cols = [rwkv_mu[l, o_lora:o_conv]]
    if has_vres:
        lora_cols.append(w_in_vres[l - 1])
        mu_cols.append(vres_mu[l - 1])
    w_lora = _pad_cols(jnp.concatenate(lora_cols, axis=1), LORA_PAD)
    mu_lora = jnp.pad(jnp.concatenate(mu_cols), (0, LORA_PAD - sum(m.shape[0] for m in mu_cols)))
    mla = wl[:, o_mla:N_IN]
    kr = mla[:, Q_LORA + KV_LORA:]
    zpad = jnp.zeros((d, LANE - QK_ROPE), F32)
    w_mla = jnp.concatenate([mla[:, :Q_LORA + KV_LORA], kr, zpad, _rotate_half_cols(kr), zpad], axis=1)

    up = jnp.zeros((LORA_PAD, 4 * RWKV_DIM), F32)
    o_a, o_g = DECAY_LORA, DECAY_LORA + ICLR_LORA
    o_v = o_g + GATE_LORA
    up = up.at[0:o_a, 0:RWKV_DIM].set(decay_up[l])
    up = up.at[o_a:o_g, RWKV_DIM:2 * RWKV_DIM].set(iclr_up[l])
    up = up.at[o_g:o_v, 2 * RWKV_DIM:3 * RWKV_DIM].set(gate_up[l])
    if has_vres:
        up = up.at[o_v:o_v + VRES_LORA, 3 * RWKV_DIM:].set(vres_up[l - 1])
    up_hi = up.astype(BF16)

    scale = (QK_NOPE + QK_ROPE) ** -0.5 * LOG2_E
    wq = (w_qb[l] * scale).reshape(Q_LORA, MLA_HEADS, QK_NOPE + QK_ROPE)
    nope, rope = wq[:, :, :QK_NOPE], wq[:, :, QK_NOPE:]
    z = jnp.zeros((Q_LORA, MLA_HEADS, LANE - QK_ROPE), F32)
    wq_main = jnp.concatenate([nope, rope, z], axis=2).reshape(Q_LORA, MLA_HEADS * QK_PAD)
    wq_rot = jnp.concatenate([_rotate_half_cols(rope), z], axis=2).reshape(Q_LORA, MLA_HEADS * LANE)
    return dict(
        w_rkv=wl[:, :o_lora].astype(BF16), w_lora=w_lora.astype(BF16),
        w_conv=wl[:, o_conv:o_mla].astype(BF16), w_mla=w_mla.astype(BF16),
        mu_rkv=rwkv_mu[l, :o_lora].reshape(1, -1), mu_lora=mu_lora.reshape(1, -1),
        w_lora_up=jnp.concatenate([up_hi, up_hi, (up - up_hi.astype(F32)).astype(BF16)], axis=0),
        wq=wq_main.astype(BF16), wq_rot=wq_rot.astype(BF16),
        wkv=w_kvb[l].astype(BF16))


def kernel(x, c, positions, ada_w, ada_b, norm1_g, norm2_g, final_g, w_in, w_in_vres, rwkv_mu,
           vres_mu, decay_w0, decay_up, iclr_a0, iclr_up, gate_up, vres_v0, vres_up, k_k, k_a, r_k,
           lnx_g, lnx_b, conv_w, conv_b, conv_ln_g, conv_ln_b, q_a_norm_g, w_qb, kv_a_norm_g, w_kvb,
           w_out, mlp_w1, mlp_w2):
    bsz, s, d = x.shape
    mod = _ada_call(c, ada_w, ada_b).reshape(DEPTH, bsz, 6, 1, d)
    cos, sin = _rope_call(positions)
    row = lambda a: a.reshape(1, -1)
    v_first = None
    for l in range(DEPTH):
        sh1, sc1, gt1, sh2, sc2, gt2 = (mod[l, :, j] for j in range(6))
        lw = _layer_weights(l, w_in, w_in_vres, rwkv_mu, vres_mu, decay_up, iclr_up, gate_up,
                            vres_up, w_qb, w_kvb)
        p_rkv, p_lora, p_conv, p_mla = _inproj_call(
            x, row(norm1_g[l]), sc1, sh1, lw["w_rkv"], lw["w_lora"], lw["w_conv"], lw["w_mla"])
        prm = dict(mu_rkv=lw["mu_rkv"], mu_lora=lw["mu_lora"], w_lora_up=lw["w_lora_up"],
                   w0=row(decay_w0[l]), a0=row(iclr_a0[l]), k_k=row(k_k[l]), k_a=row(k_a[l]),
                   r_k=row(r_k[l]), lnx_g=row(lnx_g[l]), lnx_b=row(lnx_b[l]))
        if l == 0:
            y_a, v_first = _rwkv_call(p_rkv, p_lora, None, prm)
        else:
            prm["v0"] = row(vres_v0[l - 1])
            y_a = _rwkv_call(p_rkv, p_lora, v_first, prm)
        y_b = _conv_call(p_conv, conv_w[l], conv_b[l], conv_ln_g[l], conv_ln_b[l])
        q, k, v = _mla_prep_call(p_mla, cos, sin, row(q_a_norm_g[l]), row(kv_a_norm_g[l]),
                                 lw["wq"], lw["wq_rot"], lw["wkv"])
        y_c = _attn_call(q, k, v)
        x1, h2 = _outproj_call(y_a, y_b, y_c, x, w_out[l].astype(BF16), gt1,
                               row(norm2_g[l]), sc2, sh2)
        x = _mlp_call(h2, x1, mlp_w1[l].astype(BF16), mlp_w2[l].astype(BF16), gt2,
                      row(final_g), final=(l == DEPTH - 1))
    return x
```

```python
import functools

import jax
import jax.numpy as jnp
from jax import lax
from jax.experimental import pallas as pl
from jax.experimental.pallas import tpu as pltpu

F32 = jnp.float32
BF16 = jnp.bfloat16

D_MODEL = 2048
DEPTH = 2
CHUNK = 64
NORM_EPS = 1e-6
LN_EPS = 1e-5
D_FF = 4 * D_MODEL

RWKV_HEAD = 64
RWKV_DIM = 512
RWKV_HEADS = 8
RWKV_PAIRS = RWKV_HEADS // 2
DECAY_LORA = 32
ICLR_LORA = 32
VRES_LORA = 32
GATE_LORA = 96
RWKV_GN_EPS = 64e-5
LORA_PAD = 256
RWKV_DECAY_SCALE = 0.6065306597126334
RWKV_INV_BASE = 8
RWKV_CHUNKS_PER_ITER = 2

CONV_DIM = 512
CONV_WIDTH = 31
CONV_HALO = 32

MLA_DIM = 1024
V_HEAD = 128
MLA_HEADS = 8
QK_NOPE = 128
QK_ROPE = 64
Q_LORA = 512
KV_LORA = 256
ROPE_THETA = 10000.0
QK_PAD = 256
ATTN_HEADS_PER_STEP = 2
MLA_IN_PAD = 1024

RWKV_IN = 3 * RWKV_DIM + DECAY_LORA + ICLR_LORA + GATE_LORA
CONV_IN = 2 * CONV_DIM
N_IN = RWKV_IN + CONV_IN + Q_LORA + KV_LORA + QK_ROPE

LANE = 128
V7X_VMEM_LIMIT = 56 * 1024 * 1024
MASK_VALUE = -1e30
LOG2_E = 1.4426950408889634


def _params(semantics, vmem=V7X_VMEM_LIMIT):
    return pltpu.CompilerParams(dimension_semantics=semantics, vmem_limit_bytes=vmem)


def _resident(shape):
    nd = len(shape)
    return pl.BlockSpec(shape, lambda *_: (0,) * nd, pipeline_mode=pl.Buffered(1))


def _dot(a, b):
    return jnp.dot(a, b, preferred_element_type=F32)


def _dot_nt(a, b):
    return lax.dot_general(a, b, (((1,), (1,)), ((), ())), preferred_element_type=F32)


def _split(x):
    hi = x.astype(BF16)
    lo = (x - hi.astype(F32)).astype(BF16)
    return hi, lo


def _dot3(a, b):
    ah, al = _split(a)
    bh, bl = _split(b)
    return _dot(jnp.concatenate([ah, al, ah], axis=1), jnp.concatenate([bh, bh, bl], axis=0))


def _dot3_nt(a, b):
    ah, al = _split(a)
    bh, bl = _split(b)
    return _dot_nt(jnp.concatenate([ah, al, ah], axis=1), jnp.concatenate([bh, bh, bl], axis=1))


def _dot2_exact_rhs(a, b_bf16):
    ah, al = _split(a)
    return _dot(jnp.concatenate([ah, al], axis=1), jnp.concatenate([b_bf16] * 2, axis=0))


def _rms(x, eps):
    return x * lax.rsqrt(jnp.mean(x * x, axis=-1, keepdims=True) + eps)


def _sigmoid(x):
    return 0.5 * jnp.tanh(0.5 * x) + 0.5


def _iota(shape, dim):
    return lax.broadcasted_iota(jnp.int32, shape, dim)


def _ada_kernel(c_ref, w_ref, b_ref, o_ref):
    c = c_ref[...]
    o_ref[...] = _dot3(c * _sigmoid(c), w_ref[...]) + b_ref[...]


def _ada_call(c, ada_w, ada_b):
    depth, d, n = ada_w.shape
    bsz = c.shape[0]
    tn = 512
    return pl.pallas_call(
        _ada_kernel,
        out_shape=jax.ShapeDtypeStruct((depth, bsz, n), F32),
        grid=(depth, n // tn),
        in_specs=[
            pl.BlockSpec((bsz, d), lambda l, j: (0, 0)),
            pl.BlockSpec((None, d, tn), lambda l, j: (l, 0, j)),
            pl.BlockSpec((None, 1, tn), lambda l, j: (l, 0, j)),
        ],
        out_specs=pl.BlockSpec((None, bsz, tn), lambda l, j: (l, 0, j)),
        compiler_params=_params(("parallel", "parallel")),
        name="ada_mod",
    )(c, ada_w, ada_b.reshape(depth, 1, n))


def _rope_kernel(pos_ref, invf_ref, cos_ref, sin_ref):
    ang = pos_ref[...] * invf_ref[...]
    cos_ref[...] = jnp.cos(ang)
    sin_ref[...] = jnp.sin(ang)


def _rope_call(positions):
    bsz, s = positions.shape
    ts = min(s, 1024)
    inv_freq = ROPE_THETA ** (-jnp.arange(0, QK_ROPE, 2, dtype=F32) / QK_ROPE)
    invf = jnp.tile(inv_freq, LANE // (QK_ROPE // 2)).reshape(1, LANE)
    pos = positions.astype(F32).reshape(bsz, s, 1)
    shp = jax.ShapeDtypeStruct((bsz, s, LANE), F32)
    return pl.pallas_call(
        _rope_kernel,
        out_shape=(shp, shp),
        grid=(bsz, s // ts),
        in_specs=[
            pl.BlockSpec((None, ts, 1), lambda b, i: (b, i, 0)),
            pl.BlockSpec((1, LANE), lambda b, i: (0, 0)),
        ],
        out_specs=(
            pl.BlockSpec((None, ts, LANE), lambda b, i: (b, i, 0)),
            pl.BlockSpec((None, ts, LANE), lambda b, i: (b, i, 0)),
        ),
        compiler_params=_params(("parallel", "parallel")),
        name="rope_tables",
    )(pos, invf)


def _inproj_kernel(x_ref, g_ref, sc_ref, sh_ref, w_rkv, w_lora, w_conv, w_mla,
                   o_rkv, o_lora, o_conv, o_mla):
    x = x_ref[...]
    h = _rms(x, NORM_EPS) * g_ref[...] * (1.0 + sc_ref[...]) + sh_ref[...]
    hb = h.astype(BF16)
    o_rkv[...] = _dot(hb, w_rkv[...])
    o_lora[...] = _dot(hb, w_lora[...])
    o_conv[...] = _dot(hb, w_conv[...])
    o_mla[...] = _dot(hb, w_mla[...])


def _inproj_call(x, g, sc, sh, w_rkv, w_lora, w_conv, w_mla):
    bsz, s, d = x.shape
    tm = min(s, 256)
    widths = (w_rkv.shape[1], w_lora.shape[1], w_conv.shape[1], w_mla.shape[1])
    tok = lambda n: pl.BlockSpec((None, tm, n), lambda b, i: (b, i, 0))
    per_b = pl.BlockSpec((None, 1, d), lambda b, i: (b, 0, 0))
    return pl.pallas_call(
        _inproj_kernel,
        out_shape=tuple(jax.ShapeDtypeStruct((bsz, s, n), F32) for n in widths),
        grid=(bsz, s // tm),
        in_specs=[tok(d), _resident((1, d)), per_b, per_b,
                  _resident(w_rkv.shape), _resident(w_lora.shape),
                  _resident(w_conv.shape), _resident(w_mla.shape)],
        out_specs=tuple(tok(n) for n in widths),
        compiler_params=_params(("parallel", "parallel")),
        name="norm_inproj",
    )(x, g, sc, sh, w_rkv, w_lora, w_conv, w_mla)


def _block_diag(y):
    lane = _iota(y.shape, 1)
    zero = jnp.zeros_like(y)
    return jnp.concatenate([jnp.where(lane < RWKV_HEAD, y, zero),
                            jnp.where(lane >= RWKV_HEAD, y, zero)], axis=0)


def _packed_mm(x, ys):
    xh, xl = _split(x)
    lhs = jnp.concatenate([xh, xl, xh], axis=1)
    cols = []
    for y in ys:
        yh, yl = _split(y)
        cols.append(jnp.concatenate([_block_diag(yh), _block_diag(yh), _block_diag(yl)], axis=0))
    out = _dot(lhs, cols[0] if len(cols) == 1 else jnp.concatenate(cols, axis=1))
    return [out[:, i * LANE:(i + 1) * LANE] for i in range(len(ys))]


def _head_sum_matrix(scale):
    r = _iota((LANE, LANE), 0) // RWKV_HEAD
    c = _iota((LANE, LANE), 1) // RWKV_HEAD
    return jnp.where(r == c, scale, 0.0).astype(BF16)


def _rwkv_kernel(*refs, has_vres, tt):
    if has_vres:
        (prkv_ref, plora_ref, vfirst_ref, mu_rkv, mu_lora, wc_ref, w0_ref, a0_ref, v0_ref,
         kk_ref, ka_ref, rk_ref, lng_ref, lnb_ref, y_ref,
         prev_rkv, prev_lora, h_scr, at_s, rt_s, bt_s, kt_s, bp_s, kp_s, v_s, pf_s, o_s) = refs
    else:
        (prkv_ref, plora_ref, mu_rkv, mu_lora, wc_ref, w0_ref, a0_ref,
         kk_ref, ka_ref, rk_ref, lng_ref, lnb_ref, y_ref, vout_ref,
         prev_rkv, prev_lora, h_scr, at_s, rt_s, bt_s, kt_s, bp_s, kp_s, v_s, pf_s, o_s) = refs
    n_chunks = tt // CHUNK

    @pl.when(pl.program_id(1) == 0)
    def _():
        prev_rkv[...] = jnp.zeros_like(prev_rkv)
        prev_lora[...] = jnp.zeros_like(prev_lora)
        h_scr[...] = jnp.zeros_like(h_scr)

    def shift_mix(p_ref, prev_ref, mu_ref):
        p = p_ref[...]
        row = _iota(p.shape, 0)
        prev = jnp.where(row == 0, prev_ref[0:1, :], pltpu.roll(p, 1, 0))
        prev_ref[0:1, :] = p[tt - 1:tt, :]
        return p + (prev - p) * mu_ref[...]

    xs = shift_mix(prkv_ref, prev_rkv, mu_rkv)
    xl = shift_mix(plora_ref, prev_lora, mu_lora)
    r = xs[:, 0:RWKV_DIM]
    k = xs[:, RWKV_DIM:2 * RWKV_DIM]
    v = xs[:, 2 * RWKV_DIM:3 * RWKV_DIM]

    lane = _iota(xl.shape, 1)
    o_w, o_a, o_g = DECAY_LORA, DECAY_LORA + ICLR_LORA, DECAY_LORA + ICLR_LORA + GATE_LORA
    act = jnp.where(lane < o_w, jnp.tanh(xl),
                    jnp.where((lane >= o_a) & (lane < o_g), _sigmoid(xl), xl))
    act_h, act_l = _split(act)
    lora = _dot(jnp.concatenate([act_h, act_l, act_h], axis=1), wc_ref[...])
    z = w0_ref[...] + lora[:, 0:RWKV_DIM]
    lw = -RWKV_DECAY_SCALE * _sigmoid(z)
    a_ic = _sigmoid(a0_ref[...] + lora[:, RWKV_DIM:2 * RWKV_DIM])
    gate = lora[:, 2 * RWKV_DIM:3 * RWKV_DIM]
    if has_vres:
        v = v + (vfirst_ref[...] - v) * _sigmoid(v0_ref[...] + lora[:, 3 * RWKV_DIM:4 * RWKV_DIM])
    else:
        vout_ref[...] = v

    ones_bd = _head_sum_matrix(1.0)
    mean_bd = _head_sum_matrix(1.0 / RWKV_HEAD)

    def per_head(x, mat, split=False):
        mm = _dot2_exact_rhs if split else (lambda a, b: _dot(a.astype(BF16), b))
        return jnp.concatenate(
            [mm(x[:, p * LANE:(p + 1) * LANE], mat) for p in range(RWKV_PAIRS)], axis=1)

    kk = k * kk_ref[...]
    kk = kk * lax.rsqrt(jnp.maximum(per_head(kk * kk, ones_bd), 1e-24))
    k2 = k * (1.0 + (a_ic - 1.0) * ka_ref[...])
    a_vec = -kk
    b_vec = kk * a_ic
    bonus = per_head(r * k2 * rk_ref[...], ones_bd) * v

    tr = _iota((tt, tt), 0)
    tc = _iota((tt, tt), 1)
    tri = jnp.where((tr // CHUNK == tc // CHUNK) & (tc <= tr), 1.0, 0.0).astype(BF16)
    cum = _dot2_exact_rhs_left(tri, lw)
    cum3 = cum.reshape(n_chunks, CHUNK, RWKV_DIM)
    cum_end = jnp.broadcast_to(cum3[:, CHUNK - 1:CHUNK, :], cum3.shape).reshape(tt, RWKV_DIM)
    pf = jnp.exp(cum)
    pinv = jnp.exp(-cum)
    pprev = jnp.exp(cum - lw)
    pend = jnp.exp(cum_end - cum)

    def put(dst, val):
        for p in range(RWKV_PAIRS):
            dst[p] = val[:, p * LANE:(p + 1) * LANE]

    put(at_s, a_vec * pprev)
    put(rt_s, r * pf)
    put(bt_s, b_vec * pinv)
    put(kt_s, k2 * pinv)
    put(bp_s, b_vec * pend)
    put(kp_s, k2 * pend)
    put(v_s, v)
    put(pf_s, pf)

    t_loc = _iota((CHUNK, LANE), 0)
    s_loc = _iota((CHUNK, LANE), 1) % RWKV_HEAD
    strict = s_loc < t_loc
    incl = s_loc <= t_loc
    eye_p = jnp.where(s_loc == t_loc, 1.0, 0.0)
    same_block = lambda n: (t_loc // n) == (s_loc // n)
    in_base = same_block(RWKV_INV_BASE)
    merge_masks = []
    n = RWKV_INV_BASE
    while n < CHUNK:
        merge_masks.append(same_block(2 * n) & jnp.logical_not(same_block(n)))
        n *= 2
    rr = _iota((LANE, LANE), 0)
    cc = _iota((LANE, LANE), 1)
    same_head = (rr // RWKV_HEAD) == (cc // RWKV_HEAD)
    diag = rr == cc

    group = min(n_chunks, RWKV_CHUNKS_PER_ITER)

    def chunk_group(cg, carry):
        streams = []
        for ci in range(group):
            c = cg * group + ci
            rows = pl.ds(pl.multiple_of(c * CHUNK, CHUNK), CHUNK)
            last = pl.ds(c * CHUNK + CHUNK - 1, 1)
            streams += [(p, rows, last) for p in range(RWKV_PAIRS)]
        ns = range(len(streams))
        load = lambda ref: [ref[p, rows, :] for (p, rows, _) in streams]
        at, rt, bt, kt, bp, kp, vv = (load(r) for r in (at_s, rt_s, bt_s, kt_s, bp_s, kp_s, v_s))
        g = [_dot3_nt(jnp.concatenate([at[i], rt[i]], axis=0),
                      jnp.concatenate([_block_diag(bt[i]), _block_diag(kt[i])], axis=0)) for i in ns]
        aab = [jnp.where(strict, g[i][0:CHUNK, 0:LANE], 0.0) for i in ns]
        aak = [jnp.where(strict, g[i][0:CHUNK, LANE:2 * LANE], 0.0) for i in ns]
        arb = [jnp.where(incl, g[i][CHUNK:2 * CHUNK, 0:LANE], 0.0) for i in ns]
        ark = [jnp.where(incl, g[i][CHUNK:2 * CHUNK, LANE:2 * LANE], 0.0) for i in ns]
        n_diag = [jnp.where(in_base, aab[i], 0.0) for i in ns]
        dpow = [_packed_mm(n_diag[i], [n_diag[i]])[0] for i in ns]
        tinv = [eye_p + n_diag[i] for i in ns]
        akv = [_packed_mm(aak[i], [vv[i]])[0] for i in ns]
        arkv = [_packed_mm(ark[i], [vv[i]])[0] for i in ns]
        res = [_packed_mm(dpow[i], [dpow[i], tinv[i]]) for i in ns]
        tinv = [tinv[i] + res[i][1] for i in ns]
        tinv = [tinv[i] + _packed_mm(res[i][0], [tinv[i]])[0] for i in ns]
        for level_mask in merge_masks:
            ct = [_packed_mm(jnp.where(level_mask, aab[i], 0.0), [tinv[i]])[0] for i in ns]
            tinv = [tinv[i] + _packed_mm(tinv[i], [ct[i]])[0] for i in ns]
        wu = [_packed_mm(tinv[i], [at[i], akv[i]]) for i in ns]
        ab = [_packed_mm(arb[i], wu[i]) for i in ns]
        q_m = [rt[i] + ab[i][0] for i in ns]
        y0 = [ab[i][1] + arkv[i] for i in ns]
        upd = [_dot3(jnp.concatenate([bp[i], kp[i]], axis=0).T,
                     jnp.concatenate([jnp.concatenate(wu[i], axis=1),
                                      jnp.concatenate([jnp.zeros_like(vv[i]), vv[i]], axis=1)], axis=0))
               for i in ns]
        h = [h_scr[p] for p in range(RWKV_PAIRS)]
        for i in ns:
            p, rows, last = streams[i]
            m_bd = jnp.where(same_head, upd[i][:, 0:LANE], 0.0) + jnp.where(diag, pf_s[p, last, :], 0.0)
            qh_mh = _dot3(jnp.concatenate([q_m[i], m_bd], axis=0), h[p])
            o_s[rows, p * LANE:(p + 1) * LANE] = qh_mh[0:CHUNK] + y0[i]
            h[p] = qh_mh[CHUNK:CHUNK + LANE] + jnp.where(same_head, upd[i][:, LANE:2 * LANE], 0.0)
        for p in range(RWKV_PAIRS):
            h_scr[p] = h[p]
        return carry

    lax.fori_loop(0, n_chunks // group, chunk_group, 0)

    o = o_s[...]
    mean = per_head(o, mean_bd, split=True)
    oc = o - mean
    var = per_head(oc * oc, mean_bd)
    on = oc * lax.rsqrt(var + RWKV_GN_EPS) * lng_ref[...] + lnb_ref[...]
    y_ref[...] = ((on + bonus) * gate).astype(y_ref.dtype)


def _dot2_exact_rhs_left(m_bf16, x):
    xh, xl = _split(x)
    return _dot(jnp.concatenate([m_bf16] * 2, axis=1), jnp.concatenate([xh, xl], axis=0))


def _rwkv_call(p_rkv, p_lora, v_first, prm):
    bsz, s, _ = p_rkv.shape
    tt = min(s, 256)
    has_vres = v_first is not None
    tok = lambda n: pl.BlockSpec((None, tt, n), lambda b, i: (b, i, 0))
    row = lambda n: _resident((1, n))
    in_specs = [tok(3 * RWKV_DIM), tok(LORA_PAD)]
    args = [p_rkv, p_lora]
    if has_vres:
        in_specs.append(tok(RWKV_DIM))
        args.append(v_first)
    in_specs += [row(3 * RWKV_DIM), row(LORA_PAD), _resident((3 * LORA_PAD, 4 * RWKV_DIM)),
                 row(RWKV_DIM), row(RWKV_DIM)]
    args += [prm["mu_rkv"], prm["mu_lora"], prm["w_lora_up"], prm["w0"], prm["a0"]]
    if has_vres:
        in_specs.append(row(RWKV_DIM))
        args.append(prm["v0"])
    in_specs += [row(RWKV_DIM)] * 5
    args += [prm["k_k"], prm["k_a"], prm["r_k"], prm["lnx_g"], prm["lnx_b"]]
    y_shape = jax.ShapeDtypeStruct((bsz, s, RWKV_DIM), BF16)
    if has_vres:
        out_shape, out_specs = y_shape, tok(RWKV_DIM)
    else:
        out_shape = (y_shape, jax.ShapeDtypeStruct((bsz, s, RWKV_DIM), F32))
        out_specs = (tok(RWKV_DIM), tok(RWKV_DIM))
    pair_tile = pltpu.VMEM((RWKV_PAIRS, tt, LANE), F32)
    out = pl.pallas_call(
        functools.partial(_rwkv_kernel, has_vres=has_vres, tt=tt),
        out_shape=out_shape,
        grid=(bsz, s // tt),
        in_specs=in_specs,
        out_specs=out_specs,
        scratch_shapes=[
            pltpu.VMEM((8, 3 * RWKV_DIM), F32),
            pltpu.VMEM((8, LORA_PAD), F32),
            pltpu.VMEM((RWKV_PAIRS, LANE, LANE), F32),
        ] + [pair_tile] * 8 + [pltpu.VMEM((tt, RWKV_DIM), F32)],
        compiler_params=_params(("parallel", "arbitrary")),
        name="rwkv7_mixer",
    )(*args)
    return out if has_vres else out


def _conv_kernel(p_ref, w_ref, b_ref, g_ref, be_ref, o_ref, ubuf, *, tt):
    @pl.when(pl.program_id(1) == 0)
    def _():
        ubuf[0:CONV_HALO, :] = jnp.zeros((CONV_HALO, CONV_DIM), F32)

    p = p_ref[...]
    ubuf[CONV_HALO:CONV_HALO + tt, :] = p[:, 0:CONV_DIM] * _sigmoid(p[:, CONV_DIM:2 * CONV_DIM])
    sub = 8
    base = CONV_HALO - sub
    ext = tt + sub
    acc = None
    for b in range(sub):
        part = None
        for a in range((CONV_WIDTH - 1 - b) // sub + 1):
            j = CONV_WIDTH - 1 - (sub * a + b)
            term = ubuf[base - sub * a:base - sub * a + ext, :] * w_ref[j:j + 1, :]
            part = term if part is None else part + term
        if b:
            part = pltpu.roll(part, b, 0)
        acc = part if acc is None else acc + part
    acc = acc[sub:sub + tt, :] + b_ref[...]
    ubuf[0:CONV_HALO, :] = ubuf[tt:tt + CONV_HALO, :]
    mean = jnp.mean(acc, axis=-1, keepdims=True)
    xc = acc - mean
    var = jnp.mean(xc * xc, axis=-1, keepdims=True)
    u = xc * lax.rsqrt(var + LN_EPS) * g_ref[...] + be_ref[...]
    o_ref[...] = (u * _sigmoid(u)).astype(o_ref.dtype)


def _conv_call(p_conv, conv_w, conv_b, ln_g, ln_b):
    bsz, s, _ = p_conv.shape
    tt = min(s, 512)
    wpad = jnp.zeros((CONV_HALO, CONV_DIM), F32).at[:CONV_WIDTH].set(conv_w)
    row = _resident((1, CONV_DIM))
    return pl.pallas_call(
        functools.partial(_conv_kernel, tt=tt),
        out_shape=jax.ShapeDtypeStruct((bsz, s, CONV_DIM), BF16),
        grid=(bsz, s // tt),
        in_specs=[pl.BlockSpec((None, tt, 2 * CONV_DIM), lambda b, i: (b, i, 0)),
                  _resident((CONV_HALO, CONV_DIM)), row, row, row],
        out_specs=pl.BlockSpec((None, tt, CONV_DIM), lambda b, i: (b, i, 0)),
        scratch_shapes=[pltpu.VMEM((tt + CONV_HALO, CONV_DIM), F32)],
        compiler_params=_params(("parallel", "arbitrary")),
        name="conformer_conv",
    )(p_conv, wpad, conv_b.reshape(1, -1), ln_g.reshape(1, -1), ln_b.reshape(1, -1))


def _mla_prep_kernel(p_ref, cos_ref, sin_ref, qg_ref, kvg_ref, wq_ref, wqs_ref, wkv_ref,
                     q_ref, k_ref, v_ref):
    p = p_ref[...]
    cos = cos_ref[...]
    sin = sin_ref[...]
    qc = (_rms(p[:, 0:Q_LORA], NORM_EPS) * qg_ref[...]).astype(BF16)
    kvc = (_rms(p[:, Q_LORA:Q_LORA + KV_LORA], NORM_EPS) * kvg_ref[...]).astype(BF16)
    q = _dot(qc, wq_ref[...])
    qs = _dot(qc, wqs_ref[...])
    kv = _dot(kvc, wkv_ref[...])
    o_kr = Q_LORA + KV_LORA
    k_rope = (p[:, o_kr:o_kr + LANE] * cos + p[:, o_kr + LANE:o_kr + 2 * LANE] * sin).astype(BF16)
    for h in range(MLA_HEADS):
        b0 = h * QK_PAD
        q_ref[:, b0:b0 + LANE] = q[:, b0:b0 + LANE].astype(BF16)
        q_ref[:, b0 + LANE:b0 + 2 * LANE] = (
            q[:, b0 + LANE:b0 + 2 * LANE] * cos + qs[:, h * LANE:(h + 1) * LANE] * sin).astype(BF16)
        k_ref[:, b0:b0 + LANE] = kv[:, b0:b0 + LANE].astype(BF16)
        k_ref[:, b0 + LANE:b0 + 2 * LANE] = k_rope
        v_ref[:, h * V_HEAD:(h + 1) * V_HEAD] = kv[:, b0 + LANE:b0 + 2 * LANE].astype(BF16)


def _mla_prep_call(p_mla, cos, sin, qg, kvg, wq, wqs, wkv):
    bsz, s, _ = p_mla.shape
    tm = min(s, 256)
    tok = lambda n: pl.BlockSpec((None, tm, n), lambda b, i: (b, i, 0))
    return pl.pallas_call(
        _mla_prep_kernel,
        out_shape=(jax.ShapeDtypeStruct((bsz, s, MLA_HEADS * QK_PAD), BF16),
                   jax.ShapeDtypeStruct((bsz, s, MLA_HEADS * QK_PAD), BF16),
                   jax.ShapeDtypeStruct((bsz, s, MLA_DIM), BF16)),
        grid=(bsz, s // tm),
        in_specs=[tok(MLA_IN_PAD), tok(LANE), tok(LANE),
                  _resident((1, Q_LORA)), _resident((1, KV_LORA)),
                  _resident(wq.shape), _resident(wqs.shape), _resident(wkv.shape)],
        out_specs=(tok(MLA_HEADS * QK_PAD), tok(MLA_HEADS * QK_PAD), tok(MLA_DIM)),
        compiler_params=_params(("parallel", "parallel")),
        name="mla_prep",
    )(p_mla, cos, sin, qg, kvg, wq, wqs, wkv)


def _attn_kernel(q_ref, k_ref, v_ref, o_ref, *, tq, nh):
    i = pl.program_id(2)
    heads = range(nh)
    q = [q_ref[:, h * QK_PAD:(h + 1) * QK_PAD] for h in heads]

    def step(j, carry, masked):
        rows = pl.ds(pl.multiple_of(j * tq, tq), tq)
        sc = [_dot_nt(q[h], k_ref[rows, h * QK_PAD:(h + 1) * QK_PAD]) for h in heads]
        if masked:
            visible = (_iota((tq, tq), 1) // CHUNK) <= (_iota((tq, tq), 0) // CHUNK)
            sc = [jnp.where(visible, s, MASK_VALUE) for s in sc]
        m_new = [jnp.maximum(carry[h][0], jnp.max(sc[h], axis=-1, keepdims=True)) for h in heads]
        alpha = [jnp.exp2(carry[h][0] - m_new[h]) for h in heads]
        pr = [jnp.exp2(sc[h] - m_new[h]) for h in heads]
        l_new = [alpha[h] * carry[h][1] + jnp.sum(pr[h], axis=-1, keepdims=True) for h in heads]
        pv = [_dot(pr[h].astype(BF16), v_ref[rows, h * V_HEAD:(h + 1) * V_HEAD]) for h in heads]
        return tuple((m_new[h], l_new[h], alpha[h] * carry[h][2] + pv[h]) for h in heads)

    init = (jnp.full((tq, 1), MASK_VALUE, F32), jnp.zeros((tq, 1), F32), jnp.zeros((tq, V_HEAD), F32))
    carry = lax.fori_loop(0, i, lambda j, c: step(j, c, False), (init,) * nh)
    carry = step(i, carry, True)
    for h in heads:
        o_ref[:, h * V_HEAD:(h + 1) * V_HEAD] = (carry[h][2] / carry[h][1]).astype(o_ref.dtype)


def _attn_call(q, k, v):
    bsz, s, _ = v.shape
    tq = min(s, 512)
    nh = ATTN_HEADS_PER_STEP
    return pl.pallas_call(
        functools.partial(_attn_kernel, tq=tq, nh=nh),
        out_shape=jax.ShapeDtypeStruct((bsz, s, MLA_DIM), BF16),
        grid=(bsz, MLA_HEADS // nh, s // tq),
        in_specs=[pl.BlockSpec((None, tq, nh * QK_PAD), lambda b, h, i: (b, i, h)),
                  pl.BlockSpec((None, s, nh * QK_PAD), lambda b, h, i: (b, 0, h)),
                  pl.BlockSpec((None, s, nh * V_HEAD), lambda b, h, i: (b, 0, h))],
        out_specs=pl.BlockSpec((None, tq, nh * V_HEAD), lambda b, h, i: (b, i, h)),
        compiler_params=_params(("parallel", "parallel", "arbitrary")),
        name="mla_attention",
    )(q, k, v)


def _outproj_kernel(ya_ref, yb_ref, yc_ref, x_ref, wo_ref, gt_ref, g_ref, sc_ref, sh_ref,
                    x1_ref, h2_ref):
    o1, o2 = RWKV_DIM, RWKV_DIM + CONV_DIM
    y = (_dot(ya_ref[...], wo_ref[0:o1, :]) + _dot(yb_ref[...], wo_ref[o1:o2, :])
         + _dot(yc_ref[...], wo_ref[o2:o2 + MLA_DIM, :]))
    x1 = x_ref[...] + gt_ref[...] * y
    x1_ref[...] = x1
    h2_ref[...] = (_rms(x1, NORM_EPS) * g_ref[...] * (1.0 + sc_ref[...]) + sh_ref[...]).astype(BF16)


def _outproj_call(ya, yb, yc, x, wo, gt, g, sc, sh):
    bsz, s, d = x.shape
    tm = min(s, 512)
    tok = lambda n: pl.BlockSpec((None, tm, n), lambda b, i: (b, i, 0))
    per_b = pl.BlockSpec((None, 1, d), lambda b, i: (b, 0, 0))
    return pl.pallas_call(
        _outproj_kernel,
        out_shape=(jax.ShapeDtypeStruct((bsz, s, d), F32), jax.ShapeDtypeStruct((bsz, s, d), BF16)),
        grid=(bsz, s // tm),
        in_specs=[tok(RWKV_DIM), tok(CONV_DIM), tok(MLA_DIM), tok(d), _resident(wo.shape),
                  per_b, _resident((1, d)), per_b, per_b],
        out_specs=(tok(d), tok(d)),
        compiler_params=_params(("parallel", "parallel")),
        name="outproj_norm2",
    )(ya, yb, yc, x, wo, gt, g, sc, sh)


def _mlp_kernel(h_ref, x_ref, w1_ref, w2_ref, gt_ref, fg_ref, o_ref, acc_ref, *, final):
    f = pl.program_id(2)

    @pl.when(f == 0)
    def _():
        acc_ref[...] = jnp.zeros_like(acc_ref)

    a = jnp.maximum(_dot(h_ref[...], w1_ref[...]), 0.0)
    acc_ref[...] += _dot((a * a).astype(BF16), w2_ref[...])

    @pl.when(f == pl.num_programs(2) - 1)
    def _():
        xo = x_ref[...] + gt_ref[...] * acc_ref[...]
        if final:
            xo = _rms(xo, NORM_EPS) * fg_ref[...]
        o_ref[...] = xo


def _mlp_call(h2, x1, w1, w2, gt, fg, final):
    bsz, s, d = x1.shape
    tm = min(s, 512)
    tf = 1024
    tok = pl.BlockSpec((None, tm, d), lambda b, i, f: (b, i, 0))
    return pl.pallas_call(
        functools.partial(_mlp_kernel, final=final),
        out_shape=jax.ShapeDtypeStruct((bsz, s, d), F32),
        grid=(bsz, s // tm, D_FF // tf),
        in_specs=[tok, tok,
                  pl.BlockSpec((d, tf), lambda b, i, f: (0, f)),
                  pl.BlockSpec((tf, d), lambda b, i, f: (f, 0)),
                  pl.BlockSpec((None, 1, d), lambda b, i, f: (b, 0, 0)),
                  pl.BlockSpec((1, d), lambda b, i, f: (0, 0))],
        out_specs=tok,
        scratch_shapes=[pltpu.VMEM((tm, d), F32)],
        compiler_params=_params(("parallel", "parallel", "arbitrary")),
        name="relu2_mlp",
    )(h2, x1, w1, w2, gt, fg)


def _pad_cols(w, n):
    return jnp.pad(w, ((0, 0), (0, n - w.shape[1])))


def _rotate_half_cols(w):
    half = w.shape[-1] // 2
    return jnp.concatenate([-w[..., half:], w[..., :half]], axis=-1)


def _layer_weights(l, w_in, w_in_vres, rwkv_mu, vres_mu, decay_up, iclr_up, gate_up, vres_up,
                   w_qb, w_kvb):
    d = D_MODEL
    wl = w_in[l]
    o_lora = 3 * RWKV_DIM
    o_conv = RWKV_IN
    o_mla = RWKV_IN + CONV_IN
    has_vres = l > 0
    lora_cols = [wl[:, o_lora:o_conv]]
    mu_cols = [rwkv_mu[l, o_lora:o_conv]]
    if has_vres:
        lora_cols.append(w_in_vres[l - 1])
        mu_cols.append(vres_mu[l - 1])
    w_lora = _pad_cols(jnp.concatenate(lora_cols, axis=1), LORA_PAD)
    mu_lora = jnp.pad(jnp.concatenate(mu_cols), (0, LORA_PAD - sum(m.shape[0] for m in mu_cols)))
    mla = wl[:, o_mla:N_IN]
    kr = mla[:, Q_LORA + KV_LORA:]
    zpad = jnp.zeros((d, LANE - QK_ROPE), F32)
    w_mla = jnp.concatenate([mla[:, :Q_LORA + KV_LORA], kr, zpad, _rotate_half_cols(kr), zpad], axis=1)

    up = jnp.zeros((LORA_PAD, 4 * RWKV_DIM), F32)
    o_a, o_g = DECAY_LORA, DECAY_LORA + ICLR_LORA
    o_v = o_g + GATE_LORA
    up = up.at[0:o_a, 0:RWKV_DIM].set(decay_up[l])
    up = up.at[o_a:o_g, RWKV_DIM:2 * RWKV_DIM].set(iclr_up[l])
    up = up.at[o_g:o_v, 2 * RWKV_DIM:3 * RWKV_DIM].set(gate_up[l])
    if has_vres:
        up = up.at[o_v:o_v + VRES_LORA, 3 * RWKV_DIM:].set(vres_up[l - 1])
    up_hi = up.astype(BF16)

    scale = (QK_NOPE + QK_ROPE) ** -0.5 * LOG2_E
    wq = (w_qb[l] * scale).reshape(Q_LORA, MLA_HEADS, QK_NOPE + QK_ROPE)
    nope, rope = wq[:, :, :QK_NOPE], wq[:, :, QK_NOPE:]
    z = jnp.zeros((Q_LORA, MLA_HEADS, LANE - QK_ROPE), F32)
    wq_main = jnp.concatenate([nope, rope, z], axis=2).reshape(Q_LORA, MLA_HEADS * QK_PAD)
    wq_rot = jnp.concatenate([_rotate_half_cols(rope), z], axis=2).reshape(Q_LORA, MLA_HEADS * LANE)
    return dict(
        w_rkv=wl[:, :o_lora].astype(BF16), w_lora=w_lora.astype(BF16),
        w_conv=wl[:, o_conv:o_mla].astype(BF16), w_mla=w_mla.astype(BF16),
        mu_rkv=rwkv_mu[l, :o_lora].reshape(1, -1), mu_lora=mu_lora.reshape(1, -1),
        w_lora_up=jnp.concatenate([up_hi, up_hi, (up - up_hi.astype(F32)).astype(BF16)], axis=0),
        wq=wq_main.astype(BF16), wq_rot=wq_rot.astype(BF16),
        wkv=w_kvb[l].astype(BF16))


def kernel(x, c, positions, ada_w, ada_b, norm1_g, norm2_g, final_g, w_in, w_in_vres, rwkv_mu,
           vres_mu, decay_w0, decay_up, iclr_a0, iclr_up, gate_up, vres_v0, vres_up, k_k, k_a, r_k,
           lnx_g, lnx_b, conv_w, conv_b, conv_ln_g, conv_ln_b, q_a_norm_g, w_qb, kv_a_norm_g, w_kvb,
           w_out, mlp_w1, mlp_w2):
    bsz, s, d = x.shape
    mod = _ada_call(c, ada_w, ada_b).reshape(DEPTH, bsz, 6, 1, d)
    cos, sin = _rope_call(positions)
    row = lambda a: a.reshape(1, -1)
    v_first = None
    for l in range(DEPTH):
        sh1, sc1, gt1, sh2, sc2, gt2 = (mod[l, :, j] for j in range(6))
        lw = _layer_weights(l, w_in, w_in_vres, rwkv_mu, vres_mu, decay_up, iclr_up, gate_up,
                            vres_up, w_qb, w_kvb)
        p_rkv, p_lora, p_conv, p_mla = _inproj_call(
            x, row(norm1_g[l]), sc1, sh1, lw["w_rkv"], lw["w_lora"], lw["w_conv"], lw["w_mla"])
        prm = dict(mu_rkv=lw["mu_rkv"], mu_lora=lw["mu_lora"], w_lora_up=lw["w_lora_up"],
                   w0=row(decay_w0[l]), a0=row(iclr_a0[l]), k_k=row(k_k[l]), k_a=row(k_a[l]),
                   r_k=row(r_k[l]), lnx_g=row(lnx_g[l]), lnx_b=row(lnx_b[l]))
        if l == 0:
            y_a, v_first = _rwkv_call(p_rkv, p_lora, None, prm)
        else:
            prm["v0"] = row(vres_v0[l - 1])
            y_a = _rwkv_call(p_rkv, p_lora, v_first, prm)
        y_b = _conv_call(p_conv, conv_w[l], conv_b[l], conv_ln_g[l], conv_ln_b[l])
        q, k, v = _mla_prep_call(p_mla, cos, sin, row(q_a_norm_g[l]), row(kv_a_norm_g[l]),
                                 lw["wq"], lw["wq_rot"], lw["wkv"])
        y_c = _attn_call(q, k, v)
        x1, h2 = _outproj_call(y_a, y_b, y_c, x, w_out[l].astype(BF16), gt1,
                               row(norm2_g[l]), sc2, sh2)
        x = _mlp_call(h2, x1, mlp_w1[l].astype(BF16), mlp_w2[l].astype(BF16), gt2,
                      row(final_g), final=(l == DEPTH - 1))
    return x
```

```python
import functools

import jax
import jax.numpy as jnp
from jax import lax
from jax.experimental import pallas as pl
from jax.experimental.pallas import tpu as pltpu

F32 = jnp.float32
BF16 = jnp.bfloat16

D_MODEL = 2048
DEPTH = 2
CHUNK = 64
NORM_EPS = 1e-6
LN_EPS = 1e-5
D_FF = 4 * D_MODEL

RWKV_HEAD = 64
RWKV_DIM = 512
RWKV_HEADS = 8
RWKV_PAIRS = RWKV_HEADS // 2
DECAY_LORA = 32
ICLR_LORA = 32
VRES_LORA = 32
GATE_LORA = 96
RWKV_GN_EPS = 64e-5
LORA_PAD = 256
RWKV_DECAY_SCALE = 0.6065306597126334
RWKV_INV_BASE = 8
RWKV_CHUNKS_PER_ITER = 4

CONV_DIM = 512
CONV_WIDTH = 31
CONV_HALO = 32

MLA_DIM = 1024
V_HEAD = 128
MLA_HEADS = 8
QK_NOPE = 128
QK_ROPE = 64
Q_LORA = 512
KV_LORA = 256
ROPE_THETA = 10000.0
QK_PAD = 256
ATTN_HEADS_PER_STEP = 2
MLA_IN_PAD = 1024

RWKV_IN = 3 * RWKV_DIM + DECAY_LORA + ICLR_LORA + GATE_LORA
CONV_IN = 2 * CONV_DIM
N_IN = RWKV_IN + CONV_IN + Q_LORA + KV_LORA + QK_ROPE

LANE = 128
V7X_VMEM_LIMIT = 56 * 1024 * 1024
MLP_VMEM_LIMIT = 60 * 1024 * 1024
MASK_VALUE = -1e30
LOG2_E = 1.4426950408889634


def _params(semantics, vmem=V7X_VMEM_LIMIT):
    return pltpu.CompilerParams(dimension_semantics=semantics, vmem_limit_bytes=vmem)


def _resident(shape):
    nd = len(shape)
    return pl.BlockSpec(shape, lambda *_: (0,) * nd, pipeline_mode=pl.Buffered(1))


def _dot(a, b):
    return jnp.dot(a, b, preferred_element_type=F32)


def _dot_nt(a, b):
    return lax.dot_general(a, b, (((1,), (1,)), ((), ())), preferred_element_type=F32)


def _split(x):
    hi = x.astype(BF16)
    lo = (x - hi.astype(F32)).astype(BF16)
    return hi, lo


def _dot3(a, b):
    ah, al = _split(a)
    bh, bl = _split(b)
    return _dot(jnp.concatenate([ah, al, ah], axis=1), jnp.concatenate([bh, bh, bl], axis=0))


def _dot3_nt(a, b):
    ah, al = _split(a)
    bh, bl = _split(b)
    return _dot_nt(jnp.concatenate([ah, al, ah], axis=1), jnp.concatenate([bh, bh, bl], axis=1))


def _dot2_exact_rhs(a, b_bf16):
    ah, al = _split(a)
    return _dot(jnp.concatenate([ah, al], axis=1), jnp.concatenate([b_bf16] * 2, axis=0))


def _rms(x, eps):
    return x * lax.rsqrt(jnp.mean(x * x, axis=-1, keepdims=True) + eps)


def _sigmoid(x):
    return 0.5 * jnp.tanh(0.5 * x) + 0.5


def _iota(shape, dim):
    return lax.broadcasted_iota(jnp.int32, shape, dim)


def _ada_kernel(c_ref, w_ref, b_ref, o_ref):
    c = c_ref[...]
    o_ref[...] = _dot3(c * _sigmoid(c), w_ref[...]) + b_ref[...]


def _ada_call(c, ada_w, ada_b):
    depth, d, n = ada_w.shape
    bsz = c.shape[0]
    tn = 512
    return pl.pallas_call(
        _ada_kernel,
        out_shape=jax.ShapeDtypeStruct((depth, bsz, n), F32),
        grid=(depth, n // tn),
        in_specs=[
            pl.BlockSpec((bsz, d), lambda l, j: (0, 0)),
            pl.BlockSpec((None, d, tn), lambda l, j: (l, 0, j)),
            pl.BlockSpec((None, 1, tn), lambda l, j: (l, 0, j)),
        ],
        out_specs=pl.BlockSpec((None, bsz, tn), lambda l, j: (l, 0, j)),
        compiler_params=_params(("parallel", "parallel")),
        name="ada_mod",
    )(c, ada_w, ada_b.reshape(depth, 1, n))


def _rope_kernel(pos_ref, invf_ref, cos_ref, sin_ref):
    ang = pos_ref[...] * invf_ref[...]
    cos_ref[...] = jnp.cos(ang)
    sin_ref[...] = jnp.sin(ang)


def _rope_call(positions):
    bsz, s = positions.shape
    ts = min(s, 1024)
    inv_freq = ROPE_THETA ** (-jnp.arange(0, QK_ROPE, 2, dtype=F32) / QK_ROPE)
    invf = jnp.tile(inv_freq, LANE // (QK_ROPE // 2)).reshape(1, LANE)
    pos = positions.astype(F32).reshape(bsz, s, 1)
    shp = jax.ShapeDtypeStruct((bsz, s, LANE), F32)
    return pl.pallas_call(
        _rope_kernel,
        out_shape=(shp, shp),
        grid=(bsz, s // ts),
        in_specs=[
            pl.BlockSpec((None, ts, 1), lambda b, i: (b, i, 0)),
            pl.BlockSpec((1, LANE), lambda b, i: (0, 0)),
        ],
        out_specs=(
            pl.BlockSpec((None, ts, LANE), lambda b, i: (b, i, 0)),
            pl.BlockSpec((None, ts, LANE), lambda b, i: (b, i, 0)),
        ),
        compiler_params=_params(("parallel", "parallel")),
        name="rope_tables",
    )(pos, invf)


def _inproj_kernel(x_ref, g_ref, sc_ref, sh_ref, w_rkv, w_lora, w_conv, w_mla,
                   o_rkv, o_lora, o_conv, o_mla):
    x = x_ref[...]
    h = _rms(x, NORM_EPS) * g_ref[...] * (1.0 + sc_ref[...]) + sh_ref[...]
    hb = h.astype(BF16)
    o_rkv[...] = _dot(hb, w_rkv[...])
    o_lora[...] = _dot(hb, w_lora[...])
    o_conv[...] = _dot(hb, w_conv[...])
    o_mla[...] = _dot(hb, w_mla[...])


def _inproj_call(x, g, sc, sh, w_rkv, w_lora, w_conv, w_mla):
    bsz, s, d = x.shape
    tm = min(s, 256)
    widths = (w_rkv.shape[1], w_lora.shape[1], w_conv.shape[1], w_mla.shape[1])
    tok = lambda n: pl.BlockSpec((None, tm, n), lambda b, i: (b, i, 0))
    per_b = pl.BlockSpec((None, 1, d), lambda b, i: (b, 0, 0))
    return pl.pallas_call(
        _inproj_kernel,
        out_shape=tuple(jax.ShapeDtypeStruct((bsz, s, n), F32) for n in widths),
        grid=(bsz, s // tm),
        in_specs=[tok(d), _resident((1, d)), per_b, per_b,
                  _resident(w_rkv.shape), _resident(w_lora.shape),
                  _resident(w_conv.shape), _resident(w_mla.shape)],
        out_specs=tuple(tok(n) for n in widths),
        compiler_params=_params(("parallel", "parallel")),
        name="norm_inproj",
    )(x, g, sc, sh, w_rkv, w_lora, w_conv, w_mla)


def _block_diag(y):
    lane = _iota(y.shape, 1)
    zero = jnp.zeros_like(y)
    return jnp.concatenate([jnp.where(lane < RWKV_HEAD, y, zero),
                            jnp.where(lane >= RWKV_HEAD, y, zero)], axis=0)


def _packed_mm(x, ys, single_pass=False):
    if single_pass:
        lhs = x.astype(BF16)
        cols = [_block_diag(y.astype(BF16)) for y in ys]
    else:
        xh, xl = _split(x)
        lhs = jnp.concatenate([xh, xl, xh], axis=1)
        cols = []
        for y in ys:
            yh, yl = _split(y)
            cols.append(jnp.concatenate([_block_diag(yh), _block_diag(yh), _block_diag(yl)], axis=0))
    out = _dot(lhs, cols[0] if len(cols) == 1 else jnp.concatenate(cols, axis=1))
    return [out[:, i * LANE:(i + 1) * LANE] for i in range(len(ys))]


def _head_sum_matrix(scale):
    r = _iota((LANE, LANE), 0) // RWKV_HEAD
    c = _iota((LANE, LANE), 1) // RWKV_HEAD
    return jnp.where(r == c, scale, 0.0).astype(BF16)


def _rwkv_kernel(*refs, has_vres, tt):
    if has_vres:
        (prkv_ref, plora_ref, vfirst_ref, mu_rkv, mu_lora, wc_ref, w0_ref, a0_ref, v0_ref,
         kk_ref, ka_ref, rk_ref, lng_ref, lnb_ref, y_ref,
         prev_rkv, prev_lora, h_scr, at_s, rt_s, bt_s, kt_s, bp_s, kp_s, v_s, pf_s, o_s) = refs
    else:
        (prkv_ref, plora_ref, mu_rkv, mu_lora, wc_ref, w0_ref, a0_ref,
         kk_ref, ka_ref, rk_ref, lng_ref, lnb_ref, y_ref, vout_ref,
         prev_rkv, prev_lora, h_scr, at_s, rt_s, bt_s, kt_s, bp_s, kp_s, v_s, pf_s, o_s) = refs
    n_chunks = tt // CHUNK

    @pl.when(pl.program_id(1) == 0)
    def _():
        prev_rkv[...] = jnp.zeros_like(prev_rkv)
        prev_lora[...] = jnp.zeros_like(prev_lora)
        h_scr[...] = jnp.zeros_like(h_scr)

    def shift_mix(p_ref, prev_ref, mu_ref):
        p = p_ref[...]
        row = _iota(p.shape, 0)
        prev = jnp.where(row == 0, prev_ref[0:1, :], pltpu.roll(p, 1, 0))
        prev_ref[0:1, :] = p[tt - 1:tt, :]
        return p + (prev - p) * mu_ref[...]

    xs = shift_mix(prkv_ref, prev_rkv, mu_rkv)
    xl = shift_mix(plora_ref, prev_lora, mu_lora)
    r = xs[:, 0:RWKV_DIM]
    k = xs[:, RWKV_DIM:2 * RWKV_DIM]
    v = xs[:, 2 * RWKV_DIM:3 * RWKV_DIM]

    lane = _iota(xl.shape, 1)
    o_w, o_a, o_g = DECAY_LORA, DECAY_LORA + ICLR_LORA, DECAY_LORA + ICLR_LORA + GATE_LORA
    act = jnp.where(lane < o_w, jnp.tanh(xl),
                    jnp.where((lane >= o_a) & (lane < o_g), _sigmoid(xl), xl))
    act_h, act_l = _split(act)
    lora = _dot(jnp.concatenate([act_h, act_l, act_h], axis=1), wc_ref[...])
    z = w0_ref[...] + lora[:, 0:RWKV_DIM]
    lw = -RWKV_DECAY_SCALE * _sigmoid(z)
    a_ic = _sigmoid(a0_ref[...] + lora[:, RWKV_DIM:2 * RWKV_DIM])
    gate = lora[:, 2 * RWKV_DIM:3 * RWKV_DIM]
    if has_vres:
        v = v + (vfirst_ref[...] - v) * _sigmoid(v0_ref[...] + lora[:, 3 * RWKV_DIM:4 * RWKV_DIM])
    else:
        vout_ref[...] = v

    ones_bd = _head_sum_matrix(1.0)
    mean_bd = _head_sum_matrix(1.0 / RWKV_HEAD)

    def per_head(x, mat, split=False):
        mm = _dot2_exact_rhs if split else (lambda a, b: _dot(a.astype(BF16), b))
        return jnp.concatenate(
            [mm(x[:, p * LANE:(p + 1) * LANE], mat) for p in range(RWKV_PAIRS)], axis=1)

    kk = k * kk_ref[...]
    kk = kk * lax.rsqrt(jnp.maximum(per_head(kk * kk, ones_bd), 1e-24))
    k2 = k * (1.0 + (a_ic - 1.0) * ka_ref[...])
    a_vec = -kk
    b_vec = kk * a_ic
    bonus = per_head(r * k2 * rk_ref[...], ones_bd) * v

    tr = _iota((tt, tt), 0)
    tc = _iota((tt, tt), 1)
    tri = jnp.where((tr // CHUNK == tc // CHUNK) & (tc <= tr), 1.0, 0.0).astype(BF16)
    cum = _dot2_exact_rhs_left(tri, lw)
    cum3 = cum.reshape(n_chunks, CHUNK, RWKV_DIM)
    cum_end = jnp.broadcast_to(cum3[:, CHUNK - 1:CHUNK, :], cum3.shape).reshape(tt, RWKV_DIM)
    pf = jnp.exp(cum)
    pinv = jnp.exp(-cum)
    pprev = jnp.exp(cum - lw)
    pend = jnp.exp(cum_end - cum)

    def put(dst, val):
        for p in range(RWKV_PAIRS):
            dst[p] = val[:, p * LANE:(p + 1) * LANE]

    put(at_s, a_vec * pprev)
    put(rt_s, r * pf)
    put(bt_s, b_vec * pinv)
    put(kt_s, k2 * pinv)
    put(bp_s, b_vec * pend)
    put(kp_s, k2 * pend)
    put(v_s, v)
    put(pf_s, pf)

    t_loc = _iota((CHUNK, LANE), 0)
    s_loc = _iota((CHUNK, LANE), 1) % RWKV_HEAD
    strict = s_loc < t_loc
    incl = s_loc <= t_loc
    eye_p = jnp.where(s_loc == t_loc, 1.0, 0.0)
    same_block = lambda n: (t_loc // n) == (s_loc // n)
    in_base = same_block(RWKV_INV_BASE)
    merge_masks = []
    n = RWKV_INV_BASE
    while n < CHUNK:
        merge_masks.append(same_block(2 * n) & jnp.logical_not(same_block(n)))
        n *= 2
    rr = _iota((LANE, LANE), 0)
    cc = _iota((LANE, LANE), 1)
    same_head = (rr // RWKV_HEAD) == (cc // RWKV_HEAD)
    diag = rr == cc

    group = min(n_chunks, RWKV_CHUNKS_PER_ITER)

    def chunk_group(cg, carry):
        streams = []
        for ci in range(group):
            c = cg * group + ci
            rows = pl.ds(pl.multiple_of(c * CHUNK, CHUNK), CHUNK)
            last = pl.ds(c * CHUNK + CHUNK - 1, 1)
            streams += [(p, rows, last) for p in range(RWKV_PAIRS)]
        ns = range(len(streams))
        load = lambda ref: [ref[p, rows, :] for (p, rows, _) in streams]
        at, rt, bt, kt, bp, kp, vv = (load(r) for r in (at_s, rt_s, bt_s, kt_s, bp_s, kp_s, v_s))
        g = [_dot3_nt(jnp.concatenate([at[i], rt[i]], axis=0),
                      jnp.concatenate([_block_diag(bt[i]), _block_diag(kt[i])], axis=0)) for i in ns]
        aab = [jnp.where(strict, g[i][0:CHUNK, 0:LANE], 0.0) for i in ns]
        aak = [jnp.where(strict, g[i][0:CHUNK, LANE:2 * LANE], 0.0) for i in ns]
        arb = [jnp.where(incl, g[i][CHUNK:2 * CHUNK, 0:LANE], 0.0) for i in ns]
        ark = [jnp.where(incl, g[i][CHUNK:2 * CHUNK, LANE:2 * LANE], 0.0) for i in ns]
        n_diag = [jnp.where(in_base, aab[i], 0.0) for i in ns]
        dpow = [_packed_mm(n_diag[i], [n_diag[i]])[0] for i in ns]
        tinv = [eye_p + n_diag[i] for i in ns]
        akv = [_packed_mm(aak[i], [vv[i]], single_pass=True)[0] for i in ns]
        arkv = [_packed_mm(ark[i], [vv[i]], single_pass=True)[0] for i in ns]
        res = [_packed_mm(dpow[i], [dpow[i], tinv[i]]) for i in ns]
        tinv = [tinv[i] + res[i][1] for i in ns]
        tinv = [tinv[i] + _packed_mm(res[i][0], [tinv[i]])[0] for i in ns]
        for level_mask in merge_masks:
            ct = [_packed_mm(jnp.where(level_mask, aab[i], 0.0), [tinv[i]])[0] for i in ns]
            tinv = [tinv[i] + _packed_mm(tinv[i], [ct[i]])[0] for i in ns]
        wu = [_packed_mm(tinv[i], [at[i], akv[i]]) for i in ns]
        ab = [_packed_mm(arb[i], wu[i], single_pass=True) for i in ns]
        q_m = [rt[i] + ab[i][0] for i in ns]
        y0 = [ab[i][1] + arkv[i] for i in ns]
        upd = [_dot3(jnp.concatenate([bp[i], kp[i]], axis=0).T,
                     jnp.concatenate([jnp.concatenate(wu[i], axis=1),
                                      jnp.concatenate([jnp.zeros_like(vv[i]), vv[i]], axis=1)], axis=0))
               for i in ns]
        h = [h_scr[p] for p in range(RWKV_PAIRS)]
        for i in ns:
            p, rows, last = streams[i]
            m_bd = jnp.where(same_head, upd[i][:, 0:LANE], 0.0) + jnp.where(diag, pf_s[p, last, :], 0.0)
            qh_mh = _dot3(jnp.concatenate([q_m[i], m_bd], axis=0), h[p])
            o_s[rows, p * LANE:(p + 1) * LANE] = qh_mh[0:CHUNK] + y0[i]
            h[p] = qh_mh[CHUNK:CHUNK + LANE] + jnp.where(same_head, upd[i][:, LANE:2 * LANE], 0.0)
        for p in range(RWKV_PAIRS):
            h_scr[p] = h[p]
        return carry

    lax.fori_loop(0, n_chunks // group, chunk_group, 0)

    o = o_s[...]
    mean = per_head(o, mean_bd, split=True)
    oc = o - mean
    var = per_head(oc * oc, mean_bd)
    on = oc * lax.rsqrt(var + RWKV_GN_EPS) * lng_ref[...] + lnb_ref[...]
    y_ref[...] = ((on + bonus) * gate).astype(y_ref.dtype)


def _dot2_exact_rhs_left(m_bf16, x):
    xh, xl = _split(x)
    return _dot(jnp.concatenate([m_bf16] * 2, axis=1), jnp.concatenate([xh, xl], axis=0))


def _rwkv_call(p_rkv, p_lora, v_first, prm):
    bsz, s, _ = p_rkv.shape
    tt = min(s, 256)
    has_vres = v_first is not None
    tok = lambda n: pl.BlockSpec((None, tt, n), lambda b, i: (b, i, 0))
    row = lambda n: _resident((1, n))
    in_specs = [tok(3 * RWKV_DIM), tok(LORA_PAD)]
    args = [p_rkv, p_lora]
    if has_vres:
        in_specs.append(tok(RWKV_DIM))
        args.append(v_first)
    in_specs += [row(3 * RWKV_DIM), row(LORA_PAD), _resident((3 * LORA_PAD, 4 * RWKV_DIM)),
                 row(RWKV_DIM), row(RWKV_DIM)]
    args += [prm["mu_rkv"], prm["mu_lora"], prm["w_lora_up"], prm["w0"], prm["a0"]]
    if has_vres:
        in_specs.append(row(RWKV_DIM))
        args.append(prm["v0"])
    in_specs += [row(RWKV_DIM)] * 5
    args += [prm["k_k"], prm["k_a"], prm["r_k"], prm["lnx_g"], prm["lnx_b"]]
    y_shape = jax.ShapeDtypeStruct((bsz, s, RWKV_DIM), BF16)
    if has_vres:
        out_shape, out_specs = y_shape, tok(RWKV_DIM)
    else:
        out_shape = (y_shape, jax.ShapeDtypeStruct((bsz, s, RWKV_DIM), F32))
        out_specs = (tok(RWKV_DIM), tok(RWKV_DIM))
    pair_tile = pltpu.VMEM((RWKV_PAIRS, tt, LANE), F32)
    out = pl.pallas_call(
        functools.partial(_rwkv_kernel, has_vres=has_vres, tt=tt),
        out_shape=out_shape,
        grid=(bsz, s // tt),
        in_specs=in_specs,
        out_specs=out_specs,
        scratch_shapes=[
            pltpu.VMEM((8, 3 * RWKV_DIM), F32),
            pltpu.VMEM((8, LORA_PAD), F32),
            pltpu.VMEM((RWKV_PAIRS, LANE, LANE), F32),
        ] + [pair_tile] * 8 + [pltpu.VMEM((tt, RWKV_DIM), F32)],
        compiler_params=_params(("parallel", "arbitrary")),
        name="rwkv7_mixer",
    )(*args)
    return out if has_vres else out


def _conv_kernel(p_ref, w_ref, b_ref, g_ref, be_ref, o_ref, ubuf, *, tt):
    @pl.when(pl.program_id(1) == 0)
    def _():
        ubuf[0:CONV_HALO, :] = jnp.zeros((CONV_HALO, CONV_DIM), F32)

    p = p_ref[...]
    ubuf[CONV_HALO:CONV_HALO + tt, :] = p[:, 0:CONV_DIM] * _sigmoid(p[:, CONV_DIM:2 * CONV_DIM])
    sub = 8
    base = CONV_HALO - sub
    ext = tt + sub
    acc = None
    for b in range(sub):
        part = None
        for a in range((CONV_WIDTH - 1 - b) // sub + 1):
            j = CONV_WIDTH - 1 - (sub * a + b)
            term = ubuf[base - sub * a:base - sub * a + ext, :] * w_ref[j:j + 1, :]
            part = term if part is None else part + term
        if b:
            part = pltpu.roll(part, b, 0)
        acc = part if acc is None else acc + part
    acc = acc[sub:sub + tt, :] + b_ref[...]
    ubuf[0:CONV_HALO, :] = ubuf[tt:tt + CONV_HALO, :]
    mean = jnp.mean(acc, axis=-1, keepdims=True)
    xc = acc - mean
    var = jnp.mean(xc * xc, axis=-1, keepdims=True)
    u = xc * lax.rsqrt(var + LN_EPS) * g_ref[...] + be_ref[...]
    o_ref[...] = (u * _sigmoid(u)).astype(o_ref.dtype)


def _conv_call(p_conv, conv_w, conv_b, ln_g, ln_b):
    bsz, s, _ = p_conv.shape
    tt = min(s, 512)
    wpad = jnp.zeros((CONV_HALO, CONV_DIM), F32).at[:CONV_WIDTH].set(conv_w)
    row = _resident((1, CONV_DIM))
    return pl.pallas_call(
        functools.partial(_conv_kernel, tt=tt),
        out_shape=jax.ShapeDtypeStruct((bsz, s, CONV_DIM), BF16),
        grid=(bsz, s // tt),
        in_specs=[pl.BlockSpec((None, tt, 2 * CONV_DIM), lambda b, i: (b, i, 0)),
                  _resident((CONV_HALO, CONV_DIM)), row, row, row],
        out_specs=pl.BlockSpec((None, tt, CONV_DIM), lambda b, i: (b, i, 0)),
        scratch_shapes=[pltpu.VMEM((tt + CONV_HALO, CONV_DIM), F32)],
        compiler_params=_params(("parallel", "arbitrary")),
        name="conformer_conv",
    )(p_conv, wpad, conv_b.reshape(1, -1), ln_g.reshape(1, -1), ln_b.reshape(1, -1))


def _mla_prep_kernel(p_ref, cos_ref, sin_ref, qg_ref, kvg_ref, wq_ref, wqs_ref, wkv_ref,
                     q_ref, k_ref, v_ref):
    p = p_ref[...]
    cos = cos_ref[...]
    sin = sin_ref[...]
    qc = (_rms(p[:, 0:Q_LORA], NORM_EPS) * qg_ref[...]).astype(BF16)
    kvc = (_rms(p[:, Q_LORA:Q_LORA + KV_LORA], NORM_EPS) * kvg_ref[...]).astype(BF16)
    q = _dot(qc, wq_ref[...])
    qs = _dot(qc, wqs_ref[...])
    kv = _dot(kvc, wkv_ref[...])
    o_kr = Q_LORA + KV_LORA
    k_rope = (p[:, o_kr:o_kr + LANE] * cos + p[:, o_kr + LANE:o_kr + 2 * LANE] * sin).astype(BF16)
    for h in range(MLA_HEADS):
        b0 = h * QK_PAD
        q_ref[:, b0:b0 + LANE] = q[:, b0:b0 + LANE].astype(BF16)
        q_ref[:, b0 + LANE:b0 + 2 * LANE] = (
            q[:, b0 + LANE:b0 + 2 * LANE] * cos + qs[:, h * LANE:(h + 1) * LANE] * sin).astype(BF16)
        k_ref[:, b0:b0 + LANE] = kv[:, b0:b0 + LANE].astype(BF16)
        k_ref[:, b0 + LANE:b0 + 2 * LANE] = k_rope
        v_ref[:, h * V_HEAD:(h + 1) * V_HEAD] = kv[:, b0 + LANE:b0 + 2 * LANE].astype(BF16)


def _mla_prep_call(p_mla, cos, sin, qg, kvg, wq, wqs, wkv):
    bsz, s, _ = p_mla.shape
    tm = min(s, 256)
    tok = lambda n: pl.BlockSpec((None, tm, n), lambda b, i: (b, i, 0))
    return pl.pallas_call(
        _mla_prep_kernel,
        out_shape=(jax.ShapeDtypeStruct((bsz, s, MLA_HEADS * QK_PAD), BF16),
                   jax.ShapeDtypeStruct((bsz, s, MLA_HEADS * QK_PAD), BF16),
                   jax.ShapeDtypeStruct((bsz, s, MLA_DIM), BF16)),
        grid=(bsz, s // tm),
        in_specs=[tok(MLA_IN_PAD), tok(LANE), tok(LANE),
                  _resident((1, Q_LORA)), _resident((1, KV_LORA)),
                  _resident(wq.shape), _resident(wqs.shape), _resident(wkv.shape)],
        out_specs=(tok(MLA_HEADS * QK_PAD), tok(MLA_HEADS * QK_PAD), tok(MLA_DIM)),
        compiler_params=_params(("parallel", "parallel")),
        name="mla_prep",
    )(p_mla, cos, sin, qg, kvg, wq, wqs, wkv)


def _attn_kernel(q_ref, k_ref, v_ref, o_ref, *, tq, nh):
    i = pl.program_id(2)
    heads = range(nh)
    q = [q_ref[:, h * QK_PAD:(h + 1) * QK_PAD] for h in heads]

    def step(j, carry, masked):
        rows = pl.ds(pl.multiple_of(j * tq, tq), tq)
        sc = [_dot_nt(q[h], k_ref[rows, h * QK_PAD:(h + 1) * QK_PAD]) for h in heads]
        if masked:
            visible = (_iota((tq, tq), 1) // CHUNK) <= (_iota((tq, tq), 0) // CHUNK)
            sc = [jnp.where(visible, s, MASK_VALUE) for s in sc]
        m_new = [jnp.maximum(carry[h][0], jnp.max(sc[h], axis=-1, keepdims=True)) for h in heads]
        alpha = [jnp.exp2(carry[h][0] - m_new[h]) for h in heads]
        pr = [jnp.exp2(sc[h] - m_new[h]) for h in heads]
        l_new = [alpha[h] * carry[h][1] + jnp.sum(pr[h], axis=-1, keepdims=True) for h in heads]
        pv = [_dot(pr[h].astype(BF16), v_ref[rows, h * V_HEAD:(h + 1) * V_HEAD]) for h in heads]
        return tuple((m_new[h], l_new[h], alpha[h] * carry[h][2] + pv[h]) for h in heads)

    init = (jnp.full((tq, 1), MASK_VALUE, F32), jnp.zeros((tq, 1), F32), jnp.zeros((tq, V_HEAD), F32))
    carry = lax.fori_loop(0, i, lambda j, c: step(j, c, False), (init,) * nh)
    carry = step(i, carry, True)
    for h in heads:
        o_ref[:, h * V_HEAD:(h + 1) * V_HEAD] = (carry[h][2] / carry[h][1]).astype(o_ref.dtype)


def _attn_call(q, k, v):
    bsz, s, _ = v.shape
    tq = min(s, 512)
    nh = ATTN_HEADS_PER_STEP
    return pl.pallas_call(
        functools.partial(_attn_kernel, tq=tq, nh=nh),
        out_shape=jax.ShapeDtypeStruct((bsz, s, MLA_DIM), BF16),
        grid=(bsz, MLA_HEADS // nh, s // tq),
        in_specs=[pl.BlockSpec((None, tq, nh * QK_PAD), lambda b, h, i: (b, i, h)),
                  pl.BlockSpec((None, s, nh * QK_PAD), lambda b, h, i: (b, 0, h)),
                  pl.BlockSpec((None, s, nh * V_HEAD), lambda b, h, i: (b, 0, h))],
        out_specs=pl.BlockSpec((None, tq, nh * V_HEAD), lambda b, h, i: (b, i, h)),
        compiler_params=_params(("parallel", "parallel", "arbitrary")),
        name="mla_attention",
    )(q, k, v)


def _outproj_kernel(ya_ref, yb_ref, yc_ref, x_ref, wo_ref, gt_ref, g_ref, sc_ref, sh_ref,
                    x1_ref, h2_ref):
    o1, o2 = RWKV_DIM, RWKV_DIM + CONV_DIM
    y = (_dot(ya_ref[...], wo_ref[0:o1, :]) + _dot(yb_ref[...], wo_ref[o1:o2, :])
         + _dot(yc_ref[...], wo_ref[o2:o2 + MLA_DIM, :]))
    x1 = x_ref[...] + gt_ref[...] * y
    x1_ref[...] = x1
    h2_ref[...] = (_rms(x1, NORM_EPS) * g_ref[...] * (1.0 + sc_ref[...]) + sh_ref[...]).astype(BF16)


def _outproj_call(ya, yb, yc, x, wo, gt, g, sc, sh):
    bsz, s, d = x.shape
    tm = min(s, 512)
    tok = lambda n: pl.BlockSpec((None, tm, n), lambda b, i: (b, i, 0))
    per_b = pl.BlockSpec((None, 1, d), lambda b, i: (b, 0, 0))
    return pl.pallas_call(
        _outproj_kernel,
        out_shape=(jax.ShapeDtypeStruct((bsz, s, d), F32), jax.ShapeDtypeStruct((bsz, s, d), BF16)),
        grid=(bsz, s // tm),
        in_specs=[tok(RWKV_DIM), tok(CONV_DIM), tok(MLA_DIM), tok(d), _resident(wo.shape),
                  per_b, _resident((1, d)), per_b, per_b],
        out_specs=(tok(d), tok(d)),
        compiler_params=_params(("parallel", "parallel")),
        name="outproj_norm2",
    )(ya, yb, yc, x, wo, gt, g, sc, sh)


def _mlp_kernel(h_ref, x_ref, w1_ref, w2_ref, gt_ref, fg_ref, o_ref, *, final):
    f = pl.program_id(2)

    @pl.when(f == 0)
    def _():
        o_ref[...] = jnp.zeros_like(o_ref)

    a = jnp.maximum(_dot(h_ref[...], w1_ref[...]), 0.0)
    o_ref[...] += _dot((a * a).astype(BF16), w2_ref[...])

    @pl.when(f == pl.num_programs(2) - 1)
    def _():
        xo = x_ref[...] + gt_ref[...] * o_ref[...]
        if final:
            xo = _rms(xo, NORM_EPS) * fg_ref[...]
        o_ref[...] = xo


def _mlp_call(h2, x1, w1, w2, gt, fg, final):
    bsz, s, d = x1.shape
    tm = min(s, 512)
    tf = 2048
    tok_map = lambda b, i, f: (b, i, 0)
    tok = lambda: pl.BlockSpec((None, tm, d), tok_map)
    return pl.pallas_call(
        functools.partial(_mlp_kernel, final=final),
        out_shape=jax.ShapeDtypeStruct((bsz, s, d), F32),
        grid=(bsz, s // tm, D_FF // tf),
        in_specs=[tok(), tok(),
                  pl.BlockSpec((d, tf), lambda b, i, f: (0, f)),
                  pl.BlockSpec((tf, d), lambda b, i, f: (f, 0)),
                  pl.BlockSpec((None, 1, d), lambda b, i, f: (b, 0, 0)),
                  pl.BlockSpec((1, d), lambda b, i, f: (0, 0))],
        out_specs=tok(),
        compiler_params=_params(("parallel", "parallel", "arbitrary"), MLP_VMEM_LIMIT),
        name="relu2_mlp",
    )(h2, x1, w1, w2, gt, fg)


def _pad_cols(w, n):
    return jnp.pad(w, ((0, 0), (0, n - w.shape[1])))


def _rotate_half_cols(w):
    half = w.shape[-1] // 2
    return jnp.concatenate([-w[..., half:], w[..., :half]], axis=-1)


def _layer_weights(l, w_in, w_in_vres, rwkv_mu, vres_mu, decay_up, iclr_up, gate_up, vres_up,
                   w_qb, w_kvb):
    d = D_MODEL
    wl = w_in[l]
    o_lora = 3 * RWKV_DIM
    o_conv = RWKV_IN
    o_mla = RWKV_IN + CONV_IN
    has_vres = l > 0
    lora_cols = [wl[:, o_lora:o_conv]]
    mu_cols = [rwkv_mu[l, o_lora:o_conv]]
    if has_vres:
        lora_cols.append(w_in_vres[l - 1])
        mu_cols.append(vres_mu[l - 1])
    w_lora = _pad_cols(jnp.concatenate(lora_cols, axis=1), LORA_PAD)
    mu_lora = jnp.pad(jnp.concatenate(mu_cols), (0, LORA_PAD - sum(m.shape[0] for m in mu_cols)))
    mla = wl[:, o_mla:N_IN]
    kr = mla[:, Q_LORA + KV_LORA:]
    zpad = jnp.zeros((d, LANE - QK_ROPE), F32)
    w_mla = jnp.concatenate([mla[:, :Q_LORA + KV_LORA], kr, zpad, _rotate_half_cols(kr), zpad], axis=1)

    up = jnp.zeros((LORA_PAD, 4 * RWKV_DIM), F32)
    o_a, o_g = DECAY_LORA, DECAY_LORA + ICLR_LORA
    o_v = o_g + GATE_LORA
    up = up.at[0:o_a, 0:RWKV_DIM].set(decay_up[l])
    up = up.at[o_a:o_g, RWKV_DIM:2 * RWKV_DIM].set(iclr_up[l])
    up = up.at[o_g:o_v, 2 * RWKV_DIM:3 * RWKV_DIM].set(gate_up[l])
    if has_vres:
        up = up.at[o_v:o_v + VRES_LORA, 3 * RWKV_DIM:].set(vres_up[l - 1])
    up_hi = up.astype(BF16)

    scale = (QK_NOPE + QK_ROPE) ** -0.5 * LOG2_E
    wq = (w_qb[l] * scale).reshape(Q_LORA, MLA_HEADS, QK_NOPE + QK_ROPE)
    nope, rope = wq[:, :, :QK_NOPE], wq[:, :, QK_NOPE:]
    z = jnp.zeros((Q_LORA, MLA_HEADS, LANE - QK_ROPE), F32)
    wq_main = jnp.concatenate([nope, rope, z], axis=2).reshape(Q_LORA, MLA_HEADS * QK_PAD)
    wq_rot = jnp.concatenate([_rotate_half_cols(rope), z], axis=2).reshape(Q_LORA, MLA_HEADS * LANE)
    return dict(
        w_rkv=wl[:, :o_lora].astype(BF16), w_lora=w_lora.astype(BF16),
        w_conv=wl[:, o_conv:o_mla].astype(BF16), w_mla=w_mla.astype(BF16),
        mu_rkv=rwkv_mu[l, :o_lora].reshape(1, -1), mu_lora=mu_lora.reshape(1, -1),
        w_lora_up=jnp.concatenate([up_hi, up_hi, (up - up_hi.astype(F32)).astype(BF16)], axis=0),
        wq=wq_main.astype(BF16), wq_rot=wq_rot.astype(BF16),
        wkv=w_kvb[l].astype(BF16))


def kernel(x, c, positions, ada_w, ada_b, norm1_g, norm2_g, final_g, w_in, w_in_vres, rwkv_mu,
           vres_mu, decay_w0, decay_up, iclr_a0, iclr_up, gate_up, vres_v0, vres_up, k_k, k_a, r_k,
           lnx_g, lnx_b, conv_w, conv_b, conv_ln_g, conv_ln_b, q_a_norm_g, w_qb, kv_a_norm_g, w_kvb,
           w_out, mlp_w1, mlp_w2):
    bsz, s, d = x.shape
    mod = _ada_call(c, ada_w, ada_b).reshape(DEPTH, bsz, 6, 1, d)
    cos, sin = _rope_call(positions)
    row = lambda a: a.reshape(1, -1)
    v_first = None
    for l in range(DEPTH):
        sh1, sc1, gt1, sh2, sc2, gt2 = (mod[l, :, j] for j in range(6))
        lw = _layer_weights(l, w_in, w_in_vres, rwkv_mu, vres_mu, decay_up, iclr_up, gate_up,
                            vres_up, w_qb, w_kvb)
        p_rkv, p_lora, p_conv, p_mla = _inproj_call(
            x, row(norm1_g[l]), sc1, sh1, lw["w_rkv"], lw["w_lora"], lw["w_conv"], lw["w_mla"])
        prm = dict(mu_rkv=lw["mu_rkv"], mu_lora=lw["mu_lora"], w_lora_up=lw["w_lora_up"],
                   w0=row(decay_w0[l]), a0=row(iclr_a0[l]), k_k=row(k_k[l]), k_a=row(k_a[l]),
                   r_k=row(r_k[l]), lnx_g=row(lnx_g[l]), lnx_b=row(lnx_b[l]))
        if l == 0:
            y_a, v_first = _rwkv_call(p_rkv, p_lora, None, prm)
        else:
            prm["v0"] = row(vres_v0[l - 1])
            y_a = _rwkv_call(p_rkv, p_lora, v_first, prm)
        y_b = _conv_call(p_conv, conv_w[l], conv_b[l], conv_ln_g[l], conv_ln_b[l])
        q, k, v = _mla_prep_call(p_mla, cos, sin, row(q_a_norm_g[l]), row(kv_a_norm_g[l]),
                                 lw["wq"], lw["wq_rot"], lw["wkv"])
        y_c = _attn_call(q, k, v)
        x1, h2 = _outproj_call(y_a, y_b, y_c, x, w_out[l].astype(BF16), gt1,
                               row(norm2_g[l]), sc2, sh2)
        x = _mlp_call(h2, x1, mlp_w1[l].astype(BF16), mlp_w2[l].astype(BF16), gt2,
                      row(final_g), final=(l == DEPTH - 1))
    return x
```

```python
import functools

import jax
import jax.numpy as jnp
from jax import lax
from jax.experimental import pallas as pl
from jax.experimental.pallas import tpu as pltpu

F32 = jnp.float32
BF16 = jnp.bfloat16

D_MODEL = 2048
DEPTH = 2
CHUNK = 64
NORM_EPS = 1e-6
LN_EPS = 1e-5
D_FF = 4 * D_MODEL

RWKV_HEAD = 64
RWKV_DIM = 512
RWKV_HEADS = 8
RWKV_PAIRS = RWKV_HEADS // 2
DECAY_LORA = 32
ICLR_LORA = 32
VRES_LORA = 32
GATE_LORA = 96
RWKV_GN_EPS = 64e-5
LORA_PAD = 256
RWKV_DECAY_SCALE = 0.6065306597126334
RWKV_INV_BASE = 8
RWKV_INV_PASSES = 3
RWKV_BASE_PASSES = 2
RWKV_CHUNKS_PER_ITER = 4

CONV_DIM = 512
CONV_WIDTH = 31
CONV_HALO = 32

MLA_DIM = 1024
V_HEAD = 128
MLA_HEADS = 8
QK_NOPE = 128
QK_ROPE = 64
Q_LORA = 512
KV_LORA = 256
ROPE_THETA = 10000.0
QK_PAD = 256
ATTN_HEADS_PER_STEP = 2
MLA_IN_PAD = 1024

RWKV_IN = 3 * RWKV_DIM + DECAY_LORA + ICLR_LORA + GATE_LORA
CONV_IN = 2 * CONV_DIM
N_IN = RWKV_IN + CONV_IN + Q_LORA + KV_LORA + QK_ROPE

LANE = 128
V7X_VMEM_LIMIT = 56 * 1024 * 1024
MLP_VMEM_LIMIT = 60 * 1024 * 1024
MASK_VALUE = -1e30
LOG2_E = 1.4426950408889634


def _params(semantics, vmem=V7X_VMEM_LIMIT):
    return pltpu.CompilerParams(dimension_semantics=semantics, vmem_limit_bytes=vmem)


def _resident(shape):
    nd = len(shape)
    return pl.BlockSpec(shape, lambda *_: (0,) * nd, pipeline_mode=pl.Buffered(1))


def _dot(a, b):
    return jnp.dot(a, b, preferred_element_type=F32)


def _dot_nt(a, b):
    return lax.dot_general(a, b, (((1,), (1,)), ((), ())), preferred_element_type=F32)


def _split(x):
    hi = x.astype(BF16)
    lo = (x - hi.astype(F32)).astype(BF16)
    return hi, lo


def _dot3(a, b):
    ah, al = _split(a)
    bh, bl = _split(b)
    return _dot(jnp.concatenate([ah, al, ah], axis=1), jnp.concatenate([bh, bh, bl], axis=0))


def _dot2(a, b):
    bh = b.astype(BF16)
    return _dot(jnp.concatenate(_split(a), axis=1), jnp.concatenate([bh, bh], axis=0))


def _dot2_nt(a, b):
    bh = b.astype(BF16)
    return _dot_nt(jnp.concatenate(_split(a), axis=1), jnp.concatenate([bh, bh], axis=1))


def _dot2_exact_rhs(a, b_bf16):
    ah, al = _split(a)
    return _dot(jnp.concatenate([ah, al], axis=1), jnp.concatenate([b_bf16] * 2, axis=0))


def _rms(x, eps):
    return x * lax.rsqrt(jnp.mean(x * x, axis=-1, keepdims=True) + eps)


def _sigmoid(x):
    return 0.5 * jnp.tanh(0.5 * x) + 0.5


def _iota(shape, dim):
    return lax.broadcasted_iota(jnp.int32, shape, dim)


def _ada_kernel(c_ref, w_ref, b_ref, o_ref):
    c = c_ref[...]
    o_ref[...] = _dot3(c * _sigmoid(c), w_ref[...]) + b_ref[...]


def _ada_call(c, ada_w, ada_b):
    depth, d, n = ada_w.shape
    bsz = c.shape[0]
    tn = 512
    return pl.pallas_call(
        _ada_kernel,
        out_shape=jax.ShapeDtypeStruct((depth, bsz, n), F32),
        grid=(depth, n // tn),
        in_specs=[
            pl.BlockSpec((bsz, d), lambda l, j: (0, 0)),
            pl.BlockSpec((None, d, tn), lambda l, j: (l, 0, j)),
            pl.BlockSpec((None, 1, tn), lambda l, j: (l, 0, j)),
        ],
        out_specs=pl.BlockSpec((None, bsz, tn), lambda l, j: (l, 0, j)),
        compiler_params=_params(("parallel", "parallel")),
        name="ada_mod",
    )(c, ada_w, ada_b.reshape(depth, 1, n))


def _rope_kernel(pos_ref, invf_ref, cos_ref, sin_ref):
    ang = pos_ref[...] * invf_ref[...]
    cos_ref[...] = jnp.cos(ang)
    sin_ref[...] = jnp.sin(ang)


def _rope_call(positions):
    bsz, s = positions.shape
    ts = min(s, 1024)
    inv_freq = ROPE_THETA ** (-jnp.arange(0, QK_ROPE, 2, dtype=F32) / QK_ROPE)
    invf = jnp.tile(inv_freq, LANE // (QK_ROPE // 2)).reshape(1, LANE)
    pos = positions.astype(F32).reshape(bsz, s, 1)
    shp = jax.ShapeDtypeStruct((bsz, s, LANE), F32)
    return pl.pallas_call(
        _rope_kernel,
        out_shape=(shp, shp),
        grid=(bsz, s // ts),
        in_specs=[
            pl.BlockSpec((None, ts, 1), lambda b, i: (b, i, 0)),
            pl.BlockSpec((1, LANE), lambda b, i: (0, 0)),
        ],
        out_specs=(
            pl.BlockSpec((None, ts, LANE), lambda b, i: (b, i, 0)),
            pl.BlockSpec((None, ts, LANE), lambda b, i: (b, i, 0)),
        ),
        compiler_params=_params(("parallel", "parallel")),
        name="rope_tables",
    )(pos, invf)


def _inproj_kernel(x_ref, g_ref, sc_ref, sh_ref, w_rkv, w_lora, w_conv, w_mla,
                   o_rkv, o_lora, o_conv, o_mla):
    x = x_ref[...]
    h = _rms(x, NORM_EPS) * g_ref[...] * (1.0 + sc_ref[...]) + sh_ref[...]
    hb = h.astype(BF16)
    o_rkv[...] = _dot(hb, w_rkv[...])
    o_lora[...] = _dot(hb, w_lora[...])
    o_conv[...] = _dot(hb, w_conv[...]).astype(o_conv.dtype)
    o_mla[...] = _dot(hb, w_mla[...]).astype(o_mla.dtype)


def _inproj_call(x, g, sc, sh, w_rkv, w_lora, w_conv, w_mla):
    bsz, s, d = x.shape
    tm = min(s, 256)
    widths = (w_rkv.shape[1], w_lora.shape[1], w_conv.shape[1], w_mla.shape[1])
    tok = lambda n: pl.BlockSpec((None, tm, n), lambda b, i: (b, i, 0))
    per_b = pl.BlockSpec((None, 1, d), lambda b, i: (b, 0, 0))
    return pl.pallas_call(
        _inproj_kernel,
        out_shape=tuple(jax.ShapeDtypeStruct((bsz, s, n), dt)
                        for n, dt in zip(widths, (F32, F32, BF16, BF16))),
        grid=(bsz, s // tm),
        in_specs=[tok(d), _resident((1, d)), per_b, per_b,
                  _resident(w_rkv.shape), _resident(w_lora.shape),
                  _resident(w_conv.shape), _resident(w_mla.shape)],
        out_specs=tuple(tok(n) for n in widths),
        compiler_params=_params(("parallel", "parallel")),
        name="norm_inproj",
    )(x, g, sc, sh, w_rkv, w_lora, w_conv, w_mla)


def _block_diag(y):
    lane = _iota(y.shape, 1)
    zero = jnp.zeros_like(y)
    return jnp.concatenate([jnp.where(lane < RWKV_HEAD, y, zero),
                            jnp.where(lane >= RWKV_HEAD, y, zero)], axis=0)


def _packed_mm(x, ys, passes=3):
    if passes == 1:
        lhs = x.astype(BF16)
        cols = [_block_diag(y.astype(BF16)) for y in ys]
    elif passes == 2:
        lhs = jnp.concatenate(_split(x), axis=1)
        cols = [jnp.concatenate([_block_diag(y.astype(BF16))] * 2, axis=0) for y in ys]
    else:
        xh, xl = _split(x)
        lhs = jnp.concatenate([xh, xl, xh], axis=1)
        cols = []
        for y in ys:
            yh, yl = _split(y)
            cols.append(jnp.concatenate([_block_diag(yh), _block_diag(yh), _block_diag(yl)], axis=0))
    out = _dot(lhs, cols[0] if len(cols) == 1 else jnp.concatenate(cols, axis=1))
    return [out[:, i * LANE:(i + 1) * LANE] for i in range(len(ys))]


def _head_sum_matrix(scale):
    r = _iota((LANE, LANE), 0) // RWKV_HEAD
    c = _iota((LANE, LANE), 1) // RWKV_HEAD
    return jnp.where(r == c, scale, 0.0).astype(BF16)


def _rwkv_kernel(*refs, has_vres, tt):
    if has_vres:
        (prkv_ref, plora_ref, vfirst_ref, mu_rkv, mu_lora, wc_ref, w0_ref, a0_ref, v0_ref,
         kk_ref, ka_ref, rk_ref, lng_ref, lnb_ref, y_ref,
         prev_rkv, prev_lora, h_scr, at_s, rt_s, bt_s, kt_s, bp_s, kp_s, v_s, pf_s, o_s) = refs
    else:
        (prkv_ref, plora_ref, mu_rkv, mu_lora, wc_ref, w0_ref, a0_ref,
         kk_ref, ka_ref, rk_ref, lng_ref, lnb_ref, y_ref, vout_ref,
         prev_rkv, prev_lora, h_scr, at_s, rt_s, bt_s, kt_s, bp_s, kp_s, v_s, pf_s, o_s) = refs
    n_chunks = tt // CHUNK

    @pl.when(pl.program_id(1) == 0)
    def _():
        prev_rkv[...] = jnp.zeros_like(prev_rkv)
        prev_lora[...] = jnp.zeros_like(prev_lora)
        h_scr[...] = jnp.zeros_like(h_scr)

    def shift_mix(p_ref, prev_ref, mu_ref):
        p = p_ref[...]
        row = _iota(p.shape, 0)
        prev = jnp.where(row == 0, prev_ref[0:1, :], pltpu.roll(p, 1, 0))
        prev_ref[0:1, :] = p[tt - 1:tt, :]
        return p + (prev - p) * mu_ref[...]

    xs = shift_mix(prkv_ref, prev_rkv, mu_rkv)
    xl = shift_mix(plora_ref, prev_lora, mu_lora)
    r = xs[:, 0:RWKV_DIM]
    k = xs[:, RWKV_DIM:2 * RWKV_DIM]
    v = xs[:, 2 * RWKV_DIM:3 * RWKV_DIM]

    lane = _iota(xl.shape, 1)
    o_w, o_a, o_g = DECAY_LORA, DECAY_LORA + ICLR_LORA, DECAY_LORA + ICLR_LORA + GATE_LORA
    act = jnp.where(lane < o_w, jnp.tanh(xl),
                    jnp.where((lane >= o_a) & (lane < o_g), _sigmoid(xl), xl))
    act_h, act_l = _split(act)
    lora = _dot(jnp.concatenate([act_h, act_l, act_h], axis=1), wc_ref[...])
    z = w0_ref[...] + lora[:, 0:RWKV_DIM]
    lw = -RWKV_DECAY_SCALE * _sigmoid(z)
    a_ic = _sigmoid(a0_ref[...] + lora[:, RWKV_DIM:2 * RWKV_DIM])
    gate = lora[:, 2 * RWKV_DIM:3 * RWKV_DIM]
    if has_vres:
        v = v + (vfirst_ref[...] - v) * _sigmoid(v0_ref[...] + lora[:, 3 * RWKV_DIM:4 * RWKV_DIM])
    else:
        vout_ref[...] = v

    ones_bd = _head_sum_matrix(1.0)
    mean_bd = _head_sum_matrix(1.0 / RWKV_HEAD)

    def per_head(x, mat, split=False):
        mm = _dot2_exact_rhs if split else (lambda a, b: _dot(a.astype(BF16), b))
        return jnp.concatenate(
            [mm(x[:, p * LANE:(p + 1) * LANE], mat) for p in range(RWKV_PAIRS)], axis=1)

    kk = k * kk_ref[...]
    kk = kk * lax.rsqrt(jnp.maximum(per_head(kk * kk, ones_bd), 1e-24))
    k2 = k * (1.0 + (a_ic - 1.0) * ka_ref[...])
    a_vec = -kk
    b_vec = kk * a_ic
    bonus = per_head(r * k2 * rk_ref[...], ones_bd) * v

    tr = _iota((tt, tt), 0)
    tc = _iota((tt, tt), 1)
    tri = jnp.where((tr // CHUNK == tc // CHUNK) & (tc <= tr), 1.0, 0.0).astype(BF16)
    cum = _dot2_exact_rhs_left(tri, lw)
    cum3 = cum.reshape(n_chunks, CHUNK, RWKV_DIM)
    cum_end = jnp.broadcast_to(cum3[:, CHUNK - 1:CHUNK, :], cum3.shape).reshape(tt, RWKV_DIM)
    pf = jnp.exp(cum)
    pinv = jnp.exp(-cum)
    pprev = jnp.exp(cum - lw)
    pend = jnp.exp(cum_end - cum)

    def put(dst, val):
        for p in range(RWKV_PAIRS):
            dst[p] = val[:, p * LANE:(p + 1) * LANE]

    put(at_s, a_vec * pprev)
    put(rt_s, r * pf)
    put(bt_s, b_vec * pinv)
    put(kt_s, k2 * pinv)
    put(bp_s, b_vec * pend)
    put(kp_s, k2 * pend)
    put(v_s, v)
    put(pf_s, pf)

    t_loc = _iota((CHUNK, LANE), 0)
    s_loc = _iota((CHUNK, LANE), 1) % RWKV_HEAD
    strict = s_loc < t_loc
    incl = s_loc <= t_loc
    eye_p = jnp.where(s_loc == t_loc, 1.0, 0.0)
    same_block = lambda n: (t_loc // n) == (s_loc // n)
    in_base = same_block(RWKV_INV_BASE)
    merge_masks = []
    n = RWKV_INV_BASE
    while n < CHUNK:
        merge_masks.append(same_block(2 * n) & jnp.logical_not(same_block(n)))
        n *= 2

    group = min(n_chunks, RWKV_CHUNKS_PER_ITER)

    def chunk_group(cg, carry):
        streams = []
        for ci in range(group):
            c = cg * group + ci
            rows = pl.ds(pl.multiple_of(c * CHUNK, CHUNK), CHUNK)
            last = pl.ds(c * CHUNK + CHUNK - 1, 1)
            streams += [(p, rows, last) for p in range(RWKV_PAIRS)]
        ns = range(len(streams))
        load = lambda ref: [ref[p, rows, :] for (p, rows, _) in streams]
        at, rt, bt, kt, bp, kp, vv = (load(r) for r in (at_s, rt_s, bt_s, kt_s, bp_s, kp_s, v_s))
        g = [_dot2_nt(jnp.concatenate([at[i], rt[i]], axis=0),
                      jnp.concatenate([_block_diag(bt[i]), _block_diag(kt[i])], axis=0)) for i in ns]
        aab = [jnp.where(strict, g[i][0:CHUNK, 0:LANE], 0.0) for i in ns]
        aak = [jnp.where(strict, g[i][0:CHUNK, LANE:2 * LANE], 0.0) for i in ns]
        arb = [jnp.where(incl, g[i][CHUNK:2 * CHUNK, 0:LANE], 0.0) for i in ns]
        ark = [jnp.where(incl, g[i][CHUNK:2 * CHUNK, LANE:2 * LANE], 0.0) for i in ns]
        n_diag = [jnp.where(in_base, aab[i], 0.0) for i in ns]
        inv, base = RWKV_INV_PASSES, RWKV_BASE_PASSES
        dpow = [_packed_mm(n_diag[i], [n_diag[i]], passes=base)[0] for i in ns]
        tinv = [eye_p + n_diag[i] for i in ns]
        akv = [_packed_mm(aak[i], [vv[i]], passes=1)[0] for i in ns]
        arkv = [_packed_mm(ark[i], [vv[i]], passes=1)[0] for i in ns]
        res = [_packed_mm(dpow[i], [dpow[i], tinv[i]], passes=base) for i in ns]
        tinv = [tinv[i] + res[i][1] for i in ns]
        tinv = [tinv[i] + _packed_mm(res[i][0], [tinv[i]], passes=base)[0] for i in ns]
        for level_mask in merge_masks:
            ct = [_packed_mm(jnp.where(level_mask, aab[i], 0.0), [tinv[i]], passes=inv)[0] for i in ns]
            tinv = [tinv[i] + _packed_mm(tinv[i], [ct[i]], passes=inv)[0] for i in ns]
        wu = [_packed_mm(tinv[i], [at[i], akv[i]], passes=inv) for i in ns]
        ab = [_packed_mm(arb[i], wu[i], passes=1) for i in ns]
        q_m = [rt[i] + ab[i][0] for i in ns]
        y0 = [ab[i][1] + arkv[i] for i in ns]
        upd = [_dot2(jnp.concatenate([bp[i], kp[i]], axis=0).T,
                     jnp.concatenate([jnp.concatenate(wu[i], axis=1),
                                      jnp.concatenate([jnp.zeros_like(vv[i]), vv[i]], axis=1)], axis=0))
               for i in ns]
        first_head = _iota((CHUNK, 2 * LANE), 1) % LANE < RWKV_HEAD
        upd = [jnp.where(first_head, upd[i][0:RWKV_HEAD], upd[i][RWKV_HEAD:2 * RWKV_HEAD]) for i in ns]
        h = [h_scr[p] for p in range(RWKV_PAIRS)]
        for i in ns:
            p, rows, last = streams[i]
            m_p = upd[i][:, 0:LANE] + jnp.where(s_loc == t_loc, pf_s[p, last, :], 0.0)
            (qh_mh,) = _packed_mm(jnp.concatenate([q_m[i], m_p], axis=0), [h[p]], passes=3)
            o_s[rows, p * LANE:(p + 1) * LANE] = qh_mh[0:CHUNK] + y0[i]
            h[p] = qh_mh[CHUNK:2 * CHUNK] + upd[i][:, LANE:2 * LANE]
        for p in range(RWKV_PAIRS):
            h_scr[p] = h[p]
        return carry

    lax.fori_loop(0, n_chunks // group, chunk_group, 0)

    o = o_s[...]
    mean = per_head(o, mean_bd, split=True)
    oc = o - mean
    var = per_head(oc * oc, mean_bd)
    on = oc * lax.rsqrt(var + RWKV_GN_EPS) * lng_ref[...] + lnb_ref[...]
    y_ref[...] = ((on + bonus) * gate).astype(y_ref.dtype)


def _dot2_exact_rhs_left(m_bf16, x):
    xh, xl = _split(x)
    return _dot(jnp.concatenate([m_bf16] * 2, axis=1), jnp.concatenate([xh, xl], axis=0))


def _rwkv_call(p_rkv, p_lora, v_first, prm):
    bsz, s, _ = p_rkv.shape
    tt = min(s, 256)
    has_vres = v_first is not None
    tok = lambda n: pl.BlockSpec((None, tt, n), lambda b, i: (b, i, 0))
    row = lambda n: _resident((1, n))
    in_specs = [tok(3 * RWKV_DIM), tok(LORA_PAD)]
    args = [p_rkv, p_lora]
    if has_vres:
        in_specs.append(tok(RWKV_DIM))
        args.append(v_first)
    in_specs += [row(3 * RWKV_DIM), row(LORA_PAD), _resident((3 * LORA_PAD, 4 * RWKV_DIM)),
                 row(RWKV_DIM), row(RWKV_DIM)]
    args += [prm["mu_rkv"], prm["mu_lora"], prm["w_lora_up"], prm["w0"], prm["a0"]]
    if has_vres:
        in_specs.append(row(RWKV_DIM))
        args.append(prm["v0"])
    in_specs += [row(RWKV_DIM)] * 5
    args += [prm["k_k"], prm["k_a"], prm["r_k"], prm["lnx_g"], prm["lnx_b"]]
    y_shape = jax.ShapeDtypeStruct((bsz, s, RWKV_DIM), BF16)
    if has_vres:
        out_shape, out_specs = y_shape, tok(RWKV_DIM)
    else:
        out_shape = (y_shape, jax.ShapeDtypeStruct((bsz, s, RWKV_DIM), F32))
        out_specs = (tok(RWKV_DIM), tok(RWKV_DIM))
    pair_tile = pltpu.VMEM((RWKV_PAIRS, tt, LANE), F32)
    out = pl.pallas_call(
        functools.partial(_rwkv_kernel, has_vres=has_vres, tt=tt),
        out_shape=out_shape,
        grid=(bsz, s // tt),
        in_specs=in_specs,
        out_specs=out_specs,
        scratch_shapes=[
            pltpu.VMEM((8, 3 * RWKV_DIM), F32),
            pltpu.VMEM((8, LORA_PAD), F32),
            pltpu.VMEM((RWKV_PAIRS, RWKV_HEAD, LANE), F32),
        ] + [pair_tile] * 8 + [pltpu.VMEM((tt, RWKV_DIM), F32)],
        compiler_params=_params(("parallel", "arbitrary")),
        name="rwkv7_mixer",
    )(*args)
    return out if has_vres else out


def _conv_kernel(p_ref, w_ref, b_ref, g_ref, be_ref, o_ref, ubuf, *, tt):
    @pl.when(pl.program_id(1) == 0)
    def _():
        ubuf[0:CONV_HALO, :] = jnp.zeros((CONV_HALO, CONV_DIM), F32)

    p = p_ref[...].astype(F32)
    ubuf[CONV_HALO:CONV_HALO + tt, :] = p[:, 0:CONV_DIM] * _sigmoid(p[:, CONV_DIM:2 * CONV_DIM])
    sub = 8
    base = CONV_HALO - sub
    ext = tt + sub
    acc = None
    for b in range(sub):
        part = None
        for a in range((CONV_WIDTH - 1 - b) // sub + 1):
            j = CONV_WIDTH - 1 - (sub * a + b)
            term = ubuf[base - sub * a:base - sub * a + ext, :] * w_ref[j:j + 1, :]
            part = term if part is None else part + term
        if b:
            part = pltpu.roll(part, b, 0)
        acc = part if acc is None else acc + part
    acc = acc[sub:sub + tt, :] + b_ref[...]
    ubuf[0:CONV_HALO, :] = ubuf[tt:tt + CONV_HALO, :]
    mean = jnp.mean(acc, axis=-1, keepdims=True)
    xc = acc - mean
    var = jnp.mean(xc * xc, axis=-1, keepdims=True)
    u = xc * lax.rsqrt(var + LN_EPS) * g_ref[...] + be_ref[...]
    o_ref[...] = (u * _sigmoid(u)).astype(o_ref.dtype)


def _conv_call(p_conv, conv_w, conv_b, ln_g, ln_b):
    bsz, s, _ = p_conv.shape
    tt = min(s, 512)
    wpad = jnp.zeros((CONV_HALO, CONV_DIM), F32).at[:CONV_WIDTH].set(conv_w)
    row = _resident((1, CONV_DIM))
    return pl.pallas_call(
        functools.partial(_conv_kernel, tt=tt),
        out_shape=jax.ShapeDtypeStruct((bsz, s, CONV_DIM), BF16),
        grid=(bsz, s // tt),
        in_specs=[pl.BlockSpec((None, tt, 2 * CONV_DIM), lambda b, i: (b, i, 0)),
                  _resident((CONV_HALO, CONV_DIM)), row, row, row],
        out_specs=pl.BlockSpec((None, tt, CONV_DIM), lambda b, i: (b, i, 0)),
        scratch_shapes=[pltpu.VMEM((tt + CONV_HALO, CONV_DIM), F32)],
        compiler_params=_params(("parallel", "arbitrary")),
        name="conformer_conv",
    )(p_conv, wpad, conv_b.reshape(1, -1), ln_g.reshape(1, -1), ln_b.reshape(1, -1))


def _mla_prep_kernel(p_ref, cos_ref, sin_ref, qg_ref, kvg_ref, wq_ref, wqs_ref, wkv_ref,
                     q_ref, k_ref, v_ref):
    p = p_ref[...].astype(F32)
    cos = cos_ref[...]
    sin = sin_ref[...]
    qc = (_rms(p[:, 0:Q_LORA], NORM_EPS) * qg_ref[...]).astype(BF16)
    kvc = (_rms(p[:, Q_LORA:Q_LORA + KV_LORA], NORM_EPS) * kvg_ref[...]).astype(BF16)
    q = _dot(qc, wq_ref[...])
    qs = _dot(qc, wqs_ref[...])
    kv = _dot(kvc, wkv_ref[...])
    o_kr = Q_LORA + KV_LORA
    k_rope = (p[:, o_kr:o_kr + LANE] * cos + p[:, o_kr + LANE:o_kr + 2 * LANE] * sin).astype(BF16)
    for h in range(MLA_HEADS):
        b0 = h * QK_PAD
        q_ref[:, b0:b0 + LANE] = q[:, b0:b0 + LANE].astype(BF16)
        q_ref[:, b0 + LANE:b0 + 2 * LANE] = (
            q[:, b0 + LANE:b0 + 2 * LANE] * cos + qs[:, h * LANE:(h + 1) * LANE] * sin).astype(BF16)
        k_ref[:, b0:b0 + LANE] = kv[:, b0:b0 + LANE].astype(BF16)
        k_ref[:, b0 + LANE:b0 + 2 * LANE] = k_rope
        v_ref[:, h * V_HEAD:(h + 1) * V_HEAD] = kv[:, b0 + LANE:b0 + 2 * LANE].astype(BF16)


def _mla_prep_call(p_mla, cos, sin, qg, kvg, wq, wqs, wkv):
    bsz, s, _ = p_mla.shape
    tm = min(s, 256)
    tok = lambda n: pl.BlockSpec((None, tm, n), lambda b, i: (b, i, 0))
    return pl.pallas_call(
        _mla_prep_kernel,
        out_shape=(jax.ShapeDtypeStruct((bsz, s, MLA_HEADS * QK_PAD), BF16),
                   jax.ShapeDtypeStruct((bsz, s, MLA_HEADS * QK_PAD), BF16),
                   jax.ShapeDtypeStruct((bsz, s, MLA_DIM), BF16)),
        grid=(bsz, s // tm),
        in_specs=[tok(MLA_IN_PAD), tok(LANE), tok(LANE),
                  _resident((1, Q_LORA)), _resident((1, KV_LORA)),
                  _resident(wq.shape), _resident(wqs.shape), _resident(wkv.shape)],
        out_specs=(tok(MLA_HEADS * QK_PAD), tok(MLA_HEADS * QK_PAD), tok(MLA_DIM)),
        compiler_params=_params(("parallel", "parallel")),
        name="mla_prep",
    )(p_mla, cos, sin, qg, kvg, wq, wqs, wkv)


def _attn_kernel(q_ref, k_ref, v_ref, o_ref, *, tq, nh):
    i = pl.program_id(2)
    heads = range(nh)
    q = [q_ref[:, h * QK_PAD:(h + 1) * QK_PAD] for h in heads]

    def step(j, carry, masked):
        rows = pl.ds(pl.multiple_of(j * tq, tq), tq)
        sc = [_dot_nt(q[h], k_ref[rows, h * QK_PAD:(h + 1) * QK_PAD]) for h in heads]
        if masked:
            visible = (_iota((tq, tq), 1) // CHUNK) <= (_iota((tq, tq), 0) // CHUNK)
            sc = [jnp.where(visible, s, MASK_VALUE) for s in sc]
        m_new = [jnp.maximum(carry[h][0], jnp.max(sc[h], axis=-1, keepdims=True)) for h in heads]
        alpha = [jnp.exp2(carry[h][0] - m_new[h]) for h in heads]
        pr = [jnp.exp2(sc[h] - m_new[h]) for h in heads]
        l_new = [alpha[h] * carry[h][1] + jnp.sum(pr[h], axis=-1, keepdims=True) for h in heads]
        pv = [_dot(pr[h].astype(BF16), v_ref[rows, h * V_HEAD:(h + 1) * V_HEAD]) for h in heads]
        return tuple((m_new[h], l_new[h], alpha[h] * carry[h][2] + pv[h]) for h in heads)

    init = (jnp.full((tq, 1), MASK_VALUE, F32), jnp.zeros((tq, 1), F32), jnp.zeros((tq, V_HEAD), F32))
    carry = lax.fori_loop(0, i, lambda j, c: step(j, c, False), (init,) * nh)
    carry = step(i, carry, True)
    for h in heads:
        o_ref[:, h * V_HEAD:(h + 1) * V_HEAD] = (carry[h][2] / carry[h][1]).astype(o_ref.dtype)


def _attn_call(q, k, v):
    bsz, s, _ = v.shape
    tq = min(s, 512)
    nh = ATTN_HEADS_PER_STEP
    return pl.pallas_call(
        functools.partial(_attn_kernel, tq=tq, nh=nh),
        out_shape=jax.ShapeDtypeStruct((bsz, s, MLA_DIM), BF16),
        grid=(bsz, MLA_HEADS // nh, s // tq),
        in_specs=[pl.BlockSpec((None, tq, nh * QK_PAD), lambda b, h, i: (b, i, h)),
                  pl.BlockSpec((None, s, nh * QK_PAD), lambda b, h, i: (b, 0, h)),
                  pl.BlockSpec((None, s, nh * V_HEAD), lambda b, h, i: (b, 0, h))],
        out_specs=pl.BlockSpec((None, tq, nh * V_HEAD), lambda b, h, i: (b, i, h)),
        compiler_params=_params(("parallel", "parallel", "arbitrary")),
        name="mla_attention",
    )(q, k, v)


def _outproj_kernel(ya_ref, yb_ref, yc_ref, x_ref, wo_ref, gt_ref, g_ref, sc_ref, sh_ref,
                    x1_ref, h2_ref):
    o1, o2 = RWKV_DIM, RWKV_DIM + CONV_DIM
    y = (_dot(ya_ref[...], wo_ref[0:o1, :]) + _dot(yb_ref[...], wo_ref[o1:o2, :])
         + _dot(yc_ref[...], wo_ref[o2:o2 + MLA_DIM, :]))
    x1 = x_ref[...] + gt_ref[...] * y
    x1_ref[...] = x1
    h2_ref[...] = (_rms(x1, NORM_EPS) * g_ref[...] * (1.0 + sc_ref[...]) + sh_ref[...]).astype(BF16)


def _outproj_call(ya, yb, yc, x, wo, gt, g, sc, sh):
    bsz, s, d = x.shape
    tm = min(s, 512)
    tok = lambda n: pl.BlockSpec((None, tm, n), lambda b, i: (b, i, 0))
    per_b = pl.BlockSpec((None, 1, d), lambda b, i: (b, 0, 0))
    return pl.pallas_call(
        _outproj_kernel,
        out_shape=(jax.ShapeDtypeStruct((bsz, s, d), F32), jax.ShapeDtypeStruct((bsz, s, d), BF16)),
        grid=(bsz, s // tm),
        in_specs=[tok(RWKV_DIM), tok(CONV_DIM), tok(MLA_DIM), tok(d), _resident(wo.shape),
                  per_b, _resident((1, d)), per_b, per_b],
        out_specs=(tok(d), tok(d)),
        compiler_params=_params(("parallel", "parallel")),
        name="outproj_norm2",
    )(ya, yb, yc, x, wo, gt, g, sc, sh)


def _mlp_kernel(h_ref, x_ref, w1_ref, w2_ref, gt_ref, fg_ref, o_ref, *, final):
    f = pl.program_id(2)

    @pl.when(f == 0)
    def _():
        o_ref[...] = jnp.zeros_like(o_ref)

    a = jnp.maximum(_dot(h_ref[...], w1_ref[...]), 0.0)
    o_ref[...] += _dot((a * a).astype(BF16), w2_ref[...])

    @pl.when(f == pl.num_programs(2) - 1)
    def _():
        xo = x_ref[...] + gt_ref[...] * o_ref[...]
        if final:
            xo = _rms(xo, NORM_EPS) * fg_ref[...]
        o_ref[...] = xo


def _mlp_call(h2, x1, w1, w2, gt, fg, final):
    bsz, s, d = x1.shape
    tm = min(s, 512)
    tf = 2048
    tok_map = lambda b, i, f: (b, i, 0)
    tok = lambda: pl.BlockSpec((None, tm, d), tok_map)
    return pl.pallas_call(
        functools.partial(_mlp_kernel, final=final),
        out_shape=jax.ShapeDtypeStruct((bsz, s, d), F32),
        grid=(bsz, s // tm, D_FF // tf),
        in_specs=[tok(), tok(),
                  pl.BlockSpec((d, tf), lambda b, i, f: (0, f)),
                  pl.BlockSpec((tf, d), lambda b, i, f: (f, 0)),
                  pl.BlockSpec((None, 1, d), lambda b, i, f: (b, 0, 0)),
                  pl.BlockSpec((1, d), lambda b, i, f: (0, 0))],
        out_specs=tok(),
        compiler_params=_params(("parallel", "parallel", "arbitrary"), MLP_VMEM_LIMIT),
        name="relu2_mlp",
    )(h2, x1, w1, w2, gt, fg)


def _pad_cols(w, n):
    return jnp.pad(w, ((0, 0), (0, n - w.shape[1])))


def _rotate_half_cols(w):
    half = w.shape[-1] // 2
    return jnp.concatenate([-w[..., half:], w[..., :half]], axis=-1)


def _layer_weights(l, w_in, w_in_vres, rwkv_mu, vres_mu, decay_up, iclr_up, gate_up, vres_up,
                   w_qb, w_kvb):
    d = D_MODEL
    wl = w_in[l]
    o_lora = 3 * RWKV_DIM
    o_conv = RWKV_IN
    o_mla = RWKV_IN + CONV_IN
    has_vres = l > 0
    lora_cols = [wl[:, o_lora:o_conv]]
    mu_cols = [rwkv_mu[l, o_lora:o_conv]]
    if has_vres:
        lora_cols.append(w_in_vres[l - 1])
        mu_cols.append(vres_mu[l - 1])
    w_lora = _pad_cols(jnp.concatenate(lora_cols, axis=1), LORA_PAD)
    mu_lora = jnp.pad(jnp.concatenate(mu_cols), (0, LORA_PAD - sum(m.shape[0] for m in mu_cols)))
    mla = wl[:, o_mla:N_IN]
    kr = mla[:, Q_LORA + KV_LORA:]
    zpad = jnp.zeros((d, LANE - QK_ROPE), F32)
    w_mla = jnp.concatenate([mla[:, :Q_LORA + KV_LORA], kr, zpad, _rotate_half_cols(kr), zpad], axis=1)

    up = jnp.zeros((LORA_PAD, 4 * RWKV_DIM), F32)
    o_a, o_g = DECAY_LORA, DECAY_LORA + ICLR_LORA
    o_v = o_g + GATE_LORA
    up = up.at[0:o_a, 0:RWKV_DIM].set(decay_up[l])
    up = up.at[o_a:o_g, RWKV_DIM:2 * RWKV_DIM].set(iclr_up[l])
    up = up.at[o_g:o_v, 2 * RWKV_DIM:3 * RWKV_DIM].set(gate_up[l])
    if has_vres:
        up = up.at[o_v:o_v + VRES_LORA, 3 * RWKV_DIM:].set(vres_up[l - 1])
    up_hi = up.astype(BF16)

    scale = (QK_NOPE + QK_ROPE) ** -0.5 * LOG2_E
    wq = (w_qb[l] * scale).reshape(Q_LORA, MLA_HEADS, QK_NOPE + QK_ROPE)
    nope, rope = wq[:, :, :QK_NOPE], wq[:, :, QK_NOPE:]
    z = jnp.zeros((Q_LORA, MLA_HEADS, LANE - QK_ROPE), F32)
    wq_main = jnp.concatenate([nope, rope, z], axis=2).reshape(Q_LORA, MLA_HEADS * QK_PAD)
    wq_rot = jnp.concatenate([_rotate_half_cols(rope), z], axis=2).reshape(Q_LORA, MLA_HEADS * LANE)
    return dict(
        w_rkv=wl[:, :o_lora].astype(BF16), w_lora=w_lora.astype(BF16),
        w_conv=wl[:, o_conv:o_mla].astype(BF16), w_mla=w_mla.astype(BF16),
        mu_rkv=rwkv_mu[l, :o_lora].reshape(1, -1), mu_lora=mu_lora.reshape(1, -1),
        w_lora_up=jnp.concatenate([up_hi, up_hi, (up - up_hi.astype(F32)).astype(BF16)], axis=0),
        wq=wq_main.astype(BF16), wq_rot=wq_rot.astype(BF16),
        wkv=w_kvb[l].astype(BF16))


def kernel(x, c, positions, ada_w, ada_b, norm1_g, norm2_g, final_g, w_in, w_in_vres, rwkv_mu,
           vres_mu, decay_w0, decay_up, iclr_a0, iclr_up, gate_up, vres_v0, vres_up, k_k, k_a, r_k,
           lnx_g, lnx_b, conv_w, conv_b, conv_ln_g, conv_ln_b, q_a_norm_g, w_qb, kv_a_norm_g, w_kvb,
           w_out, mlp_w1, mlp_w2):
    bsz, s, d = x.shape
    mod = _ada_call(c, ada_w, ada_b).reshape(DEPTH, bsz, 6, 1, d)
    cos, sin = _rope_call(positions)
    row = lambda a: a.reshape(1, -1)
    v_first = None
    for l in range(DEPTH):
        sh1, sc1, gt1, sh2, sc2, gt2 = (mod[l, :, j] for j in range(6))
        lw = _layer_weights(l, w_in, w_in_vres, rwkv_mu, vres_mu, decay_up, iclr_up, gate_up,
                            vres_up, w_qb, w_kvb)
        p_rkv, p_lora, p_conv, p_mla = _inproj_call(
            x, row(norm1_g[l]), sc1, sh1, lw["w_rkv"], lw["w_lora"], lw["w_conv"], lw["w_mla"])
        prm = dict(mu_rkv=lw["mu_rkv"], mu_lora=lw["mu_lora"], w_lora_up=lw["w_lora_up"],
                   w0=row(decay_w0[l]), a0=row(iclr_a0[l]), k_k=row(k_k[l]), k_a=row(k_a[l]),
                   r_k=row(r_k[l]), lnx_g=row(lnx_g[l]), lnx_b=row(lnx_b[l]))
        if l == 0:
            y_a, v_first = _rwkv_call(p_rkv, p_lora, None, prm)
        else:
            prm["v0"] = row(vres_v0[l - 1])
            y_a = _rwkv_call(p_rkv, p_lora, v_first, prm)
        y_b = _conv_call(p_conv, conv_w[l], conv_b[l], conv_ln_g[l], conv_ln_b[l])
        q, k, v = _mla_prep_call(p_mla, cos, sin, row(q_a_norm_g[l]), row(kv_a_norm_g[l]),
                                 lw["wq"], lw["wq_rot"], lw["wkv"])
        y_c = _attn_call(q, k, v)
        x1, h2 = _outproj_call(y_a, y_b, y_c, x, w_out[l].astype(BF16), gt1,
                               row(norm2_g[l]), sc2, sh2)
        x = _mlp_call(h2, x1, mlp_w1[l].astype(BF16), mlp_w2[l].astype(BF16), gt2,
                      row(final_g), final=(l == DEPTH - 1))
    return x
```

```python
import functools

import jax
import jax.numpy as jnp
from jax import lax
from jax.experimental import pallas as pl
from jax.experimental.pallas import tpu as pltpu

F32 = jnp.float32
BF16 = jnp.bfloat16

D_MODEL = 2048
DEPTH = 2
CHUNK = 64
NORM_EPS = 1e-6
LN_EPS = 1e-5
D_FF = 4 * D_MODEL

RWKV_HEAD = 64
RWKV_DIM = 512
RWKV_HEADS = 8
RWKV_PAIRS = RWKV_HEADS // 2
DECAY_LORA = 32
ICLR_LORA = 32
VRES_LORA = 32
GATE_LORA = 96
RWKV_GN_EPS = 64e-5
LORA_PAD = 256
RWKV_DECAY_SCALE = 0.6065306597126334
RWKV_INV_BASE = 8
RWKV_INV_PASSES = 3
RWKV_BASE_PASSES = 2
RWKV_CHUNKS_PER_ITER = 4

CONV_DIM = 512
CONV_WIDTH = 31
CONV_HALO = 32

MLA_DIM = 1024
V_HEAD = 128
MLA_HEADS = 8
QK_NOPE = 128
QK_ROPE = 64
Q_LORA = 512
KV_LORA = 256
ROPE_THETA = 10000.0
QK_PAD = 256
ATTN_HEADS_PER_STEP = 4
ATTN_TILE = 512
MLA_IN_PAD = 1024

RWKV_IN = 3 * RWKV_DIM + DECAY_LORA + ICLR_LORA + GATE_LORA
CONV_IN = 2 * CONV_DIM
N_IN = RWKV_IN + CONV_IN + Q_LORA + KV_LORA + QK_ROPE

LANE = 128
V7X_VMEM_LIMIT = 56 * 1024 * 1024
MLP_VMEM_LIMIT = 60 * 1024 * 1024
MASK_VALUE = -1e30
LOG2_E = 1.4426950408889634


def _params(semantics, vmem=V7X_VMEM_LIMIT):
    return pltpu.CompilerParams(dimension_semantics=semantics, vmem_limit_bytes=vmem)


def _resident(shape):
    nd = len(shape)
    return pl.BlockSpec(shape, lambda *_: (0,) * nd, pipeline_mode=pl.Buffered(1))


def _dot(a, b):
    return jnp.dot(a, b, preferred_element_type=F32)


def _dot_nt(a, b):
    return lax.dot_general(a, b, (((1,), (1,)), ((), ())), preferred_element_type=F32)


def _split(x):
    hi = x.astype(BF16)
    lo = (x - hi.astype(F32)).astype(BF16)
    return hi, lo


def _dot3(a, b):
    ah, al = _split(a)
    bh, bl = _split(b)
    return _dot(jnp.concatenate([ah, al, ah], axis=1), jnp.concatenate([bh, bh, bl], axis=0))


def _dot2(a, b):
    bh = b.astype(BF16)
    return _dot(jnp.concatenate(_split(a), axis=1), jnp.concatenate([bh, bh], axis=0))


def _dot2_nt(a, b):
    bh = b.astype(BF16)
    return _dot_nt(jnp.concatenate(_split(a), axis=1), jnp.concatenate([bh, bh], axis=1))


def _dot2_exact_rhs(a, b_bf16):
    ah, al = _split(a)
    return _dot(jnp.concatenate([ah, al], axis=1), jnp.concatenate([b_bf16] * 2, axis=0))


def _rms(x, eps):
    return x * lax.rsqrt(jnp.mean(x * x, axis=-1, keepdims=True) + eps)


def _sigmoid(x):
    return 0.5 * jnp.tanh(0.5 * x) + 0.5


def _iota(shape, dim):
    return lax.broadcasted_iota(jnp.int32, shape, dim)


def _ada_kernel(c_ref, w_ref, b_ref, o_ref):
    c = c_ref[...]
    o_ref[...] = _dot3(c * _sigmoid(c), w_ref[...]) + b_ref[...]


def _ada_call(c, ada_w, ada_b):
    depth, d, n = ada_w.shape
    bsz = c.shape[0]
    tn = 512
    return pl.pallas_call(
        _ada_kernel,
        out_shape=jax.ShapeDtypeStruct((depth, bsz, n), F32),
        grid=(depth, n // tn),
        in_specs=[
            pl.BlockSpec((bsz, d), lambda l, j: (0, 0)),
            pl.BlockSpec((None, d, tn), lambda l, j: (l, 0, j)),
            pl.BlockSpec((None, 1, tn), lambda l, j: (l, 0, j)),
        ],
        out_specs=pl.BlockSpec((None, bsz, tn), lambda l, j: (l, 0, j)),
        compiler_params=_params(("parallel", "parallel")),
        name="ada_mod",
    )(c, ada_w, ada_b.reshape(depth, 1, n))


def _rope_kernel(pos_ref, invf_ref, cos_ref, sin_ref):
    ang = pos_ref[...] * invf_ref[...]
    cos_ref[...] = jnp.cos(ang)
    sin_ref[...] = jnp.sin(ang)


def _rope_call(positions):
    bsz, s = positions.shape
    ts = min(s, 1024)
    inv_freq = ROPE_THETA ** (-jnp.arange(0, QK_ROPE, 2, dtype=F32) / QK_ROPE)
    invf = jnp.tile(inv_freq, LANE // (QK_ROPE // 2)).reshape(1, LANE)
    pos = positions.astype(F32).reshape(bsz, s, 1)
    shp = jax.ShapeDtypeStruct((bsz, s, LANE), F32)
    return pl.pallas_call(
        _rope_kernel,
        out_shape=(shp, shp),
        grid=(bsz, s // ts),
        in_specs=[
            pl.BlockSpec((None, ts, 1), lambda b, i: (b, i, 0)),
            pl.BlockSpec((1, LANE), lambda b, i: (0, 0)),
        ],
        out_specs=(
            pl.BlockSpec((None, ts, LANE), lambda b, i: (b, i, 0)),
            pl.BlockSpec((None, ts, LANE), lambda b, i: (b, i, 0)),
        ),
        compiler_params=_params(("parallel", "parallel")),
        name="rope_tables",
    )(pos, invf)


def _inproj_kernel(x_ref, g_ref, sc_ref, sh_ref, w_rkv, w_lora, w_conv, w_mla,
                   o_rkv, o_lora, o_conv, o_mla):
    x = x_ref[...]
    h = _rms(x, NORM_EPS) * g_ref[...] * (1.0 + sc_ref[...]) + sh_ref[...]
    hb = h.astype(BF16)
    o_rkv[...] = _dot(hb, w_rkv[...])
    o_lora[...] = _dot(hb, w_lora[...])
    o_conv[...] = _dot(hb, w_conv[...]).astype(o_conv.dtype)
    o_mla[...] = _dot(hb, w_mla[...]).astype(o_mla.dtype)


def _inproj_call(x, g, sc, sh, w_rkv, w_lora, w_conv, w_mla):
    bsz, s, d = x.shape
    tm = min(s, 256)
    widths = (w_rkv.shape[1], w_lora.shape[1], w_conv.shape[1], w_mla.shape[1])
    tok = lambda n: pl.BlockSpec((None, tm, n), lambda b, i: (b, i, 0))
    per_b = pl.BlockSpec((None, 1, d), lambda b, i: (b, 0, 0))
    return pl.pallas_call(
        _inproj_kernel,
        out_shape=tuple(jax.ShapeDtypeStruct((bsz, s, n), dt)
                        for n, dt in zip(widths, (F32, F32, BF16, BF16))),
        grid=(bsz, s // tm),
        in_specs=[tok(d), _resident((1, d)), per_b, per_b,
                  _resident(w_rkv.shape), _resident(w_lora.shape),
                  _resident(w_conv.shape), _resident(w_mla.shape)],
        out_specs=tuple(tok(n) for n in widths),
        compiler_params=_params(("parallel", "parallel")),
        name="norm_inproj",
    )(x, g, sc, sh, w_rkv, w_lora, w_conv, w_mla)


def _block_diag(y):
    lane = _iota(y.shape, 1)
    zero = jnp.zeros_like(y)
    return jnp.concatenate([jnp.where(lane < RWKV_HEAD, y, zero),
                            jnp.where(lane >= RWKV_HEAD, y, zero)], axis=0)


def _packed_mm(x, ys, passes=3):
    if passes == 1:
        lhs = x.astype(BF16)
        cols = [_block_diag(y.astype(BF16)) for y in ys]
    elif passes == 2:
        lhs = jnp.concatenate(_split(x), axis=1)
        cols = [jnp.concatenate([_block_diag(y.astype(BF16))] * 2, axis=0) for y in ys]
    else:
        xh, xl = _split(x)
        lhs = jnp.concatenate([xh, xl, xh], axis=1)
        cols = []
        for y in ys:
            yh, yl = _split(y)
            cols.append(jnp.concatenate([_block_diag(yh), _block_diag(yh), _block_diag(yl)], axis=0))
    out = _dot(lhs, cols[0] if len(cols) == 1 else jnp.concatenate(cols, axis=1))
    return [out[:, i * LANE:(i + 1) * LANE] for i in range(len(ys))]


def _head_sum_matrix(scale):
    r = _iota((LANE, LANE), 0) // RWKV_HEAD
    c = _iota((LANE, LANE), 1) // RWKV_HEAD
    return jnp.where(r == c, scale, 0.0).astype(BF16)


def _rwkv_kernel(*refs, has_vres, tt):
    if has_vres:
        (prkv_ref, plora_ref, vfirst_ref, mu_rkv, mu_lora, wc_ref, w0_ref, a0_ref, v0_ref,
         kk_ref, ka_ref, rk_ref, lng_ref, lnb_ref, y_ref,
         prev_rkv, prev_lora, h_scr, at_s, rt_s, bt_s, kt_s, bp_s, kp_s, v_s, pf_s, o_s) = refs
    else:
        (prkv_ref, plora_ref, mu_rkv, mu_lora, wc_ref, w0_ref, a0_ref,
         kk_ref, ka_ref, rk_ref, lng_ref, lnb_ref, y_ref, vout_ref,
         prev_rkv, prev_lora, h_scr, at_s, rt_s, bt_s, kt_s, bp_s, kp_s, v_s, pf_s, o_s) = refs
    n_chunks = tt // CHUNK

    @pl.when(pl.program_id(1) == 0)
    def _():
        prev_rkv[...] = jnp.zeros_like(prev_rkv)
        prev_lora[...] = jnp.zeros_like(prev_lora)
        h_scr[...] = jnp.zeros_like(h_scr)

    def shift_mix(p_ref, prev_ref, mu_ref):
        p = p_ref[...]
        row = _iota(p.shape, 0)
        prev = jnp.where(row == 0, prev_ref[0:1, :], pltpu.roll(p, 1, 0))
        prev_ref[0:1, :] = p[tt - 1:tt, :]
        return p + (prev - p) * mu_ref[...]

    xs = shift_mix(prkv_ref, prev_rkv, mu_rkv)
    xl = shift_mix(plora_ref, prev_lora, mu_lora)
    r = xs[:, 0:RWKV_DIM]
    k = xs[:, RWKV_DIM:2 * RWKV_DIM]
    v = xs[:, 2 * RWKV_DIM:3 * RWKV_DIM]

    lane = _iota(xl.shape, 1)
    o_w, o_a, o_g = DECAY_LORA, DECAY_LORA + ICLR_LORA, DECAY_LORA + ICLR_LORA + GATE_LORA
    act = jnp.where(lane < o_w, jnp.tanh(xl),
                    jnp.where((lane >= o_a) & (lane < o_g), _sigmoid(xl), xl))
    act_h, act_l = _split(act)
    lora = _dot(jnp.concatenate([act_h, act_l, act_h], axis=1), wc_ref[...])
    z = w0_ref[...] + lora[:, 0:RWKV_DIM]
    lw = -RWKV_DECAY_SCALE * _sigmoid(z)
    a_ic = _sigmoid(a0_ref[...] + lora[:, RWKV_DIM:2 * RWKV_DIM])
    gate = lora[:, 2 * RWKV_DIM:3 * RWKV_DIM]
    if has_vres:
        v = v + (vfirst_ref[...] - v) * _sigmoid(v0_ref[...] + lora[:, 3 * RWKV_DIM:4 * RWKV_DIM])
    else:
        vout_ref[...] = v

    ones_bd = _head_sum_matrix(1.0)
    mean_bd = _head_sum_matrix(1.0 / RWKV_HEAD)

    def per_head(x, mat, split=False):
        mm = _dot2_exact_rhs if split else (lambda a, b: _dot(a.astype(BF16), b))
        return jnp.concatenate(
            [mm(x[:, p * LANE:(p + 1) * LANE], mat) for p in range(RWKV_PAIRS)], axis=1)

    kk = k * kk_ref[...]
    kk = kk * lax.rsqrt(jnp.maximum(per_head(kk * kk, ones_bd), 1e-24))
    k2 = k * (1.0 + (a_ic - 1.0) * ka_ref[...])
    a_vec = -kk
    b_vec = kk * a_ic
    bonus = per_head(r * k2 * rk_ref[...], ones_bd) * v

    tr = _iota((tt, tt), 0)
    tc = _iota((tt, tt), 1)
    tri = jnp.where((tr // CHUNK == tc // CHUNK) & (tc <= tr), 1.0, 0.0).astype(BF16)
    cum = _dot2_exact_rhs_left(tri, lw)
    cum3 = cum.reshape(n_chunks, CHUNK, RWKV_DIM)
    cum_end = jnp.broadcast_to(cum3[:, CHUNK - 1:CHUNK, :], cum3.shape).reshape(tt, RWKV_DIM)
    pf = jnp.exp(cum)
    pinv = jnp.exp(-cum)
    pprev = jnp.exp(cum - lw)
    pend = jnp.exp(cum_end - cum)

    def put(dst, val):
        for p in range(RWKV_PAIRS):
            dst[p] = val[:, p * LANE:(p + 1) * LANE]

    put(at_s, a_vec * pprev)
    put(rt_s, r * pf)
    put(bt_s, b_vec * pinv)
    put(kt_s, k2 * pinv)
    put(bp_s, b_vec * pend)
    put(kp_s, k2 * pend)
    put(v_s, v)
    put(pf_s, pf)

    t_loc = _iota((CHUNK, LANE), 0)
    s_loc = _iota((CHUNK, LANE), 1) % RWKV_HEAD
    strict = s_loc < t_loc
    incl = s_loc <= t_loc
    eye_p = jnp.where(s_loc == t_loc, 1.0, 0.0)
    same_block = lambda n: (t_loc // n) == (s_loc // n)
    in_base = same_block(RWKV_INV_BASE)
    merge_masks = []
    n = RWKV_INV_BASE
    while n < CHUNK:
        merge_masks.append(same_block(2 * n) & jnp.logical_not(same_block(n)))
        n *= 2

    group = min(n_chunks, RWKV_CHUNKS_PER_ITER)

    def chunk_group(cg, carry):
        streams = []
        for ci in range(group):
            c = cg * group + ci
            rows = pl.ds(pl.multiple_of(c * CHUNK, CHUNK), CHUNK)
            last = pl.ds(c * CHUNK + CHUNK - 1, 1)
            streams += [(p, rows, last) for p in range(RWKV_PAIRS)]
        ns = range(len(streams))
        load = lambda ref: [ref[p, rows, :] for (p, rows, _) in streams]
        at, rt, bt, kt, bp, kp, vv = (load(r) for r in (at_s, rt_s, bt_s, kt_s, bp_s, kp_s, v_s))
        g = [_dot2_nt(jnp.concatenate([at[i], rt[i]], axis=0),
                      jnp.concatenate([_block_diag(bt[i]), _block_diag(kt[i])], axis=0)) for i in ns]
        aab = [jnp.where(strict, g[i][0:CHUNK, 0:LANE], 0.0) for i in ns]
        aak = [jnp.where(strict, g[i][0:CHUNK, LANE:2 * LANE], 0.0) for i in ns]
        arb = [jnp.where(incl, g[i][CHUNK:2 * CHUNK, 0:LANE], 0.0) for i in ns]
        ark = [jnp.where(incl, g[i][CHUNK:2 * CHUNK, LANE:2 * LANE], 0.0) for i in ns]
        n_diag = [jnp.where(in_base, aab[i], 0.0) for i in ns]
        inv, base = RWKV_INV_PASSES, RWKV_BASE_PASSES
        dpow = [_packed_mm(n_diag[i], [n_diag[i]], passes=base)[0] for i in ns]
        tinv = [eye_p + n_diag[i] for i in ns]
        akv = [_packed_mm(aak[i], [vv[i]], passes=1)[0] for i in ns]
        arkv = [_packed_mm(ark[i], [vv[i]], passes=1)[0] for i in ns]
        res = [_packed_mm(dpow[i], [dpow[i], tinv[i]], passes=base) for i in ns]
        tinv = [tinv[i] + res[i][1] for i in ns]
        tinv = [tinv[i] + _packed_mm(res[i][0], [tinv[i]], passes=base)[0] for i in ns]
        for level_mask in merge_masks:
            ct = [_packed_mm(jnp.where(level_mask, aab[i], 0.0), [tinv[i]], passes=inv)[0] for i in ns]
            tinv = [tinv[i] + _packed_mm(tinv[i], [ct[i]], passes=inv)[0] for i in ns]
        wu = [_packed_mm(tinv[i], [at[i], akv[i]], passes=inv) for i in ns]
        ab = [_packed_mm(arb[i], wu[i], passes=1) for i in ns]
        q_m = [rt[i] + ab[i][0] for i in ns]
        y0 = [ab[i][1] + arkv[i] for i in ns]
        upd = [_dot2(jnp.concatenate([bp[i], kp[i]], axis=0).T,
                     jnp.concatenate([jnp.concatenate(wu[i], axis=1),
                                      jnp.concatenate([jnp.zeros_like(vv[i]), vv[i]], axis=1)], axis=0))
               for i in ns]
        first_head = _iota((CHUNK, 2 * LANE), 1) % LANE < RWKV_HEAD
        upd = [jnp.where(first_head, upd[i][0:RWKV_HEAD], upd[i][RWKV_HEAD:2 * RWKV_HEAD]) for i in ns]
        h = [h_scr[p] for p in range(RWKV_PAIRS)]
        for i in ns:
            p, rows, last = streams[i]
            m_p = upd[i][:, 0:LANE] + jnp.where(s_loc == t_loc, pf_s[p, last, :], 0.0)
            (qh_mh,) = _packed_mm(jnp.concatenate([q_m[i], m_p], axis=0), [h[p]], passes=3)
            o_s[rows, p * LANE:(p + 1) * LANE] = qh_mh[0:CHUNK] + y0[i]
            h[p] = qh_mh[CHUNK:2 * CHUNK] + upd[i][:, LANE:2 * LANE]
        for p in range(RWKV_PAIRS):
            h_scr[p] = h[p]
        return carry

    lax.fori_loop(0, n_chunks // group, chunk_group, 0)

    o = o_s[...]
    mean = per_head(o, mean_bd, split=True)
    oc = o - mean
    var = per_head(oc * oc, mean_bd)
    on = oc * lax.rsqrt(var + RWKV_GN_EPS) * lng_ref[...] + lnb_ref[...]
    y_ref[...] = ((on + bonus) * gate).astype(y_ref.dtype)


def _dot2_exact_rhs_left(m_bf16, x):
    xh, xl = _split(x)
    return _dot(jnp.concatenate([m_bf16] * 2, axis=1), jnp.concatenate([xh, xl], axis=0))


def _rwkv_call(p_rkv, p_lora, v_first, prm):
    bsz, s, _ = p_rkv.shape
    tt = min(s, 256)
    has_vres = v_first is not None
    tok = lambda n: pl.BlockSpec((None, tt, n), lambda b, i: (b, i, 0))
    row = lambda n: _resident((1, n))
    in_specs = [tok(3 * RWKV_DIM), tok(LORA_PAD)]
    args = [p_rkv, p_lora]
    if has_vres:
        in_specs.append(tok(RWKV_DIM))
        args.append(v_first)
    in_specs += [row(3 * RWKV_DIM), row(LORA_PAD), _resident((3 * LORA_PAD, 4 * RWKV_DIM)),
                 row(RWKV_DIM), row(RWKV_DIM)]
    args += [prm["mu_rkv"], prm["mu_lora"], prm["w_lora_up"], prm["w0"], prm["a0"]]
    if has_vres:
        in_specs.append(row(RWKV_DIM))
        args.append(prm["v0"])
    in_specs += [row(RWKV_DIM)] * 5
    args += [prm["k_k"], prm["k_a"], prm["r_k"], prm["lnx_g"], prm["lnx_b"]]
    y_shape = jax.ShapeDtypeStruct((bsz, s, RWKV_DIM), BF16)
    if has_vres:
        out_shape, out_specs = y_shape, tok(RWKV_DIM)
    else:
        out_shape = (y_shape, jax.ShapeDtypeStruct((bsz, s, RWKV_DIM), F32))
        out_specs = (tok(RWKV_DIM), tok(RWKV_DIM))
    pair_tile = pltpu.VMEM((RWKV_PAIRS, tt, LANE), F32)
    out = pl.pallas_call(
        functools.partial(_rwkv_kernel, has_vres=has_vres, tt=tt),
        out_shape=out_shape,
        grid=(bsz, s // tt),
        in_specs=in_specs,
        out_specs=out_specs,
        scratch_shapes=[
            pltpu.VMEM((8, 3 * RWKV_DIM), F32),
            pltpu.VMEM((8, LORA_PAD), F32),
            pltpu.VMEM((RWKV_PAIRS, RWKV_HEAD, LANE), F32),
        ] + [pair_tile] * 8 + [pltpu.VMEM((tt, RWKV_DIM), F32)],
        compiler_params=_params(("parallel", "arbitrary")),
        name="rwkv7_mixer",
    )(*args)
    return out if has_vres else out


def _conv_kernel(p_ref, w_ref, b_ref, g_ref, be_ref, o_ref, ubuf, *, tt):
    @pl.when(pl.program_id(1) == 0)
    def _():
        ubuf[0:CONV_HALO, :] = jnp.zeros((CONV_HALO, CONV_DIM), F32)

    p = p_ref[...].astype(F32)
    ubuf[CONV_HALO:CONV_HALO + tt, :] = p[:, 0:CONV_DIM] * _sigmoid(p[:, CONV_DIM:2 * CONV_DIM])
    sub = 8
    base = CONV_HALO - sub
    ext = tt + sub
    acc = None
    for b in range(sub):
        part = None
        for a in range((CONV_WIDTH - 1 - b) // sub + 1):
            j = CONV_WIDTH - 1 - (sub * a + b)
            term = ubuf[base - sub * a:base - sub * a + ext, :] * w_ref[j:j + 1, :]
            part = term if part is None else part + term
        if b:
            part = pltpu.roll(part, b, 0)
        acc = part if acc is None else acc + part
    acc = acc[sub:sub + tt, :] + b_ref[...]
    ubuf[0:CONV_HALO, :] = ubuf[tt:tt + CONV_HALO, :]
    mean = jnp.mean(acc, axis=-1, keepdims=True)
    xc = acc - mean
    var = jnp.mean(xc * xc, axis=-1, keepdims=True)
    u = xc * lax.rsqrt(var + LN_EPS) * g_ref[...] + be_ref[...]
    o_ref[...] = (u * _sigmoid(u)).astype(o_ref.dtype)


def _conv_call(p_conv, conv_w, conv_b, ln_g, ln_b):
    bsz, s, _ = p_conv.shape
    tt = min(s, 512)
    wpad = jnp.zeros((CONV_HALO, CONV_DIM), F32).at[:CONV_WIDTH].set(conv_w)
    row = _resident((1, CONV_DIM))
    return pl.pallas_call(
        functools.partial(_conv_kernel, tt=tt),
        out_shape=jax.ShapeDtypeStruct((bsz, s, CONV_DIM), BF16),
        grid=(bsz, s // tt),
        in_specs=[pl.BlockSpec((None, tt, 2 * CONV_DIM), lambda b, i: (b, i, 0)),
                  _resident((CONV_HALO, CONV_DIM)), row, row, row],
        out_specs=pl.BlockSpec((None, tt, CONV_DIM), lambda b, i: (b, i, 0)),
        scratch_shapes=[pltpu.VMEM((tt + CONV_HALO, CONV_DIM), F32)],
        compiler_params=_params(("parallel", "arbitrary")),
        name="conformer_conv",
    )(p_conv, wpad, conv_b.reshape(1, -1), ln_g.reshape(1, -1), ln_b.reshape(1, -1))


def _mla_prep_kernel(p_ref, cos_ref, sin_ref, qg_ref, kvg_ref, wq_ref, wqs_ref, wkv_ref,
                     q_ref, k_ref, v_ref):
    p = p_ref[...].astype(F32)
    cos = cos_ref[...]
    sin = sin_ref[...]
    qc = (_rms(p[:, 0:Q_LORA], NORM_EPS) * qg_ref[...]).astype(BF16)
    kvc = (_rms(p[:, Q_LORA:Q_LORA + KV_LORA], NORM_EPS) * kvg_ref[...]).astype(BF16)
    q = _dot(qc, wq_ref[...])
    qs = _dot(qc, wqs_ref[...])
    kv = _dot(kvc, wkv_ref[...])
    o_kr = Q_LORA + KV_LORA
    k_rope = (p[:, o_kr:o_kr + LANE] * cos + p[:, o_kr + LANE:o_kr + 2 * LANE] * sin).astype(BF16)
    for h in range(MLA_HEADS):
        b0 = h * QK_PAD
        q_ref[:, b0:b0 + LANE] = q[:, b0:b0 + LANE].astype(BF16)
        q_ref[:, b0 + LANE:b0 + 2 * LANE] = (
            q[:, b0 + LANE:b0 + 2 * LANE] * cos + qs[:, h * LANE:(h + 1) * LANE] * sin).astype(BF16)
        k_ref[:, b0:b0 + LANE] = kv[:, b0:b0 + LANE].astype(BF16)
        k_ref[:, b0 + LANE:b0 + 2 * LANE] = k_rope
        v_ref[h * V_HEAD:(h + 1) * V_HEAD, :] = kv[:, b0 + LANE:b0 + 2 * LANE].T.astype(BF16)


def _mla_prep_call(p_mla, cos, sin, qg, kvg, wq, wqs, wkv):
    bsz, s, _ = p_mla.shape
    tm = min(s, ATTN_TILE)
    tok = lambda n: pl.BlockSpec((None, tm, n), lambda b, i: (b, i, 0))
    return pl.pallas_call(
        _mla_prep_kernel,
        out_shape=(jax.ShapeDtypeStruct((bsz, s, MLA_HEADS * QK_PAD), BF16),
                   jax.ShapeDtypeStruct((bsz, s, MLA_HEADS * QK_PAD), BF16),
                   jax.ShapeDtypeStruct((bsz, s // tm, MLA_DIM, tm), BF16)),
        grid=(bsz, s // tm),
        in_specs=[tok(MLA_IN_PAD), tok(LANE), tok(LANE),
                  _resident((1, Q_LORA)), _resident((1, KV_LORA)),
                  _resident(wq.shape), _resident(wqs.shape), _resident(wkv.shape)],
        out_specs=(tok(MLA_HEADS * QK_PAD), tok(MLA_HEADS * QK_PAD),
                   pl.BlockSpec((None, None, MLA_DIM, tm), lambda b, i: (b, i, 0, 0))),
        compiler_params=_params(("parallel", "parallel")),
        name="mla_prep",
    )(p_mla, cos, sin, qg, kvg, wq, wqs, wkv)


def _attn_kernel(q_ref, k_ref, vt_ref, o_ref, *, tq, nh):
    i = pl.program_id(2)
    heads = range(nh)
    q = [q_ref[:, h * QK_PAD:(h + 1) * QK_PAD] for h in heads]

    def step(j, carry, masked):
        keys = pl.ds(pl.multiple_of(j * tq, tq), tq)
        st = [_dot_nt(k_ref[keys, h * QK_PAD:(h + 1) * QK_PAD], q[h]) for h in heads]
        if masked:
            visible = (_iota((tq, tq), 0) // CHUNK) <= (_iota((tq, tq), 1) // CHUNK)
            st = [jnp.where(visible, s, MASK_VALUE) for s in st]
        m_new = [jnp.maximum(carry[h][0], jnp.max(st[h], axis=0, keepdims=True)) for h in heads]
        alpha = [jnp.exp2(carry[h][0] - m_new[h]) for h in heads]
        pt = [jnp.exp2(st[h] - m_new[h]) for h in heads]
        l_new = [alpha[h] * carry[h][1] + jnp.sum(pt[h], axis=0, keepdims=True) for h in heads]
        pv = [_dot(vt_ref[j, h * V_HEAD:(h + 1) * V_HEAD, :], pt[h].astype(BF16)) for h in heads]
        return tuple((m_new[h], l_new[h], alpha[h] * carry[h][2] + pv[h]) for h in heads)

    init = (jnp.full((1, tq), MASK_VALUE, F32), jnp.zeros((1, tq), F32), jnp.zeros((V_HEAD, tq), F32))
    carry = lax.fori_loop(0, i, lambda j, c: step(j, c, False), (init,) * nh)
    carry = step(i, carry, True)
    for h in heads:
        o_ref[:, h * V_HEAD:(h + 1) * V_HEAD] = (carry[h][2] / carry[h][1]).T.astype(o_ref.dtype)


def _attn_call(q, k, vt):
    bsz, s, _ = q.shape
    tq = min(s, ATTN_TILE)
    nh = ATTN_HEADS_PER_STEP
    return pl.pallas_call(
        functools.partial(_attn_kernel, tq=tq, nh=nh),
        out_shape=jax.ShapeDtypeStruct((bsz, s, MLA_DIM), BF16),
        grid=(bsz, MLA_HEADS // nh, s // tq),
        in_specs=[pl.BlockSpec((None, tq, nh * QK_PAD), lambda b, h, i: (b, i, h)),
                  pl.BlockSpec((None, s, nh * QK_PAD), lambda b, h, i: (b, 0, h)),
                  pl.BlockSpec((None, s // tq, nh * V_HEAD, tq), lambda b, h, i: (b, 0, h, 0))],
        out_specs=pl.BlockSpec((None, tq, nh * V_HEAD), lambda b, h, i: (b, i, h)),
        compiler_params=_params(("parallel", "parallel", "arbitrary")),
        name="mla_attention",
    )(q, k, vt)


def _outproj_kernel(ya_ref, yb_ref, yc_ref, x_ref, wo_ref, gt_ref, g_ref, sc_ref, sh_ref,
                    x1_ref, h2_ref):
    o1, o2 = RWKV_DIM, RWKV_DIM + CONV_DIM
    y = (_dot(ya_ref[...], wo_ref[0:o1, :]) + _dot(yb_ref[...], wo_ref[o1:o2, :])
         + _dot(yc_ref[...], wo_ref[o2:o2 + MLA_DIM, :]))
    x1 = x_ref[...] + gt_ref[...] * y
    x1_ref[...] = x1
    h2_ref[...] = (_rms(x1, NORM_EPS) * g_ref[...] * (1.0 + sc_ref[...]) + sh_ref[...]).astype(BF16)


def _outproj_call(ya, yb, yc, x, wo, gt, g, sc, sh):
    bsz, s, d = x.shape
    tm = min(s, 512)
    tok = lambda n: pl.BlockSpec((None, tm, n), lambda b, i: (b, i, 0))
    per_b = pl.BlockSpec((None, 1, d), lambda b, i: (b, 0, 0))
    return pl.pallas_call(
        _outproj_kernel,
        out_shape=(jax.ShapeDtypeStruct((bsz, s, d), F32), jax.ShapeDtypeStruct((bsz, s, d), BF16)),
        grid=(bsz, s // tm),
        in_specs=[tok(RWKV_DIM), tok(CONV_DIM), tok(MLA_DIM), tok(d), _resident(wo.shape),
                  per_b, _resident((1, d)), per_b, per_b],
        out_specs=(tok(d), tok(d)),
        compiler_params=_params(("parallel", "parallel")),
        name="outproj_norm2",
    )(ya, yb, yc, x, wo, gt, g, sc, sh)


def _mlp_kernel(h_ref, x_ref, w1_ref, w2_ref, gt_ref, fg_ref, o_ref, *, final):
    f = pl.program_id(2)

    @pl.when(f == 0)
    def _():
        o_ref[...] = jnp.zeros_like(o_ref)

    a = jnp.maximum(_dot(h_ref[...], w1_ref[...]), 0.0)
    o_ref[...] += _dot((a * a).astype(BF16), w2_ref[...])

    @pl.when(f == pl.num_programs(2) - 1)
    def _():
        xo = x_ref[...] + gt_ref[...] * o_ref[...]
        if final:
            xo = _rms(xo, NORM_EPS) * fg_ref[...]
        o_ref[...] = xo


def _mlp_call(h2, x1, w1, w2, gt, fg, final):
    bsz, s, d = x1.shape
    tm = min(s, 512)
    tf = 2048
    tok_map = lambda b, i, f: (b, i, 0)
    tok = lambda: pl.BlockSpec((None, tm, d), tok_map)
    return pl.pallas_call(
        functools.partial(_mlp_kernel, final=final),
        out_shape=jax.ShapeDtypeStruct((bsz, s, d), F32),
        grid=(bsz, s // tm, D_FF // tf),
        in_specs=[tok(), tok(),
                  pl.BlockSpec((d, tf), lambda b, i, f: (0, f)),
                  pl.BlockSpec((tf, d), lambda b, i, f: (f, 0)),
                  pl.BlockSpec((None, 1, d), lambda b, i, f: (b, 0, 0)),
                  pl.BlockSpec((1, d), lambda b, i, f: (0, 0))],
        out_specs=tok(),
        compiler_params=_params(("parallel", "parallel", "arbitrary"), MLP_VMEM_LIMIT),
        name="relu2_mlp",
    )(h2, x1, w1, w2, gt, fg)


def _pad_cols(w, n):
    return jnp.pad(w, ((0, 0), (0, n - w.shape[1])))


def _rotate_half_cols(w):
    half = w.shape[-1] // 2
    return jnp.concatenate([-w[..., half:], w[..., :half]], axis=-1)


def _layer_weights(l, w_in, w_in_vres, rwkv_mu, vres_mu, decay_up, iclr_up, gate_up, vres_up,
                   w_qb, w_kvb):
    d = D_MODEL
    wl = w_in[l]
    o_lora = 3 * RWKV_DIM
    o_conv = RWKV_IN
    o_mla = RWKV_IN + CONV_IN
    has_vres = l > 0
    lora_cols = [wl[:, o_lora:o_conv]]
    mu_cols = [rwkv_mu[l, o_lora:o_conv]]
    if has_vres:
        lora_cols.append(w_in_vres[l - 1])
        mu_cols.append(vres_mu[l - 1])
    w_lora = _pad_cols(jnp.concatenate(lora_cols, axis=1), LORA_PAD)
    mu_lora = jnp.pad(jnp.concatenate(mu_cols), (0, LORA_PAD - sum(m.shape[0] for m in mu_cols)))
    mla = wl[:, o_mla:N_IN]
    kr = mla[:, Q_LORA + KV_LORA:]
    zpad = jnp.zeros((d, LANE - QK_ROPE), F32)
    w_mla = jnp.concatenate([mla[:, :Q_LORA + KV_LORA], kr, zpad, _rotate_half_cols(kr), zpad], axis=1)

    up = jnp.zeros((LORA_PAD, 4 * RWKV_DIM), F32)
    o_a, o_g = DECAY_LORA, DECAY_LORA + ICLR_LORA
    o_v = o_g + GATE_LORA
    up = up.at[0:o_a, 0:RWKV_DIM].set(decay_up[l])
    up = up.at[o_a:o_g, RWKV_DIM:2 * RWKV_DIM].set(iclr_up[l])
    up = up.at[o_g:o_v, 2 * RWKV_DIM:3 * RWKV_DIM].set(gate_up[l])
    if has_vres:
        up = up.at[o_v:o_v + VRES_LORA, 3 * RWKV_DIM:].set(vres_up[l - 1])
    up_hi = up.astype(BF16)

    scale = (QK_NOPE + QK_ROPE) ** -0.5 * LOG2_E
    wq = (w_qb[l] * scale).reshape(Q_LORA, MLA_HEADS, QK_NOPE + QK_ROPE)
    nope, rope = wq[:, :, :QK_NOPE], wq[:, :, QK_NOPE:]
    z = jnp.zeros((Q_LORA, MLA_HEADS, LANE - QK_ROPE), F32)
    wq_main = jnp.concatenate([nope, rope, z], axis=2).reshape(Q_LORA, MLA_HEADS * QK_PAD)
    wq_rot = jnp.concatenate([_rotate_half_cols(rope), z], axis=2).reshape(Q_LORA, MLA_HEADS * LANE)
    return dict(
        w_rkv=wl[:, :o_lora].astype(BF16), w_lora=w_lora.astype(BF16),
        w_conv=wl[:, o_conv:o_mla].astype(BF16), w_mla=w_mla.astype(BF16),
        mu_rkv=rwkv_mu[l, :o_lora].reshape(1, -1), mu_lora=mu_lora.reshape(1, -1),
        w_lora_up=jnp.concatenate([up_hi, up_hi, (up - up_hi.astype(F32)).astype(BF16)], axis=0),
        wq=wq_main.astype(BF16), wq_rot=wq_rot.astype(BF16),
        wkv=w_kvb[l].astype(BF16))


def kernel(x, c, positions, ada_w, ada_b, norm1_g, norm2_g, final_g, w_in, w_in_vres, rwkv_mu,
           vres_mu, decay_w0, decay_up, iclr_a0, iclr_up, gate_up, vres_v0, vres_up, k_k, k_a, r_k,
           lnx_g, lnx_b, conv_w, conv_b, conv_ln_g, conv_ln_b, q_a_norm_g, w_qb, kv_a_norm_g, w_kvb,
           w_out, mlp_w1, mlp_w2):
    bsz, s, d = x.shape
    mod = _ada_call(c, ada_w, ada_b).reshape(DEPTH, bsz, 6, 1, d)
    cos, sin = _rope_call(positions)
    row = lambda a: a.reshape(1, -1)
    v_first = None
    for l in range(DEPTH):
        sh1, sc1, gt1, sh2, sc2, gt2 = (mod[l, :, j] for j in range(6))
        lw = _layer_weights(l, w_in, w_in_vres, rwkv_mu, vres_mu, decay_up, iclr_up, gate_up,
                            vres_up, w_qb, w_kvb)
        p_rkv, p_lora, p_conv, p_mla = _inproj_call(
            x, row(norm1_g[l]), sc1, sh1, lw["w_rkv"], lw["w_lora"], lw["w_conv"], lw["w_mla"])
        prm = dict(mu_rkv=lw["mu_rkv"], mu_lora=lw["mu_lora"], w_lora_up=lw["w_lora_up"],
                   w0=row(decay_w0[l]), a0=row(iclr_a0[l]), k_k=row(k_k[l]), k_a=row(k_a[l]),
                   r_k=row(r_k[l]), lnx_g=row(lnx_g[l]), lnx_b=row(lnx_b[l]))
        if l == 0:
            y_a, v_first = _rwkv_call(p_rkv, p_lora, None, prm)
        else:
            prm["v0"] = row(vres_v0[l - 1])
            y_a = _rwkv_call(p_rkv, p_lora, v_first, prm)
        y_b = _conv_call(p_conv, conv_w[l], conv_b[l], conv_ln_g[l], conv_ln_b[l])
        q, k, v = _mla_prep_call(p_mla, cos, sin, row(q_a_norm_g[l]), row(kv_a_norm_g[l]),
                                 lw["wq"], lw["wq_rot"], lw["wkv"])
        y_c = _attn_call(q, k, v)
        x1, h2 = _outproj_call(y_a, y_b, y_c, x, w_out[l].astype(BF16), gt1,
                               row(norm2_g[l]), sc2, sh2)
        x = _mlp_call(h2, x1, mlp_w1[l].astype(BF16), mlp_w2[l].astype(BF16), gt2,
                      row(final_g), final=(l == DEPTH - 1))
    return x
```

```python
import functools

import jax
import jax.numpy as jnp
from jax import lax
from jax.experimental import pallas as pl
from jax.experimental.pallas import tpu as pltpu

F32 = jnp.float32
BF16 = jnp.bfloat16

D_MODEL = 2048
DEPTH = 2
CHUNK = 64
NORM_EPS = 1e-6
LN_EPS = 1e-5
D_FF = 4 * D_MODEL

RWKV_HEAD = 64
RWKV_DIM = 512
RWKV_HEADS = 8
RWKV_PAIRS = RWKV_HEADS // 2
DECAY_LORA = 32
ICLR_LORA = 32
VRES_LORA = 32
GATE_LORA = 96
RWKV_GN_EPS = 64e-5
LORA_PAD = 256
RWKV_DECAY_SCALE = 0.6065306597126334
RWKV_INV_BASE = 8
RWKV_INV_PASSES = 3
RWKV_BASE_PASSES = 2
RWKV_CHUNKS_PER_ITER = 4

CONV_DIM = 512
CONV_WIDTH = 31
CONV_HALO = 32

MLA_DIM = 1024
V_HEAD = 128
MLA_HEADS = 8
QK_NOPE = 128
QK_ROPE = 64
Q_LORA = 512
KV_LORA = 256
ROPE_THETA = 10000.0
QK_PAD = 256
ATTN_HEADS_PER_STEP = 4
ATTN_TILE = 512
MLA_IN_PAD = 1024

RWKV_IN = 3 * RWKV_DIM + DECAY_LORA + ICLR_LORA + GATE_LORA
CONV_IN = 2 * CONV_DIM
N_IN = RWKV_IN + CONV_IN + Q_LORA + KV_LORA + QK_ROPE

LANE = 128
V7X_VMEM_LIMIT = 56 * 1024 * 1024
MLP_VMEM_LIMIT = 60 * 1024 * 1024
MASK_VALUE = -1e30
LOG2_E = 1.4426950408889634


def _params(semantics, vmem=V7X_VMEM_LIMIT):
    return pltpu.CompilerParams(dimension_semantics=semantics, vmem_limit_bytes=vmem)


def _resident(shape):
    nd = len(shape)
    return pl.BlockSpec(shape, lambda *_: (0,) * nd, pipeline_mode=pl.Buffered(1))


def _dot(a, b):
    return jnp.dot(a, b, preferred_element_type=F32)


def _dot_nt(a, b):
    return lax.dot_general(a, b, (((1,), (1,)), ((), ())), preferred_element_type=F32)


def _split(x):
    hi = x.astype(BF16)
    lo = (x - hi.astype(F32)).astype(BF16)
    return hi, lo


def _dot3(a, b):
    ah, al = _split(a)
    bh, bl = _split(b)
    return _dot(jnp.concatenate([ah, al, ah], axis=1), jnp.concatenate([bh, bh, bl], axis=0))


def _dot2(a, b):
    bh = b.astype(BF16)
    return _dot(jnp.concatenate(_split(a), axis=1), jnp.concatenate([bh, bh], axis=0))


def _dot2_nt(a, b):
    bh = b.astype(BF16)
    return _dot_nt(jnp.concatenate(_split(a), axis=1), jnp.concatenate([bh, bh], axis=1))


def _dot2_exact_rhs(a, b_bf16):
    ah, al = _split(a)
    return _dot(jnp.concatenate([ah, al], axis=1), jnp.concatenate([b_bf16] * 2, axis=0))


def _rms(x, eps):
    return x * lax.rsqrt(jnp.mean(x * x, axis=-1, keepdims=True) + eps)


def _sigmoid(x):
    return 0.5 * jnp.tanh(0.5 * x) + 0.5


def _iota(shape, dim):
    return lax.broadcasted_iota(jnp.int32, shape, dim)


def _ada_kernel(c_ref, w_ref, b_ref, o_ref):
    c = c_ref[...]
    o_ref[...] = _dot3(c * _sigmoid(c), w_ref[...]) + b_ref[...]


def _ada_call(c, ada_w, ada_b):
    depth, d, n = ada_w.shape
    bsz = c.shape[0]
    tn = 512
    return pl.pallas_call(
        _ada_kernel,
        out_shape=jax.ShapeDtypeStruct((depth, bsz, n), F32),
        grid=(depth, n // tn),
        in_specs=[
            pl.BlockSpec((bsz, d), lambda l, j: (0, 0)),
            pl.BlockSpec((None, d, tn), lambda l, j: (l, 0, j)),
            pl.BlockSpec((None, 1, tn), lambda l, j: (l, 0, j)),
        ],
        out_specs=pl.BlockSpec((None, bsz, tn), lambda l, j: (l, 0, j)),
        compiler_params=_params(("parallel", "parallel")),
        name="ada_mod",
    )(c, ada_w, ada_b.reshape(depth, 1, n))


def _rope_kernel(pos_ref, invf_ref, cos_ref, sin_ref):
    ang = pos_ref[...] * invf_ref[...]
    cos_ref[...] = jnp.cos(ang)
    sin_ref[...] = jnp.sin(ang)


def _rope_call(positions):
    bsz, s = positions.shape
    ts = min(s, 1024)
    inv_freq = ROPE_THETA ** (-jnp.arange(0, QK_ROPE, 2, dtype=F32) / QK_ROPE)
    invf = jnp.tile(inv_freq, LANE // (QK_ROPE // 2)).reshape(1, LANE)
    pos = positions.astype(F32).reshape(bsz, s, 1)
    shp = jax.ShapeDtypeStruct((bsz, s, LANE), F32)
    return pl.pallas_call(
        _rope_kernel,
        out_shape=(shp, shp),
        grid=(bsz, s // ts),
        in_specs=[
            pl.BlockSpec((None, ts, 1), lambda b, i: (b, i, 0)),
            pl.BlockSpec((1, LANE), lambda b, i: (0, 0)),
        ],
        out_specs=(
            pl.BlockSpec((None, ts, LANE), lambda b, i: (b, i, 0)),
            pl.BlockSpec((None, ts, LANE), lambda b, i: (b, i, 0)),
        ),
        compiler_params=_params(("parallel", "parallel")),
        name="rope_tables",
    )(pos, invf)


def _inproj_kernel(x_ref, g_ref, sc_ref, sh_ref, w_rkv, w_lora, w_conv, w_mla,
                   o_rkv, o_lora, o_conv, o_mla):
    x = x_ref[...]
    h = _rms(x, NORM_EPS) * g_ref[...] * (1.0 + sc_ref[...]) + sh_ref[...]
    hb = h.astype(BF16)
    o_rkv[...] = _dot(hb, w_rkv[...])
    o_lora[...] = _dot(hb, w_lora[...])
    o_conv[...] = _dot(hb, w_conv[...]).astype(o_conv.dtype)
    o_mla[...] = _dot(hb, w_mla[...]).astype(o_mla.dtype)


def _inproj_call(x, g, sc, sh, w_rkv, w_lora, w_conv, w_mla):
    bsz, s, d = x.shape
    tm = min(s, 512)
    widths = (w_rkv.shape[1], w_lora.shape[1], w_conv.shape[1], w_mla.shape[1])
    tok = lambda n: pl.BlockSpec((None, tm, n), lambda b, i: (b, i, 0))
    per_b = pl.BlockSpec((None, 1, d), lambda b, i: (b, 0, 0))
    return pl.pallas_call(
        _inproj_kernel,
        out_shape=tuple(jax.ShapeDtypeStruct((bsz, s, n), dt)
                        for n, dt in zip(widths, (F32, F32, BF16, BF16))),
        grid=(bsz, s // tm),
        in_specs=[tok(d), _resident((1, d)), per_b, per_b,
                  _resident(w_rkv.shape), _resident(w_lora.shape),
                  _resident(w_conv.shape), _resident(w_mla.shape)],
        out_specs=tuple(tok(n) for n in widths),
        compiler_params=_params(("parallel", "parallel")),
        name="norm_inproj",
    )(x, g, sc, sh, w_rkv, w_lora, w_conv, w_mla)


def _block_diag(y):
    lane = _iota(y.shape, 1)
    zero = jnp.zeros_like(y)
    return jnp.concatenate([jnp.where(lane < RWKV_HEAD, y, zero),
                            jnp.where(lane >= RWKV_HEAD, y, zero)], axis=0)


def _packed_mm(x, ys, passes=3):
    if passes == 1:
        lhs = x.astype(BF16)
        cols = [_block_diag(y.astype(BF16)) for y in ys]
    elif passes == 2:
        lhs = jnp.concatenate(_split(x), axis=1)
        cols = [jnp.concatenate([_block_diag(y.astype(BF16))] * 2, axis=0) for y in ys]
    else:
        xh, xl = _split(x)
        lhs = jnp.concatenate([xh, xl, xh], axis=1)
        cols = []
        for y in ys:
            yh, yl = _split(y)
            cols.append(jnp.concatenate([_block_diag(yh), _block_diag(yh), _block_diag(yl)], axis=0))
    out = _dot(lhs, cols[0] if len(cols) == 1 else jnp.concatenate(cols, axis=1))
    return [out[:, i * LANE:(i + 1) * LANE] for i in range(len(ys))]


def _head_sum_matrix(scale):
    r = _iota((LANE, LANE), 0) // RWKV_HEAD
    c = _iota((LANE, LANE), 1) // RWKV_HEAD
    return jnp.where(r == c, scale, 0.0).astype(BF16)


def _rwkv_kernel(*refs, has_vres, tt):
    if has_vres:
        (prkv_ref, plora_ref, vfirst_ref, mu_rkv, mu_lora, wc_ref, w0_ref, a0_ref, v0_ref,
         kk_ref, ka_ref, rk_ref, lng_ref, lnb_ref, y_ref,
         prev_rkv, prev_lora, h_scr, at_s, rt_s, bt_s, kt_s, bp_s, kp_s, v_s, pf_s, o_s) = refs
    else:
        (prkv_ref, plora_ref, mu_rkv, mu_lora, wc_ref, w0_ref, a0_ref,
         kk_ref, ka_ref, rk_ref, lng_ref, lnb_ref, y_ref, vout_ref,
         prev_rkv, prev_lora, h_scr, at_s, rt_s, bt_s, kt_s, bp_s, kp_s, v_s, pf_s, o_s) = refs
    n_chunks = tt // CHUNK

    @pl.when(pl.program_id(1) == 0)
    def _():
        prev_rkv[...] = jnp.zeros_like(prev_rkv)
        prev_lora[...] = jnp.zeros_like(prev_lora)
        h_scr[...] = jnp.zeros_like(h_scr)

    def shift_mix(p_ref, prev_ref, mu_ref):
        p = p_ref[...]
        row = _iota(p.shape, 0)
        prev = jnp.where(row == 0, prev_ref[0:1, :], pltpu.roll(p, 1, 0))
        prev_ref[0:1, :] = p[tt - 1:tt, :]
        return p + (prev - p) * mu_ref[...]

    xs = shift_mix(prkv_ref, prev_rkv, mu_rkv)
    xl = shift_mix(plora_ref, prev_lora, mu_lora)
    r = xs[:, 0:RWKV_DIM]
    k = xs[:, RWKV_DIM:2 * RWKV_DIM]
    v = xs[:, 2 * RWKV_DIM:3 * RWKV_DIM]

    lane = _iota(xl.shape, 1)
    o_w, o_a, o_g = DECAY_LORA, DECAY_LORA + ICLR_LORA, DECAY_LORA + ICLR_LORA + GATE_LORA
    act = jnp.where(lane < o_w, jnp.tanh(xl),
                    jnp.where((lane >= o_a) & (lane < o_g), _sigmoid(xl), xl))
    act_h, act_l = _split(act)
    lora = _dot(jnp.concatenate([act_h, act_l, act_h], axis=1), wc_ref[...])
    z = w0_ref[...] + lora[:, 0:RWKV_DIM]
    lw = -RWKV_DECAY_SCALE * _sigmoid(z)
    a_ic = _sigmoid(a0_ref[...] + lora[:, RWKV_DIM:2 * RWKV_DIM])
    gate = lora[:, 2 * RWKV_DIM:3 * RWKV_DIM]
    if has_vres:
        v = v + (vfirst_ref[...] - v) * _sigmoid(v0_ref[...] + lora[:, 3 * RWKV_DIM:4 * RWKV_DIM])
    else:
        vout_ref[...] = v

    ones_bd = _head_sum_matrix(1.0)
    mean_bd = _head_sum_matrix(1.0 / RWKV_HEAD)

    def per_head(x, mat, split=False):
        mm = _dot2_exact_rhs if split else (lambda a, b: _dot(a.astype(BF16), b))
        return jnp.concatenate(
            [mm(x[:, p * LANE:(p + 1) * LANE], mat) for p in range(RWKV_PAIRS)], axis=1)

    kk = k * kk_ref[...]
    kk = kk * lax.rsqrt(jnp.maximum(per_head(kk * kk, ones_bd), 1e-24))
    k2 = k * (1.0 + (a_ic - 1.0) * ka_ref[...])
    a_vec = -kk
    b_vec = kk * a_ic
    bonus = per_head(r * k2 * rk_ref[...], ones_bd) * v

    tr = _iota((tt, tt), 0)
    tc = _iota((tt, tt), 1)
    tri = jnp.where((tr // CHUNK == tc // CHUNK) & (tc <= tr), 1.0, 0.0).astype(BF16)
    cum = _dot2_exact_rhs_left(tri, lw)
    cum3 = cum.reshape(n_chunks, CHUNK, RWKV_DIM)
    cum_end = jnp.broadcast_to(cum3[:, CHUNK - 1:CHUNK, :], cum3.shape).reshape(tt, RWKV_DIM)
    pf = jnp.exp(cum)
    pinv = jnp.exp(-cum)
    pprev = jnp.exp(cum - lw)
    pend = jnp.exp(cum_end - cum)

    def put(dst, val):
        for p in range(RWKV_PAIRS):
            dst[p] = val[:, p * LANE:(p + 1) * LANE]

    put(at_s, a_vec * pprev)
    put(rt_s, r * pf)
    put(bt_s, b_vec * pinv)
    put(kt_s, k2 * pinv)
    put(bp_s, b_vec * pend)
    put(kp_s, k2 * pend)
    put(v_s, v)
    put(pf_s, pf)

    t_loc = _iota((CHUNK, LANE), 0)
    s_loc = _iota((CHUNK, LANE), 1) % RWKV_HEAD
    strict = s_loc < t_loc
    incl = s_loc <= t_loc
    eye_p = jnp.where(s_loc == t_loc, 1.0, 0.0)
    same_block = lambda n: (t_loc // n) == (s_loc // n)
    in_base = same_block(RWKV_INV_BASE)
    merge_masks = []
    n = RWKV_INV_BASE
    while n < CHUNK:
        merge_masks.append(same_block(2 * n) & jnp.logical_not(same_block(n)))
        n *= 2

    group = min(n_chunks, RWKV_CHUNKS_PER_ITER)

    def chunk_group(cg, carry):
        streams = []
        for ci in range(group):
            c = cg * group + ci
            rows = pl.ds(pl.multiple_of(c * CHUNK, CHUNK), CHUNK)
            last = pl.ds(c * CHUNK + CHUNK - 1, 1)
            streams += [(p, rows, last) for p in range(RWKV_PAIRS)]
        ns = range(len(streams))
        load = lambda ref: [ref[p, rows, :] for (p, rows, _) in streams]
        at, rt, bt, kt, bp, kp, vv = (load(r) for r in (at_s, rt_s, bt_s, kt_s, bp_s, kp_s, v_s))
        g = [_dot2_nt(jnp.concatenate([at[i], rt[i]], axis=0),
                      jnp.concatenate([_block_diag(bt[i]), _block_diag(kt[i])], axis=0)) for i in ns]
        aab = [jnp.where(strict, g[i][0:CHUNK, 0:LANE], 0.0) for i in ns]
        aak = [jnp.where(strict, g[i][0:CHUNK, LANE:2 * LANE], 0.0) for i in ns]
        arb = [jnp.where(incl, g[i][CHUNK:2 * CHUNK, 0:LANE], 0.0) for i in ns]
        ark = [jnp.where(incl, g[i][CHUNK:2 * CHUNK, LANE:2 * LANE], 0.0) for i in ns]
        n_diag = [jnp.where(in_base, aab[i], 0.0) for i in ns]
        inv, base = RWKV_INV_PASSES, RWKV_BASE_PASSES
        dpow = [_packed_mm(n_diag[i], [n_diag[i]], passes=base)[0] for i in ns]
        tinv = [eye_p + n_diag[i] for i in ns]
        akv = [_packed_mm(aak[i], [vv[i]], passes=1)[0] for i in ns]
        arkv = [_packed_mm(ark[i], [vv[i]], passes=1)[0] for i in ns]
        res = [_packed_mm(dpow[i], [dpow[i], tinv[i]], passes=base) for i in ns]
        tinv = [tinv[i] + res[i][1] for i in ns]
        tinv = [tinv[i] + _packed_mm(res[i][0], [tinv[i]], passes=base)[0] for i in ns]
        for level_mask in merge_masks:
            ct = [_packed_mm(jnp.where(level_mask, aab[i], 0.0), [tinv[i]], passes=inv)[0] for i in ns]
            tinv = [tinv[i] + _packed_mm(tinv[i], [ct[i]], passes=inv)[0] for i in ns]
        wu = [_packed_mm(tinv[i], [at[i], akv[i]], passes=inv) for i in ns]
        ab = [_packed_mm(arb[i], wu[i], passes=1) for i in ns]
        q_m = [rt[i] + ab[i][0] for i in ns]
        y0 = [ab[i][1] + arkv[i] for i in ns]
        upd = [_dot2(jnp.concatenate([bp[i], kp[i]], axis=0).T,
                     jnp.concatenate([jnp.concatenate(wu[i], axis=1),
                                      jnp.concatenate([jnp.zeros_like(vv[i]), vv[i]], axis=1)], axis=0))
               for i in ns]
        first_head = _iota((CHUNK, 2 * LANE), 1) % LANE < RWKV_HEAD
        upd = [jnp.where(first_head, upd[i][0:RWKV_HEAD], upd[i][RWKV_HEAD:2 * RWKV_HEAD]) for i in ns]
        h = [h_scr[p] for p in range(RWKV_PAIRS)]
        for i in ns:
            p, rows, last = streams[i]
            m_p = upd[i][:, 0:LANE] + jnp.where(s_loc == t_loc, pf_s[p, last, :], 0.0)
            (qh_mh,) = _packed_mm(jnp.concatenate([q_m[i], m_p], axis=0), [h[p]], passes=3)
            o_s[rows, p * LANE:(p + 1) * LANE] = qh_mh[0:CHUNK] + y0[i]
            h[p] = qh_mh[CHUNK:2 * CHUNK] + upd[i][:, LANE:2 * LANE]
        for p in range(RWKV_PAIRS):
            h_scr[p] = h[p]
        return carry

    lax.fori_loop(0, n_chunks // group, chunk_group, 0)

    o = o_s[...]
    mean = per_head(o, mean_bd, split=True)
    oc = o - mean
    var = per_head(oc * oc, mean_bd)
    on = oc * lax.rsqrt(var + RWKV_GN_EPS) * lng_ref[...] + lnb_ref[...]
    y_ref[...] = ((on + bonus) * gate).astype(y_ref.dtype)


def _dot2_exact_rhs_left(m_bf16, x):
    xh, xl = _split(x)
    return _dot(jnp.concatenate([m_bf16] * 2, axis=1), jnp.concatenate([xh, xl], axis=0))


def _rwkv_call(p_rkv, p_lora, v_first, prm):
    bsz, s, _ = p_rkv.shape
    tt = min(s, 256)
    has_vres = v_first is not None
    tok = lambda n: pl.BlockSpec((None, tt, n), lambda b, i: (b, i, 0))
    row = lambda n: _resident((1, n))
    in_specs = [tok(3 * RWKV_DIM), tok(LORA_PAD)]
    args = [p_rkv, p_lora]
    if has_vres:
        in_specs.append(tok(RWKV_DIM))
        args.append(v_first)
    in_specs += [row(3 * RWKV_DIM), row(LORA_PAD), _resident((3 * LORA_PAD, 4 * RWKV_DIM)),
                 row(RWKV_DIM), row(RWKV_DIM)]
    args += [prm["mu_rkv"], prm["mu_lora"], prm["w_lora_up"], prm["w0"], prm["a0"]]
    if has_vres:
        in_specs.append(row(RWKV_DIM))
        args.append(prm["v0"])
    in_specs += [row(RWKV_DIM)] * 5
    args += [prm["k_k"], prm["k_a"], prm["r_k"], prm["lnx_g"], prm["lnx_b"]]
    y_shape = jax.ShapeDtypeStruct((bsz, s, RWKV_DIM), BF16)
    if has_vres:
        out_shape, out_specs = y_shape, tok(RWKV_DIM)
    else:
        out_shape = (y_shape, jax.ShapeDtypeStruct((bsz, s, RWKV_DIM), F32))
        out_specs = (tok(RWKV_DIM), tok(RWKV_DIM))
    pair_tile = pltpu.VMEM((RWKV_PAIRS, tt, LANE), F32)
    out = pl.pallas_call(
        functools.partial(_rwkv_kernel, has_vres=has_vres, tt=tt),
        out_shape=out_shape,
        grid=(bsz, s // tt),
        in_specs=in_specs,
        out_specs=out_specs,
        scratch_shapes=[
            pltpu.VMEM((8, 3 * RWKV_DIM), F32),
            pltpu.VMEM((8, LORA_PAD), F32),
            pltpu.VMEM((RWKV_PAIRS, RWKV_HEAD, LANE), F32),
        ] + [pair_tile] * 8 + [pltpu.VMEM((tt, RWKV_DIM), F32)],
        compiler_params=_params(("parallel", "arbitrary")),
        name="rwkv7_mixer",
    )(*args)
    return out if has_vres else out


def _conv_kernel(p_ref, w_ref, b_ref, g_ref, be_ref, o_ref, ubuf, *, tt):
    @pl.when(pl.program_id(1) == 0)
    def _():
        ubuf[0:CONV_HALO, :] = jnp.zeros((CONV_HALO, CONV_DIM), F32)

    p = p_ref[...].astype(F32)
    ubuf[CONV_HALO:CONV_HALO + tt, :] = p[:, 0:CONV_DIM] * _sigmoid(p[:, CONV_DIM:2 * CONV_DIM])
    sub = 8
    base = CONV_HALO - sub
    ext = tt + sub
    acc = None
    for b in range(sub):
        part = None
        for a in range((CONV_WIDTH - 1 - b) // sub + 1):
            j = CONV_WIDTH - 1 - (sub * a + b)
            term = ubuf[base - sub * a:base - sub * a + ext, :] * w_ref[j:j + 1, :]
            part = term if part is None else part + term
        if b:
            part = pltpu.roll(part, b, 0)
        acc = part if acc is None else acc + part
    acc = acc[sub:sub + tt, :] + b_ref[...]
    ubuf[0:CONV_HALO, :] = ubuf[tt:tt + CONV_HALO, :]
    mean = jnp.mean(acc, axis=-1, keepdims=True)
    xc = acc - mean
    var = jnp.mean(xc * xc, axis=-1, keepdims=True)
    u = xc * lax.rsqrt(var + LN_EPS) * g_ref[...] + be_ref[...]
    o_ref[...] = (u * _sigmoid(u)).astype(o_ref.dtype)


def _conv_call(p_conv, conv_w, conv_b, ln_g, ln_b):
    bsz, s, _ = p_conv.shape
    tt = min(s, 512)
    wpad = jnp.zeros((CONV_HALO, CONV_DIM), F32).at[:CONV_WIDTH].set(conv_w)
    row = _resident((1, CONV_DIM))
    return pl.pallas_call(
        functools.partial(_conv_kernel, tt=tt),
        out_shape=jax.ShapeDtypeStruct((bsz, s, CONV_DIM), BF16),
        grid=(bsz, s // tt),
        in_specs=[pl.BlockSpec((None, tt, 2 * CONV_DIM), lambda b, i: (b, i, 0)),
                  _resident((CONV_HALO, CONV_DIM)), row, row, row],
        out_specs=pl.BlockSpec((None, tt, CONV_DIM), lambda b, i: (b, i, 0)),
        scratch_shapes=[pltpu.VMEM((tt + CONV_HALO, CONV_DIM), F32)],
        compiler_params=_params(("parallel", "arbitrary")),
        name="conformer_conv",
    )(p_conv, wpad, conv_b.reshape(1, -1), ln_g.reshape(1, -1), ln_b.reshape(1, -1))


def _mla_prep_kernel(p_ref, cos_ref, sin_ref, qg_ref, kvg_ref, wq_ref, wqs_ref, wkv_ref,
                     q_ref, k_ref, v_ref):
    p = p_ref[...].astype(F32)
    cos = cos_ref[...]
    sin = sin_ref[...]
    qc = (_rms(p[:, 0:Q_LORA], NORM_EPS) * qg_ref[...]).astype(BF16)
    kvc = (_rms(p[:, Q_LORA:Q_LORA + KV_LORA], NORM_EPS) * kvg_ref[...]).astype(BF16)
    q = _dot(qc, wq_ref[...])
    qs = _dot(qc, wqs_ref[...])
    kv = _dot(kvc, wkv_ref[...])
    o_kr = Q_LORA + KV_LORA
    k_rope = (p[:, o_kr:o_kr + LANE] * cos + p[:, o_kr + LANE:o_kr + 2 * LANE] * sin).astype(BF16)
    for h in range(MLA_HEADS):
        b0 = h * QK_PAD
        q_ref[:, b0:b0 + LANE] = q[:, b0:b0 + LANE].astype(BF16)
        q_ref[:, b0 + LANE:b0 + 2 * LANE] = (
            q[:, b0 + LANE:b0 + 2 * LANE] * cos + qs[:, h * LANE:(h + 1) * LANE] * sin).astype(BF16)
        k_ref[:, b0:b0 + LANE] = kv[:, b0:b0 + LANE].astype(BF16)
        k_ref[:, b0 + LANE:b0 + 2 * LANE] = k_rope
        v_ref[h * V_HEAD:(h + 1) * V_HEAD, :] = kv[:, b0 + LANE:b0 + 2 * LANE].T.astype(BF16)


def _mla_prep_call(p_mla, cos, sin, qg, kvg, wq, wqs, wkv):
    bsz, s, _ = p_mla.shape
    tm = min(s, ATTN_TILE)
    tok = lambda n: pl.BlockSpec((None, tm, n), lambda b, i: (b, i, 0))
    return pl.pallas_call(
        _mla_prep_kernel,
        out_shape=(jax.ShapeDtypeStruct((bsz, s, MLA_HEADS * QK_PAD), BF16),
                   jax.ShapeDtypeStruct((bsz, s, MLA_HEADS * QK_PAD), BF16),
                   jax.ShapeDtypeStruct((bsz, s // tm, MLA_DIM, tm), BF16)),
        grid=(bsz, s // tm),
        in_specs=[tok(MLA_IN_PAD), tok(LANE), tok(LANE),
                  _resident((1, Q_LORA)), _resident((1, KV_LORA)),
                  _resident(wq.shape), _resident(wqs.shape), _resident(wkv.shape)],
        out_specs=(tok(MLA_HEADS * QK_PAD), tok(MLA_HEADS * QK_PAD),
                   pl.BlockSpec((None, None, MLA_DIM, tm), lambda b, i: (b, i, 0, 0))),
        compiler_params=_params(("parallel", "parallel")),
        name="mla_prep",
    )(p_mla, cos, sin, qg, kvg, wq, wqs, wkv)


def _attn_kernel(q_ref, k_ref, vt_ref, o_ref, *, tq, nh):
    i = pl.program_id(2)
    heads = range(nh)
    q = [q_ref[:, h * QK_PAD:(h + 1) * QK_PAD] for h in heads]

    def step(j, carry, masked):
        keys = pl.ds(pl.multiple_of(j * tq, tq), tq)
        st = [_dot_nt(k_ref[keys, h * QK_PAD:(h + 1) * QK_PAD], q[h]) for h in heads]
        if masked:
            visible = (_iota((tq, tq), 0) // CHUNK) <= (_iota((tq, tq), 1) // CHUNK)
            st = [jnp.where(visible, s, MASK_VALUE) for s in st]
        m_new = [jnp.maximum(carry[h][0], jnp.max(st[h], axis=0, keepdims=True)) for h in heads]
        alpha = [jnp.exp2(carry[h][0] - m_new[h]) for h in heads]
        pt = [jnp.exp2(st[h] - m_new[h]) for h in heads]
        l_new = [alpha[h] * carry[h][1] + jnp.sum(pt[h], axis=0, keepdims=True) for h in heads]
        pv = [_dot(vt_ref[j, h * V_HEAD:(h + 1) * V_HEAD, :], pt[h].astype(BF16)) for h in heads]
        return tuple((m_new[h], l_new[h], alpha[h] * carry[h][2] + pv[h]) for h in heads)

    init = (jnp.full((1, tq), MASK_VALUE, F32), jnp.zeros((1, tq), F32), jnp.zeros((V_HEAD, tq), F32))
    carry = lax.fori_loop(0, i, lambda j, c: step(j, c, False), (init,) * nh)
    carry = step(i, carry, True)
    for h in heads:
        o_ref[:, h * V_HEAD:(h + 1) * V_HEAD] = (carry[h][2] / carry[h][1]).T.astype(o_ref.dtype)


def _attn_call(q, k, vt):
    bsz, s, _ = q.shape
    tq = min(s, ATTN_TILE)
    nh = ATTN_HEADS_PER_STEP
    return pl.pallas_call(
        functools.partial(_attn_kernel, tq=tq, nh=nh),
        out_shape=jax.ShapeDtypeStruct((bsz, s, MLA_DIM), BF16),
        grid=(bsz, MLA_HEADS // nh, s // tq),
        in_specs=[pl.BlockSpec((None, tq, nh * QK_PAD), lambda b, h, i: (b, i, h)),
                  pl.BlockSpec((None, s, nh * QK_PAD), lambda b, h, i: (b, 0, h)),
                  pl.BlockSpec((None, s // tq, nh * V_HEAD, tq), lambda b, h, i: (b, 0, h, 0))],
        out_specs=pl.BlockSpec((None, tq, nh * V_HEAD), lambda b, h, i: (b, i, h)),
        compiler_params=_params(("parallel", "parallel", "arbitrary")),
        name="mla_attention",
    )(q, k, vt)


def _outproj_kernel(ya_ref, yb_ref, yc_ref, x_ref, wo_ref, gt_ref, g_ref, sc_ref, sh_ref,
                    x1_ref, h2_ref):
    o1, o2 = RWKV_DIM, RWKV_DIM + CONV_DIM
    half = x_ref.shape[0] // 2
    ys = []
    for r in (slice(0, half), slice(half, 2 * half)):
        ys.append(_dot(ya_ref[r, :], wo_ref[0:o1, :]) + _dot(yb_ref[r, :], wo_ref[o1:o2, :])
                  + _dot(yc_ref[r, :], wo_ref[o2:o2 + MLA_DIM, :]))
    for r, y in zip((slice(0, half), slice(half, 2 * half)), ys):
        x1 = x_ref[r, :] + gt_ref[...] * y
        x1_ref[r, :] = x1
        h2_ref[r, :] = (_rms(x1, NORM_EPS) * g_ref[...] * (1.0 + sc_ref[...]) + sh_ref[...]).astype(BF16)


def _outproj_call(ya, yb, yc, x, wo, gt, g, sc, sh):
    bsz, s, d = x.shape
    tm = min(s, 512)
    tok = lambda n: pl.BlockSpec((None, tm, n), lambda b, i: (b, i, 0))
    per_b = pl.BlockSpec((None, 1, d), lambda b, i: (b, 0, 0))
    return pl.pallas_call(
        _outproj_kernel,
        out_shape=(jax.ShapeDtypeStruct((bsz, s, d), F32), jax.ShapeDtypeStruct((bsz, s, d), BF16)),
        grid=(bsz, s // tm),
        in_specs=[tok(RWKV_DIM), tok(CONV_DIM), tok(MLA_DIM), tok(d), _resident(wo.shape),
                  per_b, _resident((1, d)), per_b, per_b],
        out_specs=(tok(d), tok(d)),
        compiler_params=_params(("parallel", "parallel")),
        name="outproj_norm2",
    )(ya, yb, yc, x, wo, gt, g, sc, sh)


def _mlp_kernel(h_ref, x_ref, w1_ref, w2_ref, gt_ref, fg_ref, o_ref, *, final):
    f = pl.program_id(2)

    @pl.when(f == 0)
    def _():
        o_ref[...] = jnp.zeros_like(o_ref)

    a = jnp.maximum(_dot(h_ref[...], w1_ref[...]), 0.0)
    o_ref[...] += _dot((a * a).astype(BF16), w2_ref[...])

    @pl.when(f == pl.num_programs(2) - 1)
    def _():
        xo = x_ref[...] + gt_ref[...] * o_ref[...]
        if final:
            xo = _rms(xo, NORM_EPS) * fg_ref[...]
        o_ref[...] = xo


def _mlp_call(h2, x1, w1, w2, gt, fg, final):
    bsz, s, d = x1.shape
    tm = min(s, 512)
    tf = 2048
    tok_map = lambda b, i, f: (b, i, 0)
    tok = lambda: pl.BlockSpec((None, tm, d), tok_map)
    return pl.pallas_call(
        functools.partial(_mlp_kernel, final=final),
        out_shape=jax.ShapeDtypeStruct((bsz, s, d), F32),
        grid=(bsz, s // tm, D_FF // tf),
        in_specs=[tok(), tok(),
                  pl.BlockSpec((d, tf), lambda b, i, f: (0, f)),
                  pl.BlockSpec((tf, d), lambda b, i, f: (f, 0)),
                  pl.BlockSpec((None, 1, d), lambda b, i, f: (b, 0, 0)),
                  pl.BlockSpec((1, d), lambda b, i, f: (0, 0))],
        out_specs=tok(),
        compiler_params=_params(("parallel", "parallel", "arbitrary"), MLP_VMEM_LIMIT),
        name="relu2_mlp",
    )(h2, x1, w1, w2, gt, fg)


def _pad_cols(w, n):
    return jnp.pad(w, ((0, 0), (0, n - w.shape[1])))


def _rotate_half_cols(w):
    half = w.shape[-1] // 2
    return jnp.concatenate([-w[..., half:], w[..., :half]], axis=-1)


def _layer_weights(l, w_in, w_in_vres, rwkv_mu, vres_mu, decay_up, iclr_up, gate_up, vres_up,
                   w_qb, w_kvb):
    d = D_MODEL
    wl = w_in[l]
    o_lora = 3 * RWKV_DIM
    o_conv = RWKV_IN
    o_mla = RWKV_IN + CONV_IN
    has_vres = l > 0
    lora_cols = [wl[:, o_lora:o_conv]]
    mu_cols = [rwkv_mu[l, o_lora:o_conv]]
    if has_vres:
        lora_cols.append(w_in_vres[l - 1])
        mu_cols.append(vres_mu[l - 1])
    w_lora = _pad_cols(jnp.concatenate(lora_cols, axis=1), LORA_PAD)
    mu_lora = jnp.pad(jnp.concatenate(mu_cols), (0, LORA_PAD - sum(m.shape[0] for m in mu_cols)))
    mla = wl[:, o_mla:N_IN]
    kr = mla[:, Q_LORA + KV_LORA:]
    zpad = jnp.zeros((d, LANE - QK_ROPE), F32)
    w_mla = jnp.concatenate([mla[:, :Q_LORA + KV_LORA], kr, zpad, _rotate_half_cols(kr), zpad], axis=1)

    up = jnp.zeros((LORA_PAD, 4 * RWKV_DIM), F32)
    o_a, o_g = DECAY_LORA, DECAY_LORA + ICLR_LORA
    o_v = o_g + GATE_LORA
    up = up.at[0:o_a, 0:RWKV_DIM].set(decay_up[l])
    up = up.at[o_a:o_g, RWKV_DIM:2 * RWKV_DIM].set(iclr_up[l])
    up = up.at[o_g:o_v, 2 * RWKV_DIM:3 * RWKV_DIM].set(gate_up[l])
    if has_vres:
        up = up.at[o_v:o_v + VRES_LORA, 3 * RWKV_DIM:].set(vres_up[l - 1])
    up_hi = up.astype(BF16)

    scale = (QK_NOPE + QK_ROPE) ** -0.5 * LOG2_E
    wq = (w_qb[l] * scale).reshape(Q_LORA, MLA_HEADS, QK_NOPE + QK_ROPE)
    nope, rope = wq[:, :, :QK_NOPE], wq[:, :, QK_NOPE:]
    z = jnp.zeros((Q_LORA, MLA_HEADS, LANE - QK_ROPE), F32)
    wq_main = jnp.concatenate([nope, rope, z], axis=2).reshape(Q_LORA, MLA_HEADS * QK_PAD)
    wq_rot = jnp.concatenate([_rotate_half_cols(rope), z], axis=2).reshape(Q_LORA, MLA_HEADS * LANE)
    return dict(
        w_rkv=wl[:, :o_lora].astype(BF16), w_lora=w_lora.astype(BF16),
        w_conv=wl[:, o_conv:o_mla].astype(BF16), w_mla=w_mla.astype(BF16),
        mu_rkv=rwkv_mu[l, :o_lora].reshape(1, -1), mu_lora=mu_lora.reshape(1, -1),
        w_lora_up=jnp.concatenate([up_hi, up_hi, (up - up_hi.astype(F32)).astype(BF16)], axis=0),
        wq=wq_main.astype(BF16), wq_rot=wq_rot.astype(BF16),
        wkv=w_kvb[l].astype(BF16))


def kernel(x, c, positions, ada_w, ada_b, norm1_g, norm2_g, final_g, w_in, w_in_vres, rwkv_mu,
           vres_mu, decay_w0, decay_up, iclr_a0, iclr_up, gate_up, vres_v0, vres_up, k_k, k_a, r_k,
           lnx_g, lnx_b, conv_w, conv_b, conv_ln_g, conv_ln_b, q_a_norm_g, w_qb, kv_a_norm_g, w_kvb,
           w_out, mlp_w1, mlp_w2):
    bsz, s, d = x.shape
    mod = _ada_call(c, ada_w, ada_b).reshape(DEPTH, bsz, 6, 1, d)
    cos, sin = _rope_call(positions)
    row = lambda a: a.reshape(1, -1)
    v_first = None
    for l in range(DEPTH):
        sh1, sc1, gt1, sh2, sc2, gt2 = (mod[l, :, j] for j in range(6))
        lw = _layer_weights(l, w_in, w_in_vres, rwkv_mu, vres_mu, decay_up, iclr_up, gate_up,
                            vres_up, w_qb, w_kvb)
        p_rkv, p_lora, p_conv, p_mla = _inproj_call(
            x, row(norm1_g[l]), sc1, sh1, lw["w_rkv"], lw["w_lora"], lw["w_conv"], lw["w_mla"])
        prm = dict(mu_rkv=lw["mu_rkv"], mu_lora=lw["mu_lora"], w_lora_up=lw["w_lora_up"],
                   w0=row(decay_w0[l]), a0=row(iclr_a0[l]), k_k=row(k_k[l]), k_a=row(k_a[l]),
                   r_k=row(r_k[l]), lnx_g=row(lnx_g[l]), lnx_b=row(lnx_b[l]))
        if l == 0:
            y_a, v_first = _rwkv_call(p_rkv, p_lora, None, prm)
        else:
            prm["v0"] = row(vres_v0[l - 1])
            y_a = _rwkv_call(p_rkv, p_lora, v_first, prm)
        y_b = _conv_call(p_conv, conv_w[l], conv_b[l], conv_ln_g[l], conv_ln_b[l])
        q, k, v = _mla_prep_call(p_mla, cos, sin, row(q_a_norm_g[l]), row(kv_a_norm_g[l]),
                                 lw["wq"], lw["wq_rot"], lw["wkv"])
        y_c = _attn_call(q, k, v)
        x1, h2 = _outproj_call(y_a, y_b, y_c, x, w_out[l].astype(BF16), gt1,
                               row(norm2_g[l]), sc2, sh2)
        x = _mlp_call(h2, x1, mlp_w1[l].astype(BF16), mlp_w2[l].astype(BF16), gt2,
                      row(final_g), final=(l == DEPTH - 1))
    return x
```

```python
import functools

import jax
import jax.numpy as jnp
from jax import lax
from jax.experimental import pallas as pl
from jax.experimental.pallas import tpu as pltpu

F32 = jnp.float32
BF16 = jnp.bfloat16

D_MODEL = 2048
DEPTH = 2
CHUNK = 64
NORM_EPS = 1e-6
LN_EPS = 1e-5
D_FF = 4 * D_MODEL

RWKV_HEAD = 64
RWKV_DIM = 512
RWKV_HEADS = 8
RWKV_PAIRS = RWKV_HEADS // 2
DECAY_LORA = 32
ICLR_LORA = 32
VRES_LORA = 32
GATE_LORA = 96
RWKV_GN_EPS = 64e-5
LORA_PAD = 256
RWKV_DECAY_SCALE = 0.6065306597126334
RWKV_INV_BASE = 8
RWKV_INV_PASSES = 3
RWKV_BASE_PASSES = 2
RWKV_CHUNKS_PER_ITER = 4

CONV_DIM = 512
CONV_WIDTH = 31
CONV_HALO = 32

MLA_DIM = 1024
V_HEAD = 128
MLA_HEADS = 8
QK_NOPE = 128
QK_ROPE = 64
Q_LORA = 512
KV_LORA = 256
ROPE_THETA = 10000.0
QK_PAD = 256
ATTN_HEADS_PER_STEP = 4
ATTN_TILE = 512
MLA_IN_PAD = 1024

RWKV_IN = 3 * RWKV_DIM + DECAY_LORA + ICLR_LORA + GATE_LORA
CONV_IN = 2 * CONV_DIM
N_IN = RWKV_IN + CONV_IN + Q_LORA + KV_LORA + QK_ROPE

LANE = 128
V7X_VMEM_LIMIT = 56 * 1024 * 1024
MLP_VMEM_LIMIT = 60 * 1024 * 1024
MASK_VALUE = -1e30
LOG2_E = 1.4426950408889634


def _params(semantics, vmem=V7X_VMEM_LIMIT):
    return pltpu.CompilerParams(dimension_semantics=semantics, vmem_limit_bytes=vmem)


def _resident(shape):
    nd = len(shape)
    return pl.BlockSpec(shape, lambda *_: (0,) * nd, pipeline_mode=pl.Buffered(1))


def _dot(a, b):
    return jnp.dot(a, b, preferred_element_type=F32)


def _dot_nt(a, b):
    return lax.dot_general(a, b, (((1,), (1,)), ((), ())), preferred_element_type=F32)


def _split(x):
    hi = x.astype(BF16)
    lo = (x - hi.astype(F32)).astype(BF16)
    return hi, lo


def _dot3(a, b):
    ah, al = _split(a)
    bh, bl = _split(b)
    return _dot(jnp.concatenate([ah, al, ah], axis=1), jnp.concatenate([bh, bh, bl], axis=0))


def _dot2(a, b):
    bh = b.astype(BF16)
    return _dot(jnp.concatenate(_split(a), axis=1), jnp.concatenate([bh, bh], axis=0))


def _dot2_nt(a, b):
    bh = b.astype(BF16)
    return _dot_nt(jnp.concatenate(_split(a), axis=1), jnp.concatenate([bh, bh], axis=1))


def _dot2_exact_rhs(a, b_bf16):
    ah, al = _split(a)
    return _dot(jnp.concatenate([ah, al], axis=1), jnp.concatenate([b_bf16] * 2, axis=0))


def _rms(x, eps):
    return x * lax.rsqrt(jnp.mean(x * x, axis=-1, keepdims=True) + eps)


def _sigmoid(x):
    return 0.5 * jnp.tanh(0.5 * x) + 0.5


def _iota(shape, dim):
    return lax.broadcasted_iota(jnp.int32, shape, dim)


def _ada_kernel(c_ref, w_ref, b_ref, o_ref):
    c = c_ref[...]
    o_ref[...] = _dot3(c * _sigmoid(c), w_ref[...]) + b_ref[...]


def _ada_call(c, ada_w, ada_b):
    depth, d, n = ada_w.shape
    bsz = c.shape[0]
    tn = 512
    return pl.pallas_call(
        _ada_kernel,
        out_shape=jax.ShapeDtypeStruct((depth, bsz, n), F32),
        grid=(depth, n // tn),
        in_specs=[
            pl.BlockSpec((bsz, d), lambda l, j: (0, 0)),
            pl.BlockSpec((None, d, tn), lambda l, j: (l, 0, j)),
            pl.BlockSpec((None, 1, tn), lambda l, j: (l, 0, j)),
        ],
        out_specs=pl.BlockSpec((None, bsz, tn), lambda l, j: (l, 0, j)),
        compiler_params=_params(("parallel", "parallel")),
        name="ada_mod",
    )(c, ada_w, ada_b.reshape(depth, 1, n))


def _rope_kernel(pos_ref, invf_ref, cos_ref, sin_ref):
    ang = pos_ref[...] * invf_ref[...]
    cos_ref[...] = jnp.cos(ang)
    sin_ref[...] = jnp.sin(ang)


def _rope_call(positions):
    bsz, s = positions.shape
    ts = min(s, 1024)
    inv_freq = ROPE_THETA ** (-jnp.arange(0, QK_ROPE, 2, dtype=F32) / QK_ROPE)
    invf = jnp.tile(inv_freq, LANE // (QK_ROPE // 2)).reshape(1, LANE)
    pos = positions.astype(F32).reshape(bsz, s, 1)
    shp = jax.ShapeDtypeStruct((bsz, s, LANE), F32)
    return pl.pallas_call(
        _rope_kernel,
        out_shape=(shp, shp),
        grid=(bsz, s // ts),
        in_specs=[
            pl.BlockSpec((None, ts, 1), lambda b, i: (b, i, 0)),
            pl.BlockSpec((1, LANE), lambda b, i: (0, 0)),
        ],
        out_specs=(
            pl.BlockSpec((None, ts, LANE), lambda b, i: (b, i, 0)),
            pl.BlockSpec((None, ts, LANE), lambda b, i: (b, i, 0)),
        ),
        compiler_params=_params(("parallel", "parallel")),
        name="rope_tables",
    )(pos, invf)


def _conv_mixer_tile(p, w_ref, b_ref, g_ref, be_ref, ubuf, tt):
    ubuf[CONV_HALO:CONV_HALO + tt, :] = p[:, 0:CONV_DIM] * _sigmoid(p[:, CONV_DIM:2 * CONV_DIM])
    sub = 8
    base = CONV_HALO - sub
    ext = tt + sub
    acc = None
    for b in range(sub):
        part = None
        for a in range((CONV_WIDTH - 1 - b) // sub + 1):
            j = CONV_WIDTH - 1 - (sub * a + b)
            term = ubuf[base - sub * a:base - sub * a + ext, :] * w_ref[j:j + 1, :]
            part = term if part is None else part + term
        if b:
            part = pltpu.roll(part, b, 0)
        acc = part if acc is None else acc + part
    acc = acc[sub:sub + tt, :] + b_ref[...]
    ubuf[0:CONV_HALO, :] = ubuf[tt:tt + CONV_HALO, :]
    mean = jnp.mean(acc, axis=-1, keepdims=True)
    xc = acc - mean
    var = jnp.mean(xc * xc, axis=-1, keepdims=True)
    u = xc * lax.rsqrt(var + LN_EPS) * g_ref[...] + be_ref[...]
    return u * _sigmoid(u)


def _inproj_kernel(x_ref, g_ref, sc_ref, sh_ref, w_rkv, w_lora, w_conv, w_mla,
                   cw_ref, cb_ref, cg_ref, cbe_ref, o_rkv, o_lora, o_yb, o_mla, ubuf):
    tm = x_ref.shape[0]

    @pl.when(pl.program_id(1) == 0)
    def _():
        ubuf[0:CONV_HALO, :] = jnp.zeros((CONV_HALO, CONV_DIM), F32)

    x = x_ref[...]
    h = _rms(x, NORM_EPS) * g_ref[...] * (1.0 + sc_ref[...]) + sh_ref[...]
    hb = h.astype(BF16)
    p_conv = _dot(hb, w_conv[...])
    o_rkv[...] = _dot(hb, w_rkv[...])
    o_lora[...] = _dot(hb, w_lora[...])
    o_mla[...] = _dot(hb, w_mla[...]).astype(o_mla.dtype)
    o_yb[...] = _conv_mixer_tile(p_conv, cw_ref, cb_ref, cg_ref, cbe_ref, ubuf, tm).astype(o_yb.dtype)


def _inproj_call(x, g, sc, sh, w_rkv, w_lora, w_conv, w_mla, conv_w, conv_b, ln_g, ln_b):
    bsz, s, d = x.shape
    tm = min(s, 512)
    widths = (w_rkv.shape[1], w_lora.shape[1], CONV_DIM, w_mla.shape[1])
    tok = lambda n: pl.BlockSpec((None, tm, n), lambda b, i: (b, i, 0))
    per_b = pl.BlockSpec((None, 1, d), lambda b, i: (b, 0, 0))
    wpad = jnp.zeros((CONV_HALO, CONV_DIM), F32).at[:CONV_WIDTH].set(conv_w)
    row = _resident((1, CONV_DIM))
    return pl.pallas_call(
        _inproj_kernel,
        out_shape=tuple(jax.ShapeDtypeStruct((bsz, s, n), dt)
                        for n, dt in zip(widths, (F32, F32, BF16, BF16))),
        grid=(bsz, s // tm),
        in_specs=[tok(d), _resident((1, d)), per_b, per_b,
                  _resident(w_rkv.shape), _resident(w_lora.shape),
                  _resident(w_conv.shape), _resident(w_mla.shape),
                  _resident((CONV_HALO, CONV_DIM)), row, row, row],
        out_specs=tuple(tok(n) for n in widths),
        scratch_shapes=[pltpu.VMEM((tm + CONV_HALO, CONV_DIM), F32)],
        compiler_params=_params(("parallel", "arbitrary")),
        name="norm_inproj_conv",
    )(x, g, sc, sh, w_rkv, w_lora, w_conv, w_mla, wpad,
      conv_b.reshape(1, -1), ln_g.reshape(1, -1), ln_b.reshape(1, -1))


def _block_diag(y):
    lane = _iota(y.shape, 1)
    zero = jnp.zeros_like(y)
    return jnp.concatenate([jnp.where(lane < RWKV_HEAD, y, zero),
                            jnp.where(lane >= RWKV_HEAD, y, zero)], axis=0)


def _packed_mm(x, ys, passes=3):
    if passes == 1:
        lhs = x.astype(BF16)
        cols = [_block_diag(y.astype(BF16)) for y in ys]
    elif passes == 2:
        lhs = jnp.concatenate(_split(x), axis=1)
        cols = [jnp.concatenate([_block_diag(y.astype(BF16))] * 2, axis=0) for y in ys]
    else:
        xh, xl = _split(x)
        lhs = jnp.concatenate([xh, xl, xh], axis=1)
        cols = []
        for y in ys:
            yh, yl = _split(y)
            cols.append(jnp.concatenate([_block_diag(yh), _block_diag(yh), _block_diag(yl)], axis=0))
    out = _dot(lhs, cols[0] if len(cols) == 1 else jnp.concatenate(cols, axis=1))
    return [out[:, i * LANE:(i + 1) * LANE] for i in range(len(ys))]


def _head_sum_matrix(scale):
    r = _iota((LANE, LANE), 0) // RWKV_HEAD
    c = _iota((LANE, LANE), 1) // RWKV_HEAD
    return jnp.where(r == c, scale, 0.0).astype(BF16)


def _rwkv_kernel(*refs, has_vres, tt):
    if has_vres:
        (prkv_ref, plora_ref, vfirst_ref, mu_rkv, mu_lora, wc_ref, w0_ref, a0_ref, v0_ref,
         kk_ref, ka_ref, rk_ref, lng_ref, lnb_ref, y_ref,
         prev_rkv, prev_lora, h_scr, at_s, rt_s, bt_s, kt_s, bp_s, kp_s, v_s, pf_s, o_s) = refs
    else:
        (prkv_ref, plora_ref, mu_rkv, mu_lora, wc_ref, w0_ref, a0_ref,
         kk_ref, ka_ref, rk_ref, lng_ref, lnb_ref, y_ref, vout_ref,
         prev_rkv, prev_lora, h_scr, at_s, rt_s, bt_s, kt_s, bp_s, kp_s, v_s, pf_s, o_s) = refs
    n_chunks = tt // CHUNK

    @pl.when(pl.program_id(1) == 0)
    def _():
        prev_rkv[...] = jnp.zeros_like(prev_rkv)
        prev_lora[...] = jnp.zeros_like(prev_lora)
        h_scr[...] = jnp.zeros_like(h_scr)

    def shift_mix(p_ref, prev_ref, mu_ref):
        p = p_ref[...]
        row = _iota(p.shape, 0)
        prev = jnp.where(row == 0, prev_ref[0:1, :], pltpu.roll(p, 1, 0))
        prev_ref[0:1, :] = p[tt - 1:tt, :]
        return p + (prev - p) * mu_ref[...]

    xs = shift_mix(prkv_ref, prev_rkv, mu_rkv)
    xl = shift_mix(plora_ref, prev_lora, mu_lora)
    r = xs[:, 0:RWKV_DIM]
    k = xs[:, RWKV_DIM:2 * RWKV_DIM]
    v = xs[:, 2 * RWKV_DIM:3 * RWKV_DIM]

    lane = _iota(xl.shape, 1)
    o_w, o_a, o_g = DECAY_LORA, DECAY_LORA + ICLR_LORA, DECAY_LORA + ICLR_LORA + GATE_LORA
    act = jnp.where(lane < o_w, jnp.tanh(xl),
                    jnp.where((lane >= o_a) & (lane < o_g), _sigmoid(xl), xl))
    act_h, act_l = _split(act)
    lora = _dot(jnp.concatenate([act_h, act_l, act_h], axis=1), wc_ref[...])
    z = w0_ref[...] + lora[:, 0:RWKV_DIM]
    lw = -RWKV_DECAY_SCALE * _sigmoid(z)
    a_ic = _sigmoid(a0_ref[...] + lora[:, RWKV_DIM:2 * RWKV_DIM])
    gate = lora[:, 2 * RWKV_DIM:3 * RWKV_DIM]
    if has_vres:
        v = v + (vfirst_ref[...] - v) * _sigmoid(v0_ref[...] + lora[:, 3 * RWKV_DIM:4 * RWKV_DIM])
    else:
        vout_ref[...] = v

    ones_bd = _head_sum_matrix(1.0)
    mean_bd = _head_sum_matrix(1.0 / RWKV_HEAD)

    def per_head(x, mat, split=False):
        mm = _dot2_exact_rhs if split else (lambda a, b: _dot(a.astype(BF16), b))
        return jnp.concatenate(
            [mm(x[:, p * LANE:(p + 1) * LANE], mat) for p in range(RWKV_PAIRS)], axis=1)

    kk = k * kk_ref[...]
    kk = kk * lax.rsqrt(jnp.maximum(per_head(kk * kk, ones_bd), 1e-24))
    k2 = k * (1.0 + (a_ic - 1.0) * ka_ref[...])
    a_vec = -kk
    b_vec = kk * a_ic
    bonus = per_head(r * k2 * rk_ref[...], ones_bd) * v

    tr = _iota((tt, tt), 0)
    tc = _iota((tt, tt), 1)
    tri = jnp.where((tr // CHUNK == tc // CHUNK) & (tc <= tr), 1.0, 0.0).astype(BF16)
    cum = _dot2_exact_rhs_left(tri, lw)
    cum3 = cum.reshape(n_chunks, CHUNK, RWKV_DIM)
    cum_end = jnp.broadcast_to(cum3[:, CHUNK - 1:CHUNK, :], cum3.shape).reshape(tt, RWKV_DIM)
    pf = jnp.exp(cum)
    pinv = jnp.exp(-cum)
    pprev = jnp.exp(cum - lw)
    pend = jnp.exp(cum_end - cum)

    def put(dst, val):
        for p in range(RWKV_PAIRS):
            dst[p] = val[:, p * LANE:(p + 1) * LANE]

    put(at_s, a_vec * pprev)
    put(rt_s, r * pf)
    put(bt_s, b_vec * pinv)
    put(kt_s, k2 * pinv)
    put(bp_s, b_vec * pend)
    put(kp_s, k2 * pend)
    put(v_s, v)
    put(pf_s, pf)

    t_loc = _iota((CHUNK, LANE), 0)
    s_loc = _iota((CHUNK, LANE), 1) % RWKV_HEAD
    strict = s_loc < t_loc
    incl = s_loc <= t_loc
    eye_p = jnp.where(s_loc == t_loc, 1.0, 0.0)
    same_block = lambda n: (t_loc // n) == (s_loc // n)
    in_base = same_block(RWKV_INV_BASE)
    merge_masks = []
    n = RWKV_INV_BASE
    while n < CHUNK:
        merge_masks.append(same_block(2 * n) & jnp.logical_not(same_block(n)))
        n *= 2

    group = min(n_chunks, RWKV_CHUNKS_PER_ITER)

    def chunk_group(cg, carry):
        streams = []
        for ci in range(group):
            c = cg * group + ci
            rows = pl.ds(pl.multiple_of(c * CHUNK, CHUNK), CHUNK)
            last = pl.ds(c * CHUNK + CHUNK - 1, 1)
            streams += [(p, rows, last) for p in range(RWKV_PAIRS)]
        ns = range(len(streams))
        load = lambda ref: [ref[p, rows, :] for (p, rows, _) in streams]
        at, rt, bt, kt, bp, kp, vv = (load(r) for r in (at_s, rt_s, bt_s, kt_s, bp_s, kp_s, v_s))
        g = [_dot2_nt(jnp.concatenate([at[i], rt[i]], axis=0),
                      jnp.concatenate([_block_diag(bt[i]), _block_diag(kt[i])], axis=0)) for i in ns]
        aab = [jnp.where(strict, g[i][0:CHUNK, 0:LANE], 0.0) for i in ns]
        aak = [jnp.where(strict, g[i][0:CHUNK, LANE:2 * LANE], 0.0) for i in ns]
        arb = [jnp.where(incl, g[i][CHUNK:2 * CHUNK, 0:LANE], 0.0) for i in ns]
        ark = [jnp.where(incl, g[i][CHUNK:2 * CHUNK, LANE:2 * LANE], 0.0) for i in ns]
        n_diag = [jnp.where(in_base, aab[i], 0.0) for i in ns]
        inv, base = RWKV_INV_PASSES, RWKV_BASE_PASSES
        dpow = [_packed_mm(n_diag[i], [n_diag[i]], passes=base)[0] for i in ns]
        tinv = [eye_p + n_diag[i] for i in ns]
        akv = [_packed_mm(aak[i], [vv[i]], passes=1)[0] for i in ns]
        arkv = [_packed_mm(ark[i], [vv[i]], passes=1)[0] for i in ns]
        res = [_packed_mm(dpow[i], [dpow[i], tinv[i]], passes=base) for i in ns]
        tinv = [tinv[i] + res[i][1] for i in ns]
        tinv = [tinv[i] + _packed_mm(res[i][0], [tinv[i]], passes=base)[0] for i in ns]
        for level_mask in merge_masks:
            ct = [_packed_mm(jnp.where(level_mask, aab[i], 0.0), [tinv[i]], passes=inv)[0] for i in ns]
            tinv = [tinv[i] + _packed_mm(tinv[i], [ct[i]], passes=inv)[0] for i in ns]
        wu = [_packed_mm(tinv[i], [at[i], akv[i]], passes=inv) for i in ns]
        ab = [_packed_mm(arb[i], wu[i], passes=1) for i in ns]
        q_m = [rt[i] + ab[i][0] for i in ns]
        y0 = [ab[i][1] + arkv[i] for i in ns]
        upd = [_dot2(jnp.concatenate([bp[i], kp[i]], axis=0).T,
                     jnp.concatenate([jnp.concatenate(wu[i], axis=1),
                                      jnp.concatenate([jnp.zeros_like(vv[i]), vv[i]], axis=1)], axis=0))
               for i in ns]
        first_head = _iota((CHUNK, 2 * LANE), 1) % LANE < RWKV_HEAD
        upd = [jnp.where(first_head, upd[i][0:RWKV_HEAD], upd[i][RWKV_HEAD:2 * RWKV_HEAD]) for i in ns]
        h = [h_scr[p] for p in range(RWKV_PAIRS)]
        for i in ns:
            p, rows, last = streams[i]
            m_p = upd[i][:, 0:LANE] + jnp.where(s_loc == t_loc, pf_s[p, last, :], 0.0)
            (qh_mh,) = _packed_mm(jnp.concatenate([q_m[i], m_p], axis=0), [h[p]], passes=3)
            o_s[rows, p * LANE:(p + 1) * LANE] = qh_mh[0:CHUNK] + y0[i]
            h[p] = qh_mh[CHUNK:2 * CHUNK] + upd[i][:, LANE:2 * LANE]
        for p in range(RWKV_PAIRS):
            h_scr[p] = h[p]
        return carry

    lax.fori_loop(0, n_chunks // group, chunk_group, 0)

    o = o_s[...]
    mean = per_head(o, mean_bd, split=True)
    oc = o - mean
    var = per_head(oc * oc, mean_bd)
    on = oc * lax.rsqrt(var + RWKV_GN_EPS) * lng_ref[...] + lnb_ref[...]
    y_ref[...] = ((on + bonus) * gate).astype(y_ref.dtype)


def _dot2_exact_rhs_left(m_bf16, x):
    xh, xl = _split(x)
    return _dot(jnp.concatenate([m_bf16] * 2, axis=1), jnp.concatenate([xh, xl], axis=0))


def _rwkv_call(p_rkv, p_lora, v_first, prm):
    bsz, s, _ = p_rkv.shape
    tt = min(s, 256)
    has_vres = v_first is not None
    tok = lambda n: pl.BlockSpec((None, tt, n), lambda b, i: (b, i, 0))
    row = lambda n: _resident((1, n))
    in_specs = [tok(3 * RWKV_DIM), tok(LORA_PAD)]
    args = [p_rkv, p_lora]
    if has_vres:
        in_specs.append(tok(RWKV_DIM))
        args.append(v_first)
    in_specs += [row(3 * RWKV_DIM), row(LORA_PAD), _resident((3 * LORA_PAD, 4 * RWKV_DIM)),
                 row(RWKV_DIM), row(RWKV_DIM)]
    args += [prm["mu_rkv"], prm["mu_lora"], prm["w_lora_up"], prm["w0"], prm["a0"]]
    if has_vres:
        in_specs.append(row(RWKV_DIM))
        args.append(prm["v0"])
    in_specs += [row(RWKV_DIM)] * 5
    args += [prm["k_k"], prm["k_a"], prm["r_k"], prm["lnx_g"], prm["lnx_b"]]
    y_shape = jax.ShapeDtypeStruct((bsz, s, RWKV_DIM), BF16)
    if has_vres:
        out_shape, out_specs = y_shape, tok(RWKV_DIM)
    else:
        out_shape = (y_shape, jax.ShapeDtypeStruct((bsz, s, RWKV_DIM), F32))
        out_specs = (tok(RWKV_DIM), tok(RWKV_DIM))
    pair_tile = pltpu.VMEM((RWKV_PAIRS, tt, LANE), F32)
    out = pl.pallas_call(
        functools.partial(_rwkv_kernel, has_vres=has_vres, tt=tt),
        out_shape=out_shape,
        grid=(bsz, s // tt),
        in_specs=in_specs,
        out_specs=out_specs,
        scratch_shapes=[
            pltpu.VMEM((8, 3 * RWKV_DIM), F32),
            pltpu.VMEM((8, LORA_PAD), F32),
            pltpu.VMEM((RWKV_PAIRS, RWKV_HEAD, LANE), F32),
        ] + [pair_tile] * 8 + [pltpu.VMEM((tt, RWKV_DIM), F32)],
        compiler_params=_params(("parallel", "arbitrary")),
        name="rwkv7_mixer",
    )(*args)
    return out if has_vres else out


def _mla_prep_kernel(p_ref, cos_ref, sin_ref, qg_ref, kvg_ref, wq_ref, wqs_ref, wkv_ref,
                     q_ref, k_ref, v_ref):
    p = p_ref[...].astype(F32)
    cos = cos_ref[...]
    sin = sin_ref[...]
    qc = (_rms(p[:, 0:Q_LORA], NORM_EPS) * qg_ref[...]).astype(BF16)
    kvc = (_rms(p[:, Q_LORA:Q_LORA + KV_LORA], NORM_EPS) * kvg_ref[...]).astype(BF16)
    q = _dot(qc, wq_ref[...])
    qs = _dot(qc, wqs_ref[...])
    kv = _dot(kvc, wkv_ref[...])
    o_kr = Q_LORA + KV_LORA
    k_rope = (p[:, o_kr:o_kr + LANE] * cos + p[:, o_kr + LANE:o_kr + 2 * LANE] * sin).astype(BF16)
    for h in range(MLA_HEADS):
        b0 = h * QK_PAD
        q_ref[:, b0:b0 + LANE] = q[:, b0:b0 + LANE].astype(BF16)
        q_ref[:, b0 + LANE:b0 + 2 * LANE] = (
            q[:, b0 + LANE:b0 + 2 * LANE] * cos + qs[:, h * LANE:(h + 1) * LANE] * sin).astype(BF16)
        k_ref[:, b0:b0 + LANE] = kv[:, b0:b0 + LANE].astype(BF16)
        k_ref[:, b0 + LANE:b0 + 2 * LANE] = k_rope
        v_ref[h * V_HEAD:(h + 1) * V_HEAD, :] = kv[:, b0 + LANE:b0 + 2 * LANE].T.astype(BF16)


def _mla_prep_call(p_mla, cos, sin, qg, kvg, wq, wqs, wkv):
    bsz, s, _ = p_mla.shape
    tm = min(s, ATTN_TILE)
    tok = lambda n: pl.BlockSpec((None, tm, n), lambda b, i: (b, i, 0))
    return pl.pallas_call(
        _mla_prep_kernel,
        out_shape=(jax.ShapeDtypeStruct((bsz, s, MLA_HEADS * QK_PAD), BF16),
                   jax.ShapeDtypeStruct((bsz, s, MLA_HEADS * QK_PAD), BF16),
                   jax.ShapeDtypeStruct((bsz, s // tm, MLA_DIM, tm), BF16)),
        grid=(bsz, s // tm),
        in_specs=[tok(MLA_IN_PAD), tok(LANE), tok(LANE),
                  _resident((1, Q_LORA)), _resident((1, KV_LORA)),
                  _resident(wq.shape), _resident(wqs.shape), _resident(wkv.shape)],
        out_specs=(tok(MLA_HEADS * QK_PAD), tok(MLA_HEADS * QK_PAD),
                   pl.BlockSpec((None, None, MLA_DIM, tm), lambda b, i: (b, i, 0, 0))),
        compiler_params=_params(("parallel", "parallel")),
        name="mla_prep",
    )(p_mla, cos, sin, qg, kvg, wq, wqs, wkv)


def _attn_kernel(q_ref, k_ref, vt_ref, o_ref, *, tq, nh):
    i = pl.program_id(2)
    heads = range(nh)
    q = [q_ref[:, h * QK_PAD:(h + 1) * QK_PAD] for h in heads]

    def step(j, carry, masked):
        keys = pl.ds(pl.multiple_of(j * tq, tq), tq)
        st = [_dot_nt(k_ref[keys, h * QK_PAD:(h + 1) * QK_PAD], q[h]) for h in heads]
        if masked:
            visible = (_iota((tq, tq), 0) // CHUNK) <= (_iota((tq, tq), 1) // CHUNK)
            st = [jnp.where(visible, s, MASK_VALUE) for s in st]
        m_new = [jnp.maximum(carry[h][0], jnp.max(st[h], axis=0, keepdims=True)) for h in heads]
        alpha = [jnp.exp2(carry[h][0] - m_new[h]) for h in heads]
        pt = [jnp.exp2(st[h] - m_new[h]) for h in heads]
        l_new = [alpha[h] * carry[h][1] + jnp.sum(pt[h], axis=0, keepdims=True) for h in heads]
        pv = [_dot(vt_ref[j, h * V_HEAD:(h + 1) * V_HEAD, :], pt[h].astype(BF16)) for h in heads]
        return tuple((m_new[h], l_new[h], alpha[h] * carry[h][2] + pv[h]) for h in heads)

    init = (jnp.full((1, tq), MASK_VALUE, F32), jnp.zeros((1, tq), F32), jnp.zeros((V_HEAD, tq), F32))
    carry = lax.fori_loop(0, i, lambda j, c: step(j, c, False), (init,) * nh)
    carry = step(i, carry, True)
    for h in heads:
        o_ref[:, h * V_HEAD:(h + 1) * V_HEAD] = (carry[h][2] / carry[h][1]).T.astype(o_ref.dtype)


def _attn_call(q, k, vt):
    bsz, s, _ = q.shape
    tq = min(s, ATTN_TILE)
    nh = ATTN_HEADS_PER_STEP
    return pl.pallas_call(
        functools.partial(_attn_kernel, tq=tq, nh=nh),
        out_shape=jax.ShapeDtypeStruct((bsz, s, MLA_DIM), BF16),
        grid=(bsz, MLA_HEADS // nh, s // tq),
        in_specs=[pl.BlockSpec((None, tq, nh * QK_PAD), lambda b, h, i: (b, i, h)),
                  pl.BlockSpec((None, s, nh * QK_PAD), lambda b, h, i: (b, 0, h)),
                  pl.BlockSpec((None, s // tq, nh * V_HEAD, tq), lambda b, h, i: (b, 0, h, 0))],
        out_specs=pl.BlockSpec((None, tq, nh * V_HEAD), lambda b, h, i: (b, i, h)),
        compiler_params=_params(("parallel", "parallel", "arbitrary")),
        name="mla_attention",
    )(q, k, vt)


def _outproj_kernel(ya_ref, yb_ref, yc_ref, x_ref, wo_ref, gt_ref, g_ref, sc_ref, sh_ref,
                    x1_ref, h2_ref):
    o1, o2 = RWKV_DIM, RWKV_DIM + CONV_DIM
    half = x_ref.shape[0] // 2
    ys = []
    for r in (slice(0, half), slice(half, 2 * half)):
        ys.append(_dot(ya_ref[r, :], wo_ref[0:o1, :]) + _dot(yb_ref[r, :], wo_ref[o1:o2, :])
                  + _dot(yc_ref[r, :], wo_ref[o2:o2 + MLA_DIM, :]))
    for r, y in zip((slice(0, half), slice(half, 2 * half)), ys):
        x1 = x_ref[r, :] + gt_ref[...] * y
        x1_ref[r, :] = x1
        h2_ref[r, :] = (_rms(x1, NORM_EPS) * g_ref[...] * (1.0 + sc_ref[...]) + sh_ref[...]).astype(BF16)


def _outproj_call(ya, yb, yc, x, wo, gt, g, sc, sh):
    bsz, s, d = x.shape
    tm = min(s, 512)
    tok = lambda n: pl.BlockSpec((None, tm, n), lambda b, i: (b, i, 0))
    per_b = pl.BlockSpec((None, 1, d), lambda b, i: (b, 0, 0))
    return pl.pallas_call(
        _outproj_kernel,
        out_shape=(jax.ShapeDtypeStruct((bsz, s, d), F32), jax.ShapeDtypeStruct((bsz, s, d), BF16)),
        grid=(bsz, s // tm),
        in_specs=[tok(RWKV_DIM), tok(CONV_DIM), tok(MLA_DIM), tok(d), _resident(wo.shape),
                  per_b, _resident((1, d)), per_b, per_b],
        out_specs=(tok(d), tok(d)),
        compiler_params=_params(("parallel", "parallel")),
        name="outproj_norm2",
    )(ya, yb, yc, x, wo, gt, g, sc, sh)


def _mlp_kernel(h_ref, x_ref, w1_ref, w2_ref, gt_ref, fg_ref, o_ref, *, final):
    f = pl.program_id(2)

    @pl.when(f == 0)
    def _():
        o_ref[...] = jnp.zeros_like(o_ref)

    a = jnp.maximum(_dot(h_ref[...], w1_ref[...]), 0.0)
    o_ref[...] += _dot((a * a).astype(BF16), w2_ref[...])

    @pl.when(f == pl.num_programs(2) - 1)
    def _():
        xo = x_ref[...] + gt_ref[...] * o_ref[...]
        if final:
            xo = _rms(xo, NORM_EPS) * fg_ref[...]
        o_ref[...] = xo


def _mlp_call(h2, x1, w1, w2, gt, fg, final):
    bsz, s, d = x1.shape
    tm = min(s, 512)
    tf = 2048
    tok_map = lambda b, i, f: (b, i, 0)
    tok = lambda: pl.BlockSpec((None, tm, d), tok_map)
    return pl.pallas_call(
        functools.partial(_mlp_kernel, final=final),
        out_shape=jax.ShapeDtypeStruct((bsz, s, d), F32),
        grid=(bsz, s // tm, D_FF // tf),
        in_specs=[tok(), tok(),
                  pl.BlockSpec((d, tf), lambda b, i, f: (0, f)),
                  pl.BlockSpec((tf, d), lambda b, i, f: (f, 0)),
                  pl.BlockSpec((None, 1, d), lambda b, i, f: (b, 0, 0)),
                  pl.BlockSpec((1, d), lambda b, i, f: (0, 0))],
        out_specs=tok(),
        compiler_params=_params(("parallel", "parallel", "arbitrary"), MLP_VMEM_LIMIT),
        name="relu2_mlp",
    )(h2, x1, w1, w2, gt, fg)


def _pad_cols(w, n):
    return jnp.pad(w, ((0, 0), (0, n - w.shape[1])))


def _rotate_half_cols(w):
    half = w.shape[-1] // 2
    return jnp.concatenate([-w[..., half:], w[..., :half]], axis=-1)


def _layer_weights(l, w_in, w_in_vres, rwkv_mu, vres_mu, decay_up, iclr_up, gate_up, vres_up,
                   w_qb, w_kvb):
    d = D_MODEL
    wl = w_in[l]
    o_lora = 3 * RWKV_DIM
    o_conv = RWKV_IN
    o_mla = RWKV_IN + CONV_IN
    has_vres = l > 0
    lora_cols = [wl[:, o_lora:o_conv]]
    mu_cols = [rwkv_mu[l, o_lora:o_conv]]
    if has_vres:
        lora_cols.append(w_in_vres[l - 1])
        mu_cols.append(vres_mu[l - 1])
    w_lora = _pad_cols(jnp.concatenate(lora_cols, axis=1), LORA_PAD)
    mu_lora = jnp.pad(jnp.concatenate(mu_cols), (0, LORA_PAD - sum(m.shape[0] for m in mu_cols)))
    mla = wl[:, o_mla:N_IN]
    kr = mla[:, Q_LORA + KV_LORA:]
    zpad = jnp.zeros((d, LANE - QK_ROPE), F32)
    w_mla = jnp.concatenate([mla[:, :Q_LORA + KV_LORA], kr, zpad, _rotate_half_cols(kr), zpad], axis=1)

    up = jnp.zeros((LORA_PAD, 4 * RWKV_DIM), F32)
    o_a, o_g = DECAY_LORA, DECAY_LORA + ICLR_LORA
    o_v = o_g + GATE_LORA
    up = up.at[0:o_a, 0:RWKV_DIM].set(decay_up[l])
    up = up.at[o_a:o_g, RWKV_DIM:2 * RWKV_DIM].set(iclr_up[l])
    up = up.at[o_g:o_v, 2 * RWKV_DIM:3 * RWKV_DIM].set(gate_up[l])
    if has_vres:
        up = up.at[o_v:o_v + VRES_LORA, 3 * RWKV_DIM:].set(vres_up[l - 1])
    up_hi = up.astype(BF16)

    scale = (QK_NOPE + QK_ROPE) ** -0.5 * LOG2_E
    wq = (w_qb[l] * scale).reshape(Q_LORA, MLA_HEADS, QK_NOPE + QK_ROPE)
    nope, rope = wq[:, :, :QK_NOPE], wq[:, :, QK_NOPE:]
    z = jnp.zeros((Q_LORA, MLA_HEADS, LANE - QK_ROPE), F32)
    wq_main = jnp.concatenate([nope, rope, z], axis=2).reshape(Q_LORA, MLA_HEADS * QK_PAD)
    wq_rot = jnp.concatenate([_rotate_half_cols(rope), z], axis=2).reshape(Q_LORA, MLA_HEADS * LANE)
    return dict(
        w_rkv=wl[:, :o_lora].astype(BF16), w_lora=w_lora.astype(BF16),
        w_conv=wl[:, o_conv:o_mla].astype(BF16), w_mla=w_mla.astype(BF16),
        mu_rkv=rwkv_mu[l, :o_lora].reshape(1, -1), mu_lora=mu_lora.reshape(1, -1),
        w_lora_up=jnp.concatenate([up_hi, up_hi, (up - up_hi.astype(F32)).astype(BF16)], axis=0),
        wq=wq_main.astype(BF16), wq_rot=wq_rot.astype(BF16),
        wkv=w_kvb[l].astype(BF16))


def kernel(x, c, positions, ada_w, ada_b, norm1_g, norm2_g, final_g, w_in, w_in_vres, rwkv_mu,
           vres_mu, decay_w0, decay_up, iclr_a0, iclr_up, gate_up, vres_v0, vres_up, k_k, k_a, r_k,
           lnx_g, lnx_b, conv_w, conv_b, conv_ln_g, conv_ln_b, q_a_norm_g, w_qb, kv_a_norm_g, w_kvb,
           w_out, mlp_w1, mlp_w2):
    bsz, s, d = x.shape
    mod = _ada_call(c, ada_w, ada_b).reshape(DEPTH, bsz, 6, 1, d)
    cos, sin = _rope_call(positions)
    row = lambda a: a.reshape(1, -1)
    v_first = None
    for l in range(DEPTH):
        sh1, sc1, gt1, sh2, sc2, gt2 = (mod[l, :, j] for j in range(6))
        lw = _layer_weights(l, w_in, w_in_vres, rwkv_mu, vres_mu, decay_up, iclr_up, gate_up,
                            vres_up, w_qb, w_kvb)
        p_rkv, p_lora, y_b, p_mla = _inproj_call(
            x, row(norm1_g[l]), sc1, sh1, lw["w_rkv"], lw["w_lora"], lw["w_conv"], lw["w_mla"],
            conv_w[l], conv_b[l], conv_ln_g[l], conv_ln_b[l])
        prm = dict(mu_rkv=lw["mu_rkv"], mu_lora=lw["mu_lora"], w_lora_up=lw["w_lora_up"],
                   w0=row(decay_w0[l]), a0=row(iclr_a0[l]), k_k=row(k_k[l]), k_a=row(k_a[l]),
                   r_k=row(r_k[l]), lnx_g=row(lnx_g[l]), lnx_b=row(lnx_b[l]))
        if l == 0:
            y_a, v_first = _rwkv_call(p_rkv, p_lora, None, prm)
        else:
            prm["v0"] = row(vres_v0[l - 1])
            y_a = _rwkv_call(p_rkv, p_lora, v_first, prm)
        q, k, v = _mla_prep_call(p_mla, cos, sin, row(q_a_norm_g[l]), row(kv_a_norm_g[l]),
                                 lw["wq"], lw["wq_rot"], lw["wkv"])
        y_c = _attn_call(q, k, v)
        x1, h2 = _outproj_call(y_a, y_b, y_c, x, w_out[l].astype(BF16), gt1,
                               row(norm2_g[l]), sc2, sh2)
        x = _mlp_call(h2, x1, mlp_w1[l].astype(BF16), mlp_w2[l].astype(BF16), gt2,
                      row(final_g), final=(l == DEPTH - 1))
    return x
```

```python
import functools

import jax
import jax.numpy as jnp
from jax import lax
from jax.experimental import pallas as pl
from jax.experimental.pallas import tpu as pltpu

F32 = jnp.float32
BF16 = jnp.bfloat16

D_MODEL = 2048
DEPTH = 2
CHUNK = 64
NORM_EPS = 1e-6
LN_EPS = 1e-5
D_FF = 4 * D_MODEL

RWKV_HEAD = 64
RWKV_DIM = 512
RWKV_HEADS = 8
RWKV_PAIRS = RWKV_HEADS // 2
DECAY_LORA = 32
ICLR_LORA = 32
VRES_LORA = 32
GATE_LORA = 96
RWKV_GN_EPS = 64e-5
LORA_PAD = 256
RWKV_DECAY_SCALE = 0.6065306597126334
RWKV_INV_BASE = 8
RWKV_INV_PASSES = 3
RWKV_BASE_PASSES = 2
RWKV_CHUNKS_PER_ITER = 4

CONV_DIM = 512
CONV_WIDTH = 31
CONV_HALO = 32

MLA_DIM = 1024
V_HEAD = 128
MLA_HEADS = 8
QK_NOPE = 128
QK_ROPE = 64
Q_LORA = 512
KV_LORA = 256
ROPE_THETA = 10000.0
QK_PAD = 256
ATTN_HEADS_PER_STEP = 4
ATTN_TILE = 512
MLA_IN_PAD = 1024

RWKV_IN = 3 * RWKV_DIM + DECAY_LORA + ICLR_LORA + GATE_LORA
CONV_IN = 2 * CONV_DIM
N_IN = RWKV_IN + CONV_IN + Q_LORA + KV_LORA + QK_ROPE

LANE = 128
V7X_VMEM_LIMIT = 56 * 1024 * 1024
MLP_VMEM_LIMIT = 60 * 1024 * 1024
MASK_VALUE = -1e30
LOG2_E = 1.4426950408889634


def _params(semantics, vmem=V7X_VMEM_LIMIT):
    return pltpu.CompilerParams(dimension_semantics=semantics, vmem_limit_bytes=vmem)


def _resident(shape):
    nd = len(shape)
    return pl.BlockSpec(shape, lambda *_: (0,) * nd, pipeline_mode=pl.Buffered(1))


def _dot(a, b):
    return jnp.dot(a, b, preferred_element_type=F32)


def _dot_nt(a, b):
    return lax.dot_general(a, b, (((1,), (1,)), ((), ())), preferred_element_type=F32)


def _split(x):
    hi = x.astype(BF16)
    lo = (x - hi.astype(F32)).astype(BF16)
    return hi, lo


def _dot3(a, b):
    ah, al = _split(a)
    bh, bl = _split(b)
    return _dot(jnp.concatenate([ah, al, ah], axis=1), jnp.concatenate([bh, bh, bl], axis=0))


def _dot2(a, b):
    bh = b.astype(BF16)
    return _dot(jnp.concatenate(_split(a), axis=1), jnp.concatenate([bh, bh], axis=0))


def _dot2_nt(a, b):
    bh = b.astype(BF16)
    return _dot_nt(jnp.concatenate(_split(a), axis=1), jnp.concatenate([bh, bh], axis=1))


def _dot2_exact_rhs(a, b_bf16):
    ah, al = _split(a)
    return _dot(jnp.concatenate([ah, al], axis=1), jnp.concatenate([b_bf16] * 2, axis=0))


def _rms(x, eps):
    return x * lax.rsqrt(jnp.mean(x * x, axis=-1, keepdims=True) + eps)


def _sigmoid(x):
    return 0.5 * jnp.tanh(0.5 * x) + 0.5


def _iota(shape, dim):
    return lax.broadcasted_iota(jnp.int32, shape, dim)


def _ada_kernel(c_ref, w_ref, b_ref, o_ref):
    c = c_ref[...]
    o_ref[...] = _dot3(c * _sigmoid(c), w_ref[...]) + b_ref[...]


def _ada_call(c, ada_w, ada_b):
    depth, d, n = ada_w.shape
    bsz = c.shape[0]
    tn = 512
    return pl.pallas_call(
        _ada_kernel,
        out_shape=jax.ShapeDtypeStruct((depth, bsz, n), F32),
        grid=(depth, n // tn),
        in_specs=[
            pl.BlockSpec((bsz, d), lambda l, j: (0, 0)),
            pl.BlockSpec((None, d, tn), lambda l, j: (l, 0, j)),
            pl.BlockSpec((None, 1, tn), lambda l, j: (l, 0, j)),
        ],
        out_specs=pl.BlockSpec((None, bsz, tn), lambda l, j: (l, 0, j)),
        compiler_params=_params(("parallel", "parallel")),
        name="ada_mod",
    )(c, ada_w, ada_b.reshape(depth, 1, n))


def _rope_kernel(pos_ref, invf_ref, cos_ref, sin_ref):
    ang = pos_ref[...] * invf_ref[...]
    cos_ref[...] = jnp.cos(ang)
    sin_ref[...] = jnp.sin(ang)


def _rope_call(positions):
    bsz, s = positions.shape
    ts = min(s, 1024)
    inv_freq = ROPE_THETA ** (-jnp.arange(0, QK_ROPE, 2, dtype=F32) / QK_ROPE)
    invf = jnp.tile(inv_freq, LANE // (QK_ROPE // 2)).reshape(1, LANE)
    pos = positions.astype(F32).reshape(bsz, s, 1)
    shp = jax.ShapeDtypeStruct((bsz, s, LANE), F32)
    return pl.pallas_call(
        _rope_kernel,
        out_shape=(shp, shp),
        grid=(bsz, s // ts),
        in_specs=[
            pl.BlockSpec((None, ts, 1), lambda b, i: (b, i, 0)),
            pl.BlockSpec((1, LANE), lambda b, i: (0, 0)),
        ],
        out_specs=(
            pl.BlockSpec((None, ts, LANE), lambda b, i: (b, i, 0)),
            pl.BlockSpec((None, ts, LANE), lambda b, i: (b, i, 0)),
        ),
        compiler_params=_params(("parallel", "parallel")),
        name="rope_tables",
    )(pos, invf)


def _conv_mixer_tile(p, w_ref, b_ref, g_ref, be_ref, ubuf, tt):
    ubuf[CONV_HALO:CONV_HALO + tt, :] = p[:, 0:CONV_DIM] * _sigmoid(p[:, CONV_DIM:2 * CONV_DIM])
    sub = 8
    base = CONV_HALO - sub
    ext = tt + sub
    acc = None
    for b in range(sub):
        part = None
        for a in range((CONV_WIDTH - 1 - b) // sub + 1):
            j = CONV_WIDTH - 1 - (sub * a + b)
            term = ubuf[base - sub * a:base - sub * a + ext, :] * w_ref[j:j + 1, :]
            part = term if part is None else part + term
        if b:
            part = pltpu.roll(part, b, 0)
        acc = part if acc is None else acc + part
    acc = acc[sub:sub + tt, :] + b_ref[...]
    ubuf[0:CONV_HALO, :] = ubuf[tt:tt + CONV_HALO, :]
    mean = jnp.mean(acc, axis=-1, keepdims=True)
    xc = acc - mean
    var = jnp.mean(xc * xc, axis=-1, keepdims=True)
    u = xc * lax.rsqrt(var + LN_EPS) * g_ref[...] + be_ref[...]
    return u * _sigmoid(u)


def _inproj_kernel(x_ref, g_ref, sc_ref, sh_ref, w_rkv, w_lora, w_conv, w_mla,
                   cw_ref, cb_ref, cg_ref, cbe_ref, o_rkv, o_lora, o_yb, o_mla, ubuf):
    tm = x_ref.shape[0]

    @pl.when(pl.program_id(1) == 0)
    def _():
        ubuf[0:CONV_HALO, :] = jnp.zeros((CONV_HALO, CONV_DIM), F32)

    x = x_ref[...]
    h = _rms(x, NORM_EPS) * g_ref[...] * (1.0 + sc_ref[...]) + sh_ref[...]
    hb = h.astype(BF16)
    p_conv = _dot(hb, w_conv[...])
    o_rkv[...] = _dot(hb, w_rkv[...])
    o_lora[...] = _dot(hb, w_lora[...])
    o_mla[...] = _dot(hb, w_mla[...]).astype(o_mla.dtype)
    o_yb[...] = _conv_mixer_tile(p_conv, cw_ref, cb_ref, cg_ref, cbe_ref, ubuf, tm).astype(o_yb.dtype)


def _inproj_call(x, g, sc, sh, w_rkv, w_lora, w_conv, w_mla, conv_w, conv_b, ln_g, ln_b):
    bsz, s, d = x.shape
    tm = min(s, 512)
    widths = (w_rkv.shape[1], w_lora.shape[1], CONV_DIM, w_mla.shape[1])
    tok = lambda n: pl.BlockSpec((None, tm, n), lambda b, i: (b, i, 0))
    per_b = pl.BlockSpec((None, 1, d), lambda b, i: (b, 0, 0))
    wpad = jnp.zeros((CONV_HALO, CONV_DIM), F32).at[:CONV_WIDTH].set(conv_w)
    row = _resident((1, CONV_DIM))
    return pl.pallas_call(
        _inproj_kernel,
        out_shape=tuple(jax.ShapeDtypeStruct((bsz, s, n), dt)
                        for n, dt in zip(widths, (F32, F32, BF16, BF16))),
        grid=(bsz, s // tm),
        in_specs=[tok(d), _resident((1, d)), per_b, per_b,
                  _resident(w_rkv.shape), _resident(w_lora.shape),
                  _resident(w_conv.shape), _resident(w_mla.shape),
                  _resident((CONV_HALO, CONV_DIM)), row, row, row],
        out_specs=tuple(tok(n) for n in widths),
        scratch_shapes=[pltpu.VMEM((tm + CONV_HALO, CONV_DIM), F32)],
        compiler_params=_params(("parallel", "arbitrary")),
        name="norm_inproj_conv",
    )(x, g, sc, sh, w_rkv, w_lora, w_conv, w_mla, wpad,
      conv_b.reshape(1, -1), ln_g.reshape(1, -1), ln_b.reshape(1, -1))


def _block_diag(y):
    lane = _iota(y.shape, 1)
    zero = jnp.zeros_like(y)
    return jnp.concatenate([jnp.where(lane < RWKV_HEAD, y, zero),
                            jnp.where(lane >= RWKV_HEAD, y, zero)], axis=0)


def _packed_mm(x, ys, passes=3):
    if passes == 1:
        lhs = x.astype(BF16)
        cols = [_block_diag(y.astype(BF16)) for y in ys]
    elif passes == 2:
        lhs = jnp.concatenate(_split(x), axis=1)
        cols = [jnp.concatenate([_block_diag(y.astype(BF16))] * 2, axis=0) for y in ys]
    else:
        xh, xl = _split(x)
        lhs = jnp.concatenate([xh, xl, xh], axis=1)
        cols = []
        for y in ys:
            yh, yl = _split(y)
            cols.append(jnp.concatenate([_block_diag(yh), _block_diag(yh), _block_diag(yl)], axis=0))
    out = _dot(lhs, cols[0] if len(cols) == 1 else jnp.concatenate(cols, axis=1))
    return [out[:, i * LANE:(i + 1) * LANE] for i in range(len(ys))]


def _head_sum_matrix(scale):
    r = _iota((LANE, LANE), 0) // RWKV_HEAD
    c = _iota((LANE, LANE), 1) // RWKV_HEAD
    return jnp.where(r == c, scale, 0.0).astype(BF16)


def _rwkv_kernel(*refs, has_vres, tt):
    if has_vres:
        (prkv_ref, plora_ref, vfirst_ref, mu_rkv, mu_lora, wc_ref, w0_ref, a0_ref, v0_ref,
         kk_ref, ka_ref, rk_ref, lng_ref, lnb_ref, y_ref,
         prev_rkv, prev_lora, h_scr, at_s, rt_s, bt_s, kt_s, bp_s, kp_s, v_s, pf_s, o_s) = refs
    else:
        (prkv_ref, plora_ref, mu_rkv, mu_lora, wc_ref, w0_ref, a0_ref,
         kk_ref, ka_ref, rk_ref, lng_ref, lnb_ref, y_ref, vout_ref,
         prev_rkv, prev_lora, h_scr, at_s, rt_s, bt_s, kt_s, bp_s, kp_s, v_s, pf_s, o_s) = refs
    n_chunks = tt // CHUNK

    @pl.when(pl.program_id(1) == 0)
    def _():
        prev_rkv[...] = jnp.zeros_like(prev_rkv)
        prev_lora[...] = jnp.zeros_like(prev_lora)
        h_scr[...] = jnp.zeros_like(h_scr)

    def shift_mix(p_ref, prev_ref, mu_ref):
        p = p_ref[...]
        row = _iota(p.shape, 0)
        prev = jnp.where(row == 0, prev_ref[0:1, :], pltpu.roll(p, 1, 0))
        prev_ref[0:1, :] = p[tt - 1:tt, :]
        return p + (prev - p) * mu_ref[...]

    xs = shift_mix(prkv_ref, prev_rkv, mu_rkv)
    xl = shift_mix(plora_ref, prev_lora, mu_lora)
    r = xs[:, 0:RWKV_DIM]
    k = xs[:, RWKV_DIM:2 * RWKV_DIM]
    v = xs[:, 2 * RWKV_DIM:3 * RWKV_DIM]

    lane = _iota(xl.shape, 1)
    o_w, o_a, o_g = DECAY_LORA, DECAY_LORA + ICLR_LORA, DECAY_LORA + ICLR_LORA + GATE_LORA
    act = jnp.where(lane < o_w, jnp.tanh(xl),
                    jnp.where((lane >= o_a) & (lane < o_g), _sigmoid(xl), xl))
    act_h, act_l = _split(act)
    lora = _dot(jnp.concatenate([act_h, act_l, act_h], axis=1), wc_ref[...])
    z = w0_ref[...] + lora[:, 0:RWKV_DIM]
    lw = -RWKV_DECAY_SCALE * _sigmoid(z)
    a_ic = _sigmoid(a0_ref[...] + lora[:, RWKV_DIM:2 * RWKV_DIM])
    gate = lora[:, 2 * RWKV_DIM:3 * RWKV_DIM]
    if has_vres:
        v = v + (vfirst_ref[...] - v) * _sigmoid(v0_ref[...] + lora[:, 3 * RWKV_DIM:4 * RWKV_DIM])
    else:
        vout_ref[...] = v

    ones_bd = _head_sum_matrix(1.0)
    mean_bd = _head_sum_matrix(1.0 / RWKV_HEAD)

    def per_head(x, mat, split=False):
        mm = _dot2_exact_rhs if split else (lambda a, b: _dot(a.astype(BF16), b))
        return jnp.concatenate(
            [mm(x[:, p * LANE:(p + 1) * LANE], mat) for p in range(RWKV_PAIRS)], axis=1)

    kk = k * kk_ref[...]
    kk = kk * lax.rsqrt(jnp.maximum(per_head(kk * kk, ones_bd), 1e-24))
    k2 = k * (1.0 + (a_ic - 1.0) * ka_ref[...])
    a_vec = -kk
    b_vec = kk * a_ic
    bonus = per_head(r * k2 * rk_ref[...], ones_bd) * v

    tr = _iota((tt, tt), 0)
    tc = _iota((tt, tt), 1)
    tri = jnp.where((tr // CHUNK == tc // CHUNK) & (tc <= tr), 1.0, 0.0).astype(BF16)
    cum = _dot2_exact_rhs_left(tri, lw)
    cum3 = cum.reshape(n_chunks, CHUNK, RWKV_DIM)
    cum_end = jnp.broadcast_to(cum3[:, CHUNK - 1:CHUNK, :], cum3.shape).reshape(tt, RWKV_DIM)
    pf = jnp.exp(cum)
    pinv = jnp.exp(-cum)
    pprev = jnp.exp(cum - lw)
    pend = jnp.exp(cum_end - cum)

    def put(dst, val):
        for p in range(RWKV_PAIRS):
            dst[p] = val[:, p * LANE:(p + 1) * LANE]

    put(at_s, a_vec * pprev)
    put(rt_s, r * pf)
    put(bt_s, b_vec * pinv)
    put(kt_s, k2 * pinv)
    put(bp_s, b_vec * pend)
    put(kp_s, k2 * pend)
    put(v_s, v)
    put(pf_s, pf)

    t_loc = _iota((CHUNK, LANE), 0)
    s_loc = _iota((CHUNK, LANE), 1) % RWKV_HEAD
    strict = s_loc < t_loc
    incl = s_loc <= t_loc
    eye_p = jnp.where(s_loc == t_loc, 1.0, 0.0)
    same_block = lambda n: (t_loc // n) == (s_loc // n)
    in_base = same_block(RWKV_INV_BASE)
    merge_masks = []
    n = RWKV_INV_BASE
    while n < CHUNK:
        merge_masks.append(same_block(2 * n) & jnp.logical_not(same_block(n)))
        n *= 2

    group = min(n_chunks, RWKV_CHUNKS_PER_ITER)

    def chunk_group(cg, carry):
        streams = []
        for ci in range(group):
            c = cg * group + ci
            rows = pl.ds(pl.multiple_of(c * CHUNK, CHUNK), CHUNK)
            last = pl.ds(c * CHUNK + CHUNK - 1, 1)
            streams += [(p, rows, last) for p in range(RWKV_PAIRS)]
        ns = range(len(streams))
        load = lambda ref: [ref[p, rows, :] for (p, rows, _) in streams]
        at, rt, bt, kt, bp, kp, vv = (load(r) for r in (at_s, rt_s, bt_s, kt_s, bp_s, kp_s, v_s))
        g = [_dot2_nt(jnp.concatenate([at[i], rt[i]], axis=0),
                      jnp.concatenate([_block_diag(bt[i]), _block_diag(kt[i])], axis=0)) for i in ns]
        aab = [jnp.where(strict, g[i][0:CHUNK, 0:LANE], 0.0) for i in ns]
        aak = [jnp.where(strict, g[i][0:CHUNK, LANE:2 * LANE], 0.0) for i in ns]
        arb = [jnp.where(incl, g[i][CHUNK:2 * CHUNK, 0:LANE], 0.0) for i in ns]
        ark = [jnp.where(incl, g[i][CHUNK:2 * CHUNK, LANE:2 * LANE], 0.0) for i in ns]
        n_diag = [jnp.where(in_base, aab[i], 0.0) for i in ns]
        inv, base = RWKV_INV_PASSES, RWKV_BASE_PASSES
        dpow = [_packed_mm(n_diag[i], [n_diag[i]], passes=base)[0] for i in ns]
        tinv = [eye_p + n_diag[i] for i in ns]
        akv = [_packed_mm(aak[i], [vv[i]], passes=1)[0] for i in ns]
        arkv = [_packed_mm(ark[i], [vv[i]], passes=1)[0] for i in ns]
        res = [_packed_mm(dpow[i], [dpow[i], tinv[i]], passes=base) for i in ns]
        tinv = [tinv[i] + res[i][1] for i in ns]
        tinv = [tinv[i] + _packed_mm(res[i][0], [tinv[i]], passes=base)[0] for i in ns]
        for level_mask in merge_masks:
            ct = [_packed_mm(jnp.where(level_mask, aab[i], 0.0), [tinv[i]], passes=inv)[0] for i in ns]
            tinv = [tinv[i] + _packed_mm(tinv[i], [ct[i]], passes=inv)[0] for i in ns]
        wu = [_packed_mm(tinv[i], [at[i], akv[i]], passes=inv) for i in ns]
        ab = [_packed_mm(arb[i], wu[i], passes=1) for i in ns]
        q_m = [rt[i] + ab[i][0] for i in ns]
        y0 = [ab[i][1] + arkv[i] for i in ns]
        upd = [_dot2(jnp.concatenate([bp[i], kp[i]], axis=0).T,
                     jnp.concatenate([jnp.concatenate(wu[i], axis=1),
                                      jnp.concatenate([jnp.zeros_like(vv[i]), vv[i]], axis=1)], axis=0))
               for i in ns]
        first_head = _iota((CHUNK, 2 * LANE), 1) % LANE < RWKV_HEAD
        upd = [jnp.where(first_head, upd[i][0:RWKV_HEAD], upd[i][RWKV_HEAD:2 * RWKV_HEAD]) for i in ns]
        h = [h_scr[p] for p in range(RWKV_PAIRS)]
        for i in ns:
            p, rows, last = streams[i]
            m_p = upd[i][:, 0:LANE] + jnp.where(s_loc == t_loc, pf_s[p, last, :], 0.0)
            (qh_mh,) = _packed_mm(jnp.concatenate([q_m[i], m_p], axis=0), [h[p]], passes=3)
            o_s[rows, p * LANE:(p + 1) * LANE] = qh_mh[0:CHUNK] + y0[i]
            h[p] = qh_mh[CHUNK:2 * CHUNK] + upd[i][:, LANE:2 * LANE]
        for p in range(RWKV_PAIRS):
            h_scr[p] = h[p]
        return carry

    lax.fori_loop(0, n_chunks // group, chunk_group, 0)

    o = o_s[...]
    mean = per_head(o, mean_bd, split=True)
    oc = o - mean
    var = per_head(oc * oc, mean_bd)
    on = oc * lax.rsqrt(var + RWKV_GN_EPS) * lng_ref[...] + lnb_ref[...]
    y_ref[...] = ((on + bonus) * gate).astype(y_ref.dtype)


def _dot2_exact_rhs_left(m_bf16, x):
    xh, xl = _split(x)
    return _dot(jnp.concatenate([m_bf16] * 2, axis=1), jnp.concatenate([xh, xl], axis=0))


def _rwkv_call(p_rkv, p_lora, v_first, prm):
    bsz, s, _ = p_rkv.shape
    tt = min(s, 256)
    has_vres = v_first is not None
    tok = lambda n: pl.BlockSpec((None, tt, n), lambda b, i: (b, i, 0))
    row = lambda n: _resident((1, n))
    in_specs = [tok(3 * RWKV_DIM), tok(LORA_PAD)]
    args = [p_rkv, p_lora]
    if has_vres:
        in_specs.append(tok(RWKV_DIM))
        args.append(v_first)
    in_specs += [row(3 * RWKV_DIM), row(LORA_PAD), _resident((3 * LORA_PAD, 4 * RWKV_DIM)),
                 row(RWKV_DIM), row(RWKV_DIM)]
    args += [prm["mu_rkv"], prm["mu_lora"], prm["w_lora_up"], prm["w0"], prm["a0"]]
    if has_vres:
        in_specs.append(row(RWKV_DIM))
        args.append(prm["v0"])
    in_specs += [row(RWKV_DIM)] * 5
    args += [prm["k_k"], prm["k_a"], prm["r_k"], prm["lnx_g"], prm["lnx_b"]]
    y_shape = jax.ShapeDtypeStruct((bsz, s, RWKV_DIM), BF16)
    if has_vres:
        out_shape, out_specs = y_shape, tok(RWKV_DIM)
    else:
        out_shape = (y_shape, jax.ShapeDtypeStruct((bsz, s, RWKV_DIM), F32))
        out_specs = (tok(RWKV_DIM), tok(RWKV_DIM))
    pair_tile = pltpu.VMEM((RWKV_PAIRS, tt, LANE), F32)
    out = pl.pallas_call(
        functools.partial(_rwkv_kernel, has_vres=has_vres, tt=tt),
        out_shape=out_shape,
        grid=(bsz, s // tt),
        in_specs=in_specs,
        out_specs=out_specs,
        scratch_shapes=[
            pltpu.VMEM((8, 3 * RWKV_DIM), F32),
            pltpu.VMEM((8, LORA_PAD), F32),
            pltpu.VMEM((RWKV_PAIRS, RWKV_HEAD, LANE), F32),
        ] + [pair_tile] * 8 + [pltpu.VMEM((tt, RWKV_DIM), F32)],
        compiler_params=_params(("parallel", "arbitrary")),
        name="rwkv7_mixer",
    )(*args)
    return out if has_vres else out


def _mla_prep_kernel(p_ref, cos_ref, sin_ref, qg_ref, kvg_ref, wq_ref, wqs_ref, wkv_ref,
                     q_ref, k_ref, v_ref):
    p = p_ref[...].astype(F32)
    cos = cos_ref[...]
    sin = sin_ref[...]
    qc = (_rms(p[:, 0:Q_LORA], NORM_EPS) * qg_ref[...]).astype(BF16)
    kvc = (_rms(p[:, Q_LORA:Q_LORA + KV_LORA], NORM_EPS) * kvg_ref[...]).astype(BF16)
    q = _dot(qc, wq_ref[...])
    qs = _dot(qc, wqs_ref[...])
    kv = _dot(kvc, wkv_ref[...])
    o_kr = Q_LORA + KV_LORA
    k_rope = (p[:, o_kr:o_kr + LANE] * cos + p[:, o_kr + LANE:o_kr + 2 * LANE] * sin).astype(BF16)
    for h in range(MLA_HEADS):
        b0 = h * QK_PAD
        q_ref[:, b0:b0 + LANE] = q[:, b0:b0 + LANE].astype(BF16)
        q_ref[:, b0 + LANE:b0 + 2 * LANE] = (
            q[:, b0 + LANE:b0 + 2 * LANE] * cos + qs[:, h * LANE:(h + 1) * LANE] * sin).astype(BF16)
        k_ref[:, b0:b0 + LANE] = kv[:, b0:b0 + LANE].astype(BF16)
        k_ref[:, b0 + LANE:b0 + 2 * LANE] = k_rope
        v_ref[h * V_HEAD:(h + 1) * V_HEAD, :] = kv[:, b0 + LANE:b0 + 2 * LANE].T.astype(BF16)


def _mla_prep_call(p_mla, cos, sin, qg, kvg, wq, wqs, wkv):
    bsz, s, _ = p_mla.shape
    tm = min(s, ATTN_TILE)
    tok = lambda n: pl.BlockSpec((None, tm, n), lambda b, i: (b, i, 0))
    return pl.pallas_call(
        _mla_prep_kernel,
        out_shape=(jax.ShapeDtypeStruct((bsz, s, MLA_HEADS * QK_PAD), BF16),
                   jax.ShapeDtypeStruct((bsz, s, MLA_HEADS * QK_PAD), BF16),
                   jax.ShapeDtypeStruct((bsz, s // tm, MLA_DIM, tm), BF16)),
        grid=(bsz, s // tm),
        in_specs=[tok(MLA_IN_PAD), tok(LANE), tok(LANE),
                  _resident((1, Q_LORA)), _resident((1, KV_LORA)),
                  _resident(wq.shape), _resident(wqs.shape), _resident(wkv.shape)],
        out_specs=(tok(MLA_HEADS * QK_PAD), tok(MLA_HEADS * QK_PAD),
                   pl.BlockSpec((None, None, MLA_DIM, tm), lambda b, i: (b, i, 0, 0))),
        compiler_params=_params(("parallel", "parallel")),
        name="mla_prep",
    )(p_mla, cos, sin, qg, kvg, wq, wqs, wkv)


def _attn_kernel(q_ref, k_ref, vt_ref, o_ref, *, tq, nh):
    i = pl.program_id(2)
    heads = range(nh)
    q = [q_ref[:, h * QK_PAD:(h + 1) * QK_PAD] for h in heads]

    def step(j, carry, masked):
        keys = pl.ds(pl.multiple_of(j * tq, tq), tq)
        st = [_dot_nt(k_ref[keys, h * QK_PAD:(h + 1) * QK_PAD], q[h]) for h in heads]
        if masked:
            visible = (_iota((tq, tq), 0) // CHUNK) <= (_iota((tq, tq), 1) // CHUNK)
            st = [jnp.where(visible, s, MASK_VALUE) for s in st]
        m_new = [jnp.maximum(carry[h][0], jnp.max(st[h], axis=0, keepdims=True)) for h in heads]
        alpha = [jnp.exp2(carry[h][0] - m_new[h]) for h in heads]
        pt = [jnp.exp2(st[h] - m_new[h]) for h in heads]
        l_new = [alpha[h] * carry[h][1] + jnp.sum(pt[h], axis=0, keepdims=True) for h in heads]
        pv = [_dot(vt_ref[j, h * V_HEAD:(h + 1) * V_HEAD, :], pt[h].astype(BF16)) for h in heads]
        return tuple((m_new[h], l_new[h], alpha[h] * carry[h][2] + pv[h]) for h in heads)

    init = (jnp.full((1, tq), MASK_VALUE, F32), jnp.zeros((1, tq), F32), jnp.zeros((V_HEAD, tq), F32))
    carry = lax.fori_loop(0, i, lambda j, c: step(j, c, False), (init,) * nh)
    carry = step(i, carry, True)
    for h in heads:
        o_ref[:, h * V_HEAD:(h + 1) * V_HEAD] = (carry[h][2] / carry[h][1]).T.astype(o_ref.dtype)


def _attn_call(q, k, vt):
    bsz, s, _ = q.shape
    tq = min(s, ATTN_TILE)
    nh = ATTN_HEADS_PER_STEP
    return pl.pallas_call(
        functools.partial(_attn_kernel, tq=tq, nh=nh),
        out_shape=jax.ShapeDtypeStruct((bsz, s, MLA_DIM), BF16),
        grid=(bsz, MLA_HEADS // nh, s // tq),
        in_specs=[pl.BlockSpec((None, tq, nh * QK_PAD), lambda b, h, i: (b, i, h)),
                  pl.BlockSpec((None, s, nh * QK_PAD), lambda b, h, i: (b, 0, h)),
                  pl.BlockSpec((None, s // tq, nh * V_HEAD, tq), lambda b, h, i: (b, 0, h, 0))],
        out_specs=pl.BlockSpec((None, tq, nh * V_HEAD), lambda b, h, i: (b, i, h)),
        compiler_params=_params(("parallel", "parallel", "arbitrary")),
        name="mla_attention",
    )(q, k, vt)


def _outproj_kernel(ya_ref, yb_ref, yc_ref, x_ref, wo_ref, gt_ref, g_ref, sc_ref, sh_ref,
                    x1_ref, h2_ref):
    o1, o2 = RWKV_DIM, RWKV_DIM + CONV_DIM
    half = x_ref.shape[0] // 2
    ys = []
    for r in (slice(0, half), slice(half, 2 * half)):
        ys.append(_dot(ya_ref[r, :], wo_ref[0:o1, :]) + _dot(yb_ref[r, :], wo_ref[o1:o2, :])
                  + _dot(yc_ref[r, :], wo_ref[o2:o2 + MLA_DIM, :]))
    for r, y in zip((slice(0, half), slice(half, 2 * half)), ys):
        x1 = x_ref[r, :] + gt_ref[...] * y
        x1_ref[r, :] = x1
        h2_ref[r, :] = (_rms(x1, NORM_EPS) * g_ref[...] * (1.0 + sc_ref[...]) + sh_ref[...]).astype(BF16)


def _outproj_call(ya, yb, yc, x, wo, layer, gt, g, sc, sh):
    bsz, s, d = x.shape
    tm = min(s, 512)
    tok = lambda n: pl.BlockSpec((None, tm, n), lambda b, i: (b, i, 0))
    per_b = pl.BlockSpec((None, 1, d), lambda b, i: (b, 0, 0))
    wo_spec = pl.BlockSpec((None,) + wo.shape[1:], lambda b, i: (layer, 0, 0),
                           pipeline_mode=pl.Buffered(1))
    return pl.pallas_call(
        _outproj_kernel,
        out_shape=(jax.ShapeDtypeStruct((bsz, s, d), F32), jax.ShapeDtypeStruct((bsz, s, d), BF16)),
        grid=(bsz, s // tm),
        in_specs=[tok(RWKV_DIM), tok(CONV_DIM), tok(MLA_DIM), tok(d), wo_spec,
                  per_b, _resident((1, d)), per_b, per_b],
        out_specs=(tok(d), tok(d)),
        compiler_params=_params(("parallel", "parallel")),
        name="outproj_norm2",
    )(ya, yb, yc, x, wo, gt, g, sc, sh)


def _mlp_kernel(h_ref, x_ref, w1_ref, w2_ref, gt_ref, fg_ref, o_ref, *, final):
    f = pl.program_id(2)

    @pl.when(f == 0)
    def _():
        o_ref[...] = jnp.zeros_like(o_ref)

    a = jnp.maximum(_dot(h_ref[...], w1_ref[...]), 0.0)
    o_ref[...] += _dot((a * a).astype(BF16), w2_ref[...])

    @pl.when(f == pl.num_programs(2) - 1)
    def _():
        xo = x_ref[...] + gt_ref[...] * o_ref[...]
        if final:
            xo = _rms(xo, NORM_EPS) * fg_ref[...]
        o_ref[...] = xo


def _mlp_call(h2, x1, w1, w2, layer, gt, fg, final):
    bsz, s, d = x1.shape
    tm = min(s, 512)
    tf = 2048
    tok_map = lambda b, i, f: (b, i, 0)
    tok = lambda: pl.BlockSpec((None, tm, d), tok_map)
    return pl.pallas_call(
        functools.partial(_mlp_kernel, final=final),
        out_shape=jax.ShapeDtypeStruct((bsz, s, d), F32),
        grid=(bsz, s // tm, D_FF // tf),
        in_specs=[tok(), tok(),
                  pl.BlockSpec((None, d, tf), lambda b, i, f: (layer, 0, f)),
                  pl.BlockSpec((None, tf, d), lambda b, i, f: (layer, f, 0)),
                  pl.BlockSpec((None, 1, d), lambda b, i, f: (b, 0, 0)),
                  pl.BlockSpec((1, d), lambda b, i, f: (0, 0))],
        out_specs=tok(),
        compiler_params=_params(("parallel", "parallel", "arbitrary"), MLP_VMEM_LIMIT),
        name="relu2_mlp",
    )(h2, x1, w1, w2, gt, fg)


def _pad_cols(w, n):
    return jnp.pad(w, ((0, 0), (0, n - w.shape[1])))


def _rotate_half_cols(w):
    half = w.shape[-1] // 2
    return jnp.concatenate([-w[..., half:], w[..., :half]], axis=-1)


def _layer_weights(l, w_in, w_in_vres, rwkv_mu, vres_mu, decay_up, iclr_up, gate_up, vres_up,
                   w_qb, w_kvb):
    d = D_MODEL
    wl = w_in[l]
    o_lora = 3 * RWKV_DIM
    o_conv = RWKV_IN
    o_mla = RWKV_IN + CONV_IN
    has_vres = l > 0
    lora_cols = [wl[:, o_lora:o_conv]]
    mu_cols = [rwkv_mu[l, o_lora:o_conv]]
    if has_vres:
        lora_cols.append(w_in_vres[l - 1])
        mu_cols.append(vres_mu[l - 1])
    w_lora = _pad_cols(jnp.concatenate(lora_cols, axis=1), LORA_PAD)
    mu_lora = jnp.pad(jnp.concatenate(mu_cols), (0, LORA_PAD - sum(m.shape[0] for m in mu_cols)))
    mla = wl[:, o_mla:N_IN]
    kr = mla[:, Q_LORA + KV_LORA:]
    zpad = jnp.zeros((d, LANE - QK_ROPE), F32)
    w_mla = jnp.concatenate([mla[:, :Q_LORA + KV_LORA], kr, zpad, _rotate_half_cols(kr), zpad], axis=1)

    up = jnp.zeros((LORA_PAD, 4 * RWKV_DIM), F32)
    o_a, o_g = DECAY_LORA, DECAY_LORA + ICLR_LORA
    o_v = o_g + GATE_LORA
    up = up.at[0:o_a, 0:RWKV_DIM].set(decay_up[l])
    up = up.at[o_a:o_g, RWKV_DIM:2 * RWKV_DIM].set(iclr_up[l])
    up = up.at[o_g:o_v, 2 * RWKV_DIM:3 * RWKV_DIM].set(gate_up[l])
    if has_vres:
        up = up.at[o_v:o_v + VRES_LORA, 3 * RWKV_DIM:].set(vres_up[l - 1])
    up_hi = up.astype(BF16)

    scale = (QK_NOPE + QK_ROPE) ** -0.5 * LOG2_E
    wq = (w_qb[l] * scale).reshape(Q_LORA, MLA_HEADS, QK_NOPE + QK_ROPE)
    nope, rope = wq[:, :, :QK_NOPE], wq[:, :, QK_NOPE:]
    z = jnp.zeros((Q_LORA, MLA_HEADS, LANE - QK_ROPE), F32)
    wq_main = jnp.concatenate([nope, rope, z], axis=2).reshape(Q_LORA, MLA_HEADS * QK_PAD)
    wq_rot = jnp.concatenate([_rotate_half_cols(rope), z], axis=2).reshape(Q_LORA, MLA_HEADS * LANE)
    return dict(
        w_rkv=wl[:, :o_lora].astype(BF16), w_lora=w_lora.astype(BF16),
        w_conv=wl[:, o_conv:o_mla].astype(BF16), w_mla=w_mla.astype(BF16),
        mu_rkv=rwkv_mu[l, :o_lora].reshape(1, -1), mu_lora=mu_lora.reshape(1, -1),
        w_lora_up=jnp.concatenate([up_hi, up_hi, (up - up_hi.astype(F32)).astype(BF16)], axis=0),
        wq=wq_main.astype(BF16), wq_rot=wq_rot.astype(BF16),
        wkv=w_kvb[l].astype(BF16))


def kernel(x, c, positions, ada_w, ada_b, norm1_g, norm2_g, final_g, w_in, w_in_vres, rwkv_mu,
           vres_mu, decay_w0, decay_up, iclr_a0, iclr_up, gate_up, vres_v0, vres_up, k_k, k_a, r_k,
           lnx_g, lnx_b, conv_w, conv_b, conv_ln_g, conv_ln_b, q_a_norm_g, w_qb, kv_a_norm_g, w_kvb,
           w_out, mlp_w1, mlp_w2):
    bsz, s, d = x.shape
    mod = _ada_call(c, ada_w, ada_b).reshape(DEPTH, bsz, 6, 1, d)
    cos, sin = _rope_call(positions)
    w1_bf16, w2_bf16, wo_bf16 = mlp_w1.astype(BF16), mlp_w2.astype(BF16), w_out.astype(BF16)
    row = lambda a: a.reshape(1, -1)
    v_first = None
    for l in range(DEPTH):
        sh1, sc1, gt1, sh2, sc2, gt2 = (mod[l, :, j] for j in range(6))
        lw = _layer_weights(l, w_in, w_in_vres, rwkv_mu, vres_mu, decay_up, iclr_up, gate_up,
                            vres_up, w_qb, w_kvb)
        p_rkv, p_lora, y_b, p_mla = _inproj_call(
            x, row(norm1_g[l]), sc1, sh1, lw["w_rkv"], lw["w_lora"], lw["w_conv"], lw["w_mla"],
            conv_w[l], conv_b[l], conv_ln_g[l], conv_ln_b[l])
        prm = dict(mu_rkv=lw["mu_rkv"], mu_lora=lw["mu_lora"], w_lora_up=lw["w_lora_up"],
                   w0=row(decay_w0[l]), a0=row(iclr_a0[l]), k_k=row(k_k[l]), k_a=row(k_a[l]),
                   r_k=row(r_k[l]), lnx_g=row(lnx_g[l]), lnx_b=row(lnx_b[l]))
        if l == 0:
            y_a, v_first = _rwkv_call(p_rkv, p_lora, None, prm)
        else:
            prm["v0"] = row(vres_v0[l - 1])
            y_a = _rwkv_call(p_rkv, p_lora, v_first, prm)
        q, k, v = _mla_prep_call(p_mla, cos, sin, row(q_a_norm_g[l]), row(kv_a_norm_g[l]),
                                 lw["wq"], lw["wq_rot"], lw["wkv"])
        y_c = _attn_call(q, k, v)
        x1, h2 = _outproj_call(y_a, y_b, y_c, x, wo_bf16, l, gt1,
                               row(norm2_g[l]), sc2, sh2)
        x = _mlp_call(h2, x1, w1_bf16, w2_bf16, l, gt2,
                      row(final_g), final=(l == DEPTH - 1))
    return x
```

```python
import functools

import jax
import jax.numpy as jnp
from jax import lax
from jax.experimental import pallas as pl
from jax.experimental.pallas import tpu as pltpu

F32 = jnp.float32
BF16 = jnp.bfloat16

D_MODEL = 2048
DEPTH = 2
CHUNK = 64
NORM_EPS = 1e-6
LN_EPS = 1e-5
D_FF = 4 * D_MODEL

RWKV_HEAD = 64
RWKV_DIM = 512
RWKV_HEADS = 8
RWKV_PAIRS = RWKV_HEADS // 2
DECAY_LORA = 32
ICLR_LORA = 32
VRES_LORA = 32
GATE_LORA = 96
RWKV_GN_EPS = 64e-5
LORA_PAD = 256
RWKV_DECAY_SCALE = 0.6065306597126334
RWKV_INV_BASE = 8
RWKV_INV_PASSES = 3
RWKV_BASE_PASSES = 2
RWKV_CHUNKS_PER_ITER = 4

CONV_DIM = 512
CONV_WIDTH = 31
CONV_HALO = 32

MLA_DIM = 1024
V_HEAD = 128
MLA_HEADS = 8
QK_NOPE = 128
QK_ROPE = 64
Q_LORA = 512
KV_LORA = 256
ROPE_THETA = 10000.0
QK_PAD = 256
ATTN_HEADS_PER_STEP = 4
ATTN_TILE = 512
MLA_IN_PAD = 1024

RWKV_IN = 3 * RWKV_DIM + DECAY_LORA + ICLR_LORA + GATE_LORA
CONV_IN = 2 * CONV_DIM
N_IN = RWKV_IN + CONV_IN + Q_LORA + KV_LORA + QK_ROPE

LANE = 128
V7X_VMEM_LIMIT = 56 * 1024 * 1024
MLP_VMEM_LIMIT = 60 * 1024 * 1024
MASK_VALUE = -1e30
LOG2_E = 1.4426950408889634


def _params(semantics, vmem=V7X_VMEM_LIMIT):
    return pltpu.CompilerParams(dimension_semantics=semantics, vmem_limit_bytes=vmem)


def _resident(shape):
    nd = len(shape)
    return pl.BlockSpec(shape, lambda *_: (0,) * nd, pipeline_mode=pl.Buffered(1))


def _dot(a, b):
    return jnp.dot(a, b, preferred_element_type=F32)


def _dot_nt(a, b):
    return lax.dot_general(a, b, (((1,), (1,)), ((), ())), preferred_element_type=F32)


def _split(x):
    hi = x.astype(BF16)
    lo = (x - hi.astype(F32)).astype(BF16)
    return hi, lo


def _dot3(a, b):
    ah, al = _split(a)
    bh, bl = _split(b)
    return _dot(jnp.concatenate([ah, al, ah], axis=1), jnp.concatenate([bh, bh, bl], axis=0))


def _dot2(a, b):
    bh = b.astype(BF16)
    return _dot(jnp.concatenate(_split(a), axis=1), jnp.concatenate([bh, bh], axis=0))


def _dot2_nt(a, b):
    bh = b.astype(BF16)
    return _dot_nt(jnp.concatenate(_split(a), axis=1), jnp.concatenate([bh, bh], axis=1))


def _dot2_exact_rhs(a, b_bf16):
    ah, al = _split(a)
    return _dot(jnp.concatenate([ah, al], axis=1), jnp.concatenate([b_bf16] * 2, axis=0))


def _rms(x, eps):
    return x * lax.rsqrt(jnp.mean(x * x, axis=-1, keepdims=True) + eps)


def _sigmoid(x):
    return 0.5 * jnp.tanh(0.5 * x) + 0.5


def _iota(shape, dim):
    return lax.broadcasted_iota(jnp.int32, shape, dim)


def _ada_kernel(c_ref, w_ref, b_ref, o_ref):
    c = c_ref[...]
    o_ref[...] = _dot3(c * _sigmoid(c), w_ref[...]) + b_ref[...]


def _ada_call(c, ada_w, ada_b):
    depth, d, n = ada_w.shape
    bsz = c.shape[0]
    tn = 512
    return pl.pallas_call(
        _ada_kernel,
        out_shape=jax.ShapeDtypeStruct((depth, bsz, n), F32),
        grid=(depth, n // tn),
        in_specs=[
            pl.BlockSpec((bsz, d), lambda l, j: (0, 0)),
            pl.BlockSpec((None, d, tn), lambda l, j: (l, 0, j)),
            pl.BlockSpec((None, 1, tn), lambda l, j: (l, 0, j)),
        ],
        out_specs=pl.BlockSpec((None, bsz, tn), lambda l, j: (l, 0, j)),
        compiler_params=_params(("parallel", "parallel")),
        name="ada_mod",
    )(c, ada_w, ada_b.reshape(depth, 1, n))


def _rope_kernel(pos_ref, invf_ref, cos_ref, sin_ref):
    ang = pos_ref[...] * invf_ref[...]
    cos_ref[...] = jnp.cos(ang)
    sin_ref[...] = jnp.sin(ang)


def _rope_call(positions):
    bsz, s = positions.shape
    ts = min(s, 1024)
    inv_freq = ROPE_THETA ** (-jnp.arange(0, QK_ROPE, 2, dtype=F32) / QK_ROPE)
    invf = jnp.tile(inv_freq, LANE // (QK_ROPE // 2)).reshape(1, LANE)
    pos = positions.astype(F32).reshape(bsz, s, 1)
    shp = jax.ShapeDtypeStruct((bsz, s, LANE), F32)
    return pl.pallas_call(
        _rope_kernel,
        out_shape=(shp, shp),
        grid=(bsz, s // ts),
        in_specs=[
            pl.BlockSpec((None, ts, 1), lambda b, i: (b, i, 0)),
            pl.BlockSpec((1, LANE), lambda b, i: (0, 0)),
        ],
        out_specs=(
            pl.BlockSpec((None, ts, LANE), lambda b, i: (b, i, 0)),
            pl.BlockSpec((None, ts, LANE), lambda b, i: (b, i, 0)),
        ),
        compiler_params=_params(("parallel", "parallel")),
        name="rope_tables",
    )(pos, invf)


def _conv_mixer_tile(p, w_ref, b_ref, g_ref, be_ref, ubuf, tt):
    ubuf[CONV_HALO:CONV_HALO + tt, :] = p[:, 0:CONV_DIM] * _sigmoid(p[:, CONV_DIM:2 * CONV_DIM])
    sub = 8
    base = CONV_HALO - sub
    ext = tt + sub
    acc = None
    for b in range(sub):
        part = None
        for a in range((CONV_WIDTH - 1 - b) // sub + 1):
            j = CONV_WIDTH - 1 - (sub * a + b)
            term = ubuf[base - sub * a:base - sub * a + ext, :] * w_ref[j:j + 1, :]
            part = term if part is None else part + term
        if b:
            part = pltpu.roll(part, b, 0)
        acc = part if acc is None else acc + part
    acc = acc[sub:sub + tt, :] + b_ref[...]
    ubuf[0:CONV_HALO, :] = ubuf[tt:tt + CONV_HALO, :]
    mean = jnp.mean(acc, axis=-1, keepdims=True)
    xc = acc - mean
    var = jnp.mean(xc * xc, axis=-1, keepdims=True)
    u = xc * lax.rsqrt(var + LN_EPS) * g_ref[...] + be_ref[...]
    return u * _sigmoid(u)


def _inproj_kernel(x_ref, g_ref, sc_ref, sh_ref, w_rkv, w_lora, w_conv, w_mla,
                   cw_ref, cb_ref, cg_ref, cbe_ref, o_rkv, o_lora, o_yb, o_mla, ubuf):
    tm = x_ref.shape[0]

    @pl.when(pl.program_id(1) == 0)
    def _():
        ubuf[0:CONV_HALO, :] = jnp.zeros((CONV_HALO, CONV_DIM), F32)

    x = x_ref[...]
    h = _rms(x, NORM_EPS) * g_ref[...] * (1.0 + sc_ref[...]) + sh_ref[...]
    hb = h.astype(BF16)
    p_conv = _dot(hb, w_conv[...])
    o_rkv[...] = _dot(hb, w_rkv[...])
    o_lora[...] = _dot(hb, w_lora[...])
    o_mla[...] = _dot(hb, w_mla[...]).astype(o_mla.dtype)
    o_yb[...] = _conv_mixer_tile(p_conv, cw_ref, cb_ref, cg_ref, cbe_ref, ubuf, tm).astype(o_yb.dtype)


def _inproj_call(x, g, sc, sh, w_rkv, w_lora, w_conv, w_mla, conv_w, conv_b, ln_g, ln_b):
    bsz, s, d = x.shape
    tm = min(s, 512)
    widths = (w_rkv.shape[1], w_lora.shape[1], CONV_DIM, w_mla.shape[1])
    tok = lambda n: pl.BlockSpec((None, tm, n), lambda b, i: (b, i, 0))
    per_b = pl.BlockSpec((None, 1, d), lambda b, i: (b, 0, 0))
    wpad = jnp.zeros((CONV_HALO, CONV_DIM), F32).at[:CONV_WIDTH].set(conv_w)
    row = _resident((1, CONV_DIM))
    return pl.pallas_call(
        _inproj_kernel,
        out_shape=tuple(jax.ShapeDtypeStruct((bsz, s, n), dt)
                        for n, dt in zip(widths, (F32, F32, BF16, BF16))),
        grid=(bsz, s // tm),
        in_specs=[tok(d), _resident((1, d)), per_b, per_b,
                  _resident(w_rkv.shape), _resident(w_lora.shape),
                  _resident(w_conv.shape), _resident(w_mla.shape),
                  _resident((CONV_HALO, CONV_DIM)), row, row, row],
        out_specs=tuple(tok(n) for n in widths),
        scratch_shapes=[pltpu.VMEM((tm + CONV_HALO, CONV_DIM), F32)],
        compiler_params=_params(("parallel", "arbitrary")),
        name="norm_inproj_conv",
    )(x, g, sc, sh, w_rkv, w_lora, w_conv, w_mla, wpad,
      conv_b.reshape(1, -1), ln_g.reshape(1, -1), ln_b.reshape(1, -1))


def _block_diag(y):
    lane = _iota(y.shape, 1)
    zero = jnp.zeros_like(y)
    return jnp.concatenate([jnp.where(lane < RWKV_HEAD, y, zero),
                            jnp.where(lane >= RWKV_HEAD, y, zero)], axis=0)


def _packed_mm(x, ys, passes=3):
    if passes == 1:
        lhs = x.astype(BF16)
        cols = [_block_diag(y.astype(BF16)) for y in ys]
    elif passes == 2:
        lhs = jnp.concatenate(_split(x), axis=1)
        cols = [jnp.concatenate([_block_diag(y.astype(BF16))] * 2, axis=0) for y in ys]
    else:
        xh, xl = _split(x)
        lhs = jnp.concatenate([xh, xl, xh], axis=1)
        cols = []
        for y in ys:
            yh, yl = _split(y)
            cols.append(jnp.concatenate([_block_diag(yh), _block_diag(yh), _block_diag(yl)], axis=0))
    out = _dot(lhs, cols[0] if len(cols) == 1 else jnp.concatenate(cols, axis=1))
    return [out[:, i * LANE:(i + 1) * LANE] for i in range(len(ys))]


def _head_sum_matrix(scale):
    r = _iota((LANE, LANE), 0) // RWKV_HEAD
    c = _iota((LANE, LANE), 1) // RWKV_HEAD
    return jnp.where(r == c, scale, 0.0).astype(BF16)


def _rwkv_kernel(*refs, has_vres, tt):
    if has_vres:
        (prkv_ref, plora_ref, vfirst_ref, mu_rkv, mu_lora, wc_ref, w0_ref, a0_ref, v0_ref,
         kk_ref, ka_ref, rk_ref, lng_ref, lnb_ref, y_ref,
         prev_rkv, prev_lora, h_scr, at_s, rt_s, bt_s, kt_s, bp_s, kp_s, v_s, pf_s, o_s) = refs
    else:
        (prkv_ref, plora_ref, mu_rkv, mu_lora, wc_ref, w0_ref, a0_ref,
         kk_ref, ka_ref, rk_ref, lng_ref, lnb_ref, y_ref, vout_ref,
         prev_rkv, prev_lora, h_scr, at_s, rt_s, bt_s, kt_s, bp_s, kp_s, v_s, pf_s, o_s) = refs
    n_chunks = tt // CHUNK

    @pl.when(pl.program_id(1) == 0)
    def _():
        prev_rkv[...] = jnp.zeros_like(prev_rkv)
        prev_lora[...] = jnp.zeros_like(prev_lora)
        h_scr[...] = jnp.zeros_like(h_scr)

    def shift_mix(p_ref, prev_ref, mu_ref):
        p = p_ref[...]
        row = _iota(p.shape, 0)
        prev = jnp.where(row == 0, prev_ref[0:1, :], pltpu.roll(p, 1, 0))
        prev_ref[0:1, :] = p[tt - 1:tt, :]
        return p + (prev - p) * mu_ref[...]

    xs = shift_mix(prkv_ref, prev_rkv, mu_rkv)
    xl = shift_mix(plora_ref, prev_lora, mu_lora)
    r = xs[:, 0:RWKV_DIM]
    k = xs[:, RWKV_DIM:2 * RWKV_DIM]
    v = xs[:, 2 * RWKV_DIM:3 * RWKV_DIM]

    lane = _iota(xl.shape, 1)
    o_w, o_a, o_g = DECAY_LORA, DECAY_LORA + ICLR_LORA, DECAY_LORA + ICLR_LORA + GATE_LORA
    act = jnp.where(lane < o_w, jnp.tanh(xl),
                    jnp.where((lane >= o_a) & (lane < o_g), _sigmoid(xl), xl))
    act_h, act_l = _split(act)
    lora = _dot(jnp.concatenate([act_h, act_l, act_h], axis=1), wc_ref[...])
    z = w0_ref[...] + lora[:, 0:RWKV_DIM]
    lw = -RWKV_DECAY_SCALE * _sigmoid(z)
    a_ic = _sigmoid(a0_ref[...] + lora[:, RWKV_DIM:2 * RWKV_DIM])
    gate = lora[:, 2 * RWKV_DIM:3 * RWKV_DIM]
    if has_vres:
        v = v + (vfirst_ref[...] - v) * _sigmoid(v0_ref[...] + lora[:, 3 * RWKV_DIM:4 * RWKV_DIM])
    else:
        vout_ref[...] = v

    ones_bd = _head_sum_matrix(1.0)
    mean_bd = _head_sum_matrix(1.0 / RWKV_HEAD)

    def per_head(x, mat, split=False):
        mm = _dot2_exact_rhs if split else (lambda a, b: _dot(a.astype(BF16), b))
        return jnp.concatenate(
            [mm(x[:, p * LANE:(p + 1) * LANE], mat) for p in range(RWKV_PAIRS)], axis=1)

    kk = k * kk_ref[...]
    kk = kk * lax.rsqrt(jnp.maximum(per_head(kk * kk, ones_bd), 1e-24))
    k2 = k * (1.0 + (a_ic - 1.0) * ka_ref[...])
    a_vec = -kk
    b_vec = kk * a_ic
    bonus = per_head(r * k2 * rk_ref[...], ones_bd) * v

    tr = _iota((tt, tt), 0)
    tc = _iota((tt, tt), 1)
    tri = jnp.where((tr // CHUNK == tc // CHUNK) & (tc <= tr), 1.0, 0.0).astype(BF16)
    cum = _dot2_exact_rhs_left(tri, lw)
    cum3 = cum.reshape(n_chunks, CHUNK, RWKV_DIM)
    cum_end = jnp.broadcast_to(cum3[:, CHUNK - 1:CHUNK, :], cum3.shape).reshape(tt, RWKV_DIM)
    pf = jnp.exp(cum)
    pinv = jnp.exp(-cum)
    pprev = jnp.exp(cum - lw)
    pend = jnp.exp(cum_end - cum)

    def put(dst, val):
        for p in range(RWKV_PAIRS):
            dst[p] = val[:, p * LANE:(p + 1) * LANE]

    put(at_s, a_vec * pprev)
    put(rt_s, r * pf)
    put(bt_s, b_vec * pinv)
    put(kt_s, k2 * pinv)
    put(bp_s, b_vec * pend)
    put(kp_s, k2 * pend)
    put(v_s, v)
    put(pf_s, pf)

    t_loc = _iota((CHUNK, LANE), 0)
    s_loc = _iota((CHUNK, LANE), 1) % RWKV_HEAD
    strict = s_loc < t_loc
    incl = s_loc <= t_loc
    eye_p = jnp.where(s_loc == t_loc, 1.0, 0.0)
    same_block = lambda n: (t_loc // n) == (s_loc // n)
    in_base = same_block(RWKV_INV_BASE)
    merge_masks = []
    n = RWKV_INV_BASE
    while n < CHUNK:
        merge_masks.append(same_block(2 * n) & jnp.logical_not(same_block(n)))
        n *= 2

    group = min(n_chunks, RWKV_CHUNKS_PER_ITER)

    def chunk_group(cg, carry):
        streams = []
        for ci in range(group):
            c = cg * group + ci
            rows = pl.ds(pl.multiple_of(c * CHUNK, CHUNK), CHUNK)
            last = pl.ds(c * CHUNK + CHUNK - 1, 1)
            streams += [(p, rows, last) for p in range(RWKV_PAIRS)]
        ns = range(len(streams))
        load = lambda ref: [ref[p, rows, :] for (p, rows, _) in streams]
        at, rt, bt, kt, bp, kp, vv = (load(r) for r in (at_s, rt_s, bt_s, kt_s, bp_s, kp_s, v_s))
        g = [_dot2_nt(jnp.concatenate([at[i], rt[i]], axis=0),
                      jnp.concatenate([_block_diag(bt[i]), _block_diag(kt[i])], axis=0)) for i in ns]
        aab = [jnp.where(strict, g[i][0:CHUNK, 0:LANE], 0.0) for i in ns]
        aak = [jnp.where(strict, g[i][0:CHUNK, LANE:2 * LANE], 0.0) for i in ns]
        arb = [jnp.where(incl, g[i][CHUNK:2 * CHUNK, 0:LANE], 0.0) for i in ns]
        ark = [jnp.where(incl, g[i][CHUNK:2 * CHUNK, LANE:2 * LANE], 0.0) for i in ns]
        n_diag = [jnp.where(in_base, aab[i], 0.0) for i in ns]
        inv, base = RWKV_INV_PASSES, RWKV_BASE_PASSES
        dpow = [_packed_mm(n_diag[i], [n_diag[i]], passes=base)[0] for i in ns]
        tinv = [eye_p + n_diag[i] for i in ns]
        akv = [_packed_mm(aak[i], [vv[i]], passes=1)[0] for i in ns]
        arkv = [_packed_mm(ark[i], [vv[i]], passes=1)[0] for i in ns]
        res = [_packed_mm(dpow[i], [dpow[i], tinv[i]], passes=base) for i in ns]
        tinv = [tinv[i] + res[i][1] for i in ns]
        tinv = [tinv[i] + _packed_mm(res[i][0], [tinv[i]], passes=base)[0] for i in ns]
        for level_mask in merge_masks:
            ct = [_packed_mm(jnp.where(level_mask, aab[i], 0.0), [tinv[i]], passes=inv)[0] for i in ns]
            tinv = [tinv[i] + _packed_mm(tinv[i], [ct[i]], passes=inv)[0] for i in ns]
        wu = [_packed_mm(tinv[i], [at[i], akv[i]], passes=inv) for i in ns]
        ab = [_packed_mm(arb[i], wu[i], passes=1) for i in ns]
        q_m = [rt[i] + ab[i][0] for i in ns]
        y0 = [ab[i][1] + arkv[i] for i in ns]
        upd = [_dot2(jnp.concatenate([bp[i], kp[i]], axis=0).T,
                     jnp.concatenate([jnp.concatenate(wu[i], axis=1),
                                      jnp.concatenate([jnp.zeros_like(vv[i]), vv[i]], axis=1)], axis=0))
               for i in ns]
        first_head = _iota((CHUNK, 2 * LANE), 1) % LANE < RWKV_HEAD
        upd = [jnp.where(first_head, upd[i][0:RWKV_HEAD], upd[i][RWKV_HEAD:2 * RWKV_HEAD]) for i in ns]
        h = [h_scr[p] for p in range(RWKV_PAIRS)]
        for i in ns:
            p, rows, last = streams[i]
            m_p = upd[i][:, 0:LANE] + jnp.where(s_loc == t_loc, pf_s[p, last, :], 0.0)
            (qh_mh,) = _packed_mm(jnp.concatenate([q_m[i], m_p], axis=0), [h[p]], passes=3)
            o_s[rows, p * LANE:(p + 1) * LANE] = qh_mh[0:CHUNK] + y0[i]
            h[p] = qh_mh[CHUNK:2 * CHUNK] + upd[i][:, LANE:2 * LANE]
        for p in range(RWKV_PAIRS):
            h_scr[p] = h[p]
        return carry

    lax.fori_loop(0, n_chunks // group, chunk_group, 0)

    o = o_s[...]
    mean = per_head(o, mean_bd, split=True)
    oc = o - mean
    var = per_head(oc * oc, mean_bd)
    on = oc * lax.rsqrt(var + RWKV_GN_EPS) * lng_ref[...] + lnb_ref[...]
    y_ref[...] = ((on + bonus) * gate).astype(y_ref.dtype)


def _dot2_exact_rhs_left(m_bf16, x):
    xh, xl = _split(x)
    return _dot(jnp.concatenate([m_bf16] * 2, axis=1), jnp.concatenate([xh, xl], axis=0))


def _rwkv_call(p_rkv, p_lora, v_first, prm):
    bsz, s, _ = p_rkv.shape
    tt = min(s, 256)
    has_vres = v_first is not None
    tok = lambda n: pl.BlockSpec((None, tt, n), lambda b, i: (b, i, 0))
    row = lambda n: _resident((1, n))
    in_specs = [tok(3 * RWKV_DIM), tok(LORA_PAD)]
    args = [p_rkv, p_lora]
    if has_vres:
        in_specs.append(tok(RWKV_DIM))
        args.append(v_first)
    in_specs += [row(3 * RWKV_DIM), row(LORA_PAD), _resident((3 * LORA_PAD, 4 * RWKV_DIM)),
                 row(RWKV_DIM), row(RWKV_DIM)]
    args += [prm["mu_rkv"], prm["mu_lora"], prm["w_lora_up"], prm["w0"], prm["a0"]]
    if has_vres:
        in_specs.append(row(RWKV_DIM))
        args.append(prm["v0"])
    in_specs += [row(RWKV_DIM)] * 5
    args += [prm["k_k"], prm["k_a"], prm["r_k"], prm["lnx_g"], prm["lnx_b"]]
    y_shape = jax.ShapeDtypeStruct((bsz, s, RWKV_DIM), BF16)
    if has_vres:
        out_shape, out_specs = y_shape, tok(RWKV_DIM)
    else:
        out_shape = (y_shape, jax.ShapeDtypeStruct((bsz, s, RWKV_DIM), F32))
        out_specs = (tok(RWKV_DIM), tok(RWKV_DIM))
    pair_tile = pltpu.VMEM((RWKV_PAIRS, tt, LANE), F32)
    out = pl.pallas_call(
        functools.partial(_rwkv_kernel, has_vres=has_vres, tt=tt),
        out_shape=out_shape,
        grid=(bsz, s // tt),
        in_specs=in_specs,
        out_specs=out_specs,
        scratch_shapes=[
            pltpu.VMEM((8, 3 * RWKV_DIM), F32),
            pltpu.VMEM((8, LORA_PAD), F32),
            pltpu.VMEM((RWKV_PAIRS, RWKV_HEAD, LANE), F32),
        ] + [pair_tile] * 8 + [pltpu.VMEM((tt, RWKV_DIM), F32)],
        compiler_params=_params(("parallel", "arbitrary")),
        name="rwkv7_mixer",
    )(*args)
    return out if has_vres else out


def _mla_prep_kernel(p_ref, cos_ref, sin_ref, qg_ref, kvg_ref, wq_ref, wkv_ref,
                     q_ref, k_ref, v_ref):
    p = p_ref[...].astype(F32)
    cos = cos_ref[...]
    sin = sin_ref[...]
    qc = (_rms(p[:, 0:Q_LORA], NORM_EPS) * qg_ref[...]).astype(BF16)
    kvc = (_rms(p[:, Q_LORA:Q_LORA + KV_LORA], NORM_EPS) * kvg_ref[...]).astype(BF16)
    q = _dot(qc, wq_ref[...])
    kv = _dot(kvc, wkv_ref[...])
    o_kr = Q_LORA + KV_LORA
    k_rope = (p[:, o_kr:o_kr + LANE] * cos + p[:, o_kr + LANE:o_kr + 2 * LANE] * sin).astype(BF16)
    o_rope = MLA_HEADS * QK_NOPE
    o_rot = o_rope + MLA_HEADS * QK_ROPE
    for h in range(MLA_HEADS):
        b0 = h * QK_PAD
        q_ref[:, b0:b0 + LANE] = q[:, h * QK_NOPE:(h + 1) * QK_NOPE].astype(BF16)
        pair = (h // 2) * LANE
        rope = q[:, o_rope + pair:o_rope + pair + LANE]
        rot = q[:, o_rot + pair:o_rot + pair + LANE]
        if h % 2:
            rope = pltpu.roll(rope, QK_ROPE, 1)
            rot = pltpu.roll(rot, QK_ROPE, 1)
        q_ref[:, b0 + LANE:b0 + 2 * LANE] = (rope * cos + rot * sin).astype(BF16)
        k_ref[:, b0:b0 + LANE] = kv[:, b0:b0 + LANE].astype(BF16)
        k_ref[:, b0 + LANE:b0 + 2 * LANE] = k_rope
        v_ref[h * V_HEAD:(h + 1) * V_HEAD, :] = kv[:, b0 + LANE:b0 + 2 * LANE].T.astype(BF16)


def _mla_prep_call(p_mla, cos, sin, qg, kvg, wq, wkv):
    bsz, s, _ = p_mla.shape
    tm = min(s, ATTN_TILE)
    tok = lambda n: pl.BlockSpec((None, tm, n), lambda b, i: (b, i, 0))
    return pl.pallas_call(
        _mla_prep_kernel,
        out_shape=(jax.ShapeDtypeStruct((bsz, s, MLA_HEADS * QK_PAD), BF16),
                   jax.ShapeDtypeStruct((bsz, s, MLA_HEADS * QK_PAD), BF16),
                   jax.ShapeDtypeStruct((bsz, s // tm, MLA_DIM, tm), BF16)),
        grid=(bsz, s // tm),
        in_specs=[tok(MLA_IN_PAD), tok(LANE), tok(LANE),
                  _resident((1, Q_LORA)), _resident((1, KV_LORA)),
                  _resident(wq.shape), _resident(wkv.shape)],
        out_specs=(tok(MLA_HEADS * QK_PAD), tok(MLA_HEADS * QK_PAD),
                   pl.BlockSpec((None, None, MLA_DIM, tm), lambda b, i: (b, i, 0, 0))),
        compiler_params=_params(("parallel", "parallel")),
        name="mla_prep",
    )(p_mla, cos, sin, qg, kvg, wq, wkv)


def _attn_kernel(q_ref, k_ref, vt_ref, o_ref, *, tq, nh):
    i = pl.program_id(2)
    heads = range(nh)
    q = [q_ref[:, h * QK_PAD:(h + 1) * QK_PAD] for h in heads]

    def step(j, carry, masked):
        keys = pl.ds(pl.multiple_of(j * tq, tq), tq)
        st = [_dot_nt(k_ref[keys, h * QK_PAD:(h + 1) * QK_PAD], q[h]) for h in heads]
        if masked:
            visible = (_iota((tq, tq), 0) // CHUNK) <= (_iota((tq, tq), 1) // CHUNK)
            st = [jnp.where(visible, s, MASK_VALUE) for s in st]
        m_new = [jnp.maximum(carry[h][0], jnp.max(st[h], axis=0, keepdims=True)) for h in heads]
        alpha = [jnp.exp2(carry[h][0] - m_new[h]) for h in heads]
        pt = [jnp.exp2(st[h] - m_new[h]) for h in heads]
        l_new = [alpha[h] * carry[h][1] + jnp.sum(pt[h], axis=0, keepdims=True) for h in heads]
        pv = [_dot(vt_ref[j, h * V_HEAD:(h + 1) * V_HEAD, :], pt[h].astype(BF16)) for h in heads]
        return tuple((m_new[h], l_new[h], alpha[h] * carry[h][2] + pv[h]) for h in heads)

    init = (jnp.full((1, tq), MASK_VALUE, F32), jnp.zeros((1, tq), F32), jnp.zeros((V_HEAD, tq), F32))
    carry = lax.fori_loop(0, i, lambda j, c: step(j, c, False), (init,) * nh)
    carry = step(i, carry, True)
    for h in heads:
        o_ref[:, h * V_HEAD:(h + 1) * V_HEAD] = (carry[h][2] / carry[h][1]).T.astype(o_ref.dtype)


def _attn_call(q, k, vt):
    bsz, s, _ = q.shape
    tq = min(s, ATTN_TILE)
    nh = ATTN_HEADS_PER_STEP
    return pl.pallas_call(
        functools.partial(_attn_kernel, tq=tq, nh=nh),
        out_shape=jax.ShapeDtypeStruct((bsz, s, MLA_DIM), BF16),
        grid=(bsz, MLA_HEADS // nh, s // tq),
        in_specs=[pl.BlockSpec((None, tq, nh * QK_PAD), lambda b, h, i: (b, i, h)),
                  pl.BlockSpec((None, s, nh * QK_PAD), lambda b, h, i: (b, 0, h)),
                  pl.BlockSpec((None, s // tq, nh * V_HEAD, tq), lambda b, h, i: (b, 0, h, 0))],
        out_specs=pl.BlockSpec((None, tq, nh * V_HEAD), lambda b, h, i: (b, i, h)),
        compiler_params=_params(("parallel", "parallel", "arbitrary")),
        name="mla_attention",
    )(q, k, vt)


def _outproj_kernel(ya_ref, yb_ref, yc_ref, x_ref, wo_ref, gt_ref, g_ref, sc_ref, sh_ref,
                    x1_ref, h2_ref):
    o1, o2 = RWKV_DIM, RWKV_DIM + CONV_DIM
    half = x_ref.shape[0] // 2
    ys = []
    for r in (slice(0, half), slice(half, 2 * half)):
        ys.append(_dot(ya_ref[r, :], wo_ref[0:o1, :]) + _dot(yb_ref[r, :], wo_ref[o1:o2, :])
                  + _dot(yc_ref[r, :], wo_ref[o2:o2 + MLA_DIM, :]))
    for r, y in zip((slice(0, half), slice(half, 2 * half)), ys):
        x1 = x_ref[r, :] + gt_ref[...] * y
        x1_ref[r, :] = x1
        h2_ref[r, :] = (_rms(x1, NORM_EPS) * g_ref[...] * (1.0 + sc_ref[...]) + sh_ref[...]).astype(BF16)


def _outproj_call(ya, yb, yc, x, wo, layer, gt, g, sc, sh):
    bsz, s, d = x.shape
    tm = min(s, 512)
    tok = lambda n: pl.BlockSpec((None, tm, n), lambda b, i: (b, i, 0))
    per_b = pl.BlockSpec((None, 1, d), lambda b, i: (b, 0, 0))
    wo_spec = pl.BlockSpec((None,) + wo.shape[1:], lambda b, i: (layer, 0, 0),
                           pipeline_mode=pl.Buffered(1))
    return pl.pallas_call(
        _outproj_kernel,
        out_shape=(jax.ShapeDtypeStruct((bsz, s, d), F32), jax.ShapeDtypeStruct((bsz, s, d), BF16)),
        grid=(bsz, s // tm),
        in_specs=[tok(RWKV_DIM), tok(CONV_DIM), tok(MLA_DIM), tok(d), wo_spec,
                  per_b, _resident((1, d)), per_b, per_b],
        out_specs=(tok(d), tok(d)),
        compiler_params=_params(("parallel", "parallel")),
        name="outproj_norm2",
    )(ya, yb, yc, x, wo, gt, g, sc, sh)


def _mlp_kernel(h_ref, x_ref, w1_ref, w2_ref, gt_ref, fg_ref, o_ref, *, final):
    f = pl.program_id(2)

    @pl.when(f == 0)
    def _():
        o_ref[...] = jnp.zeros_like(o_ref)

    a = jnp.maximum(_dot(h_ref[...], w1_ref[...]), 0.0)
    o_ref[...] += _dot((a * a).astype(BF16), w2_ref[...])

    @pl.when(f == pl.num_programs(2) - 1)
    def _():
        xo = x_ref[...] + gt_ref[...] * o_ref[...]
        if final:
            xo = _rms(xo, NORM_EPS) * fg_ref[...]
        o_ref[...] = xo


def _mlp_call(h2, x1, w1, w2, layer, gt, fg, final):
    bsz, s, d = x1.shape
    tm = min(s, 512)
    tf = 2048
    tok_map = lambda b, i, f: (b, i, 0)
    tok = lambda: pl.BlockSpec((None, tm, d), tok_map)
    return pl.pallas_call(
        functools.partial(_mlp_kernel, final=final),
        out_shape=jax.ShapeDtypeStruct((bsz, s, d), F32),
        grid=(bsz, s // tm, D_FF // tf),
        in_specs=[tok(), tok(),
                  pl.BlockSpec((None, d, tf), lambda b, i, f: (layer, 0, f)),
                  pl.BlockSpec((None, tf, d), lambda b, i, f: (layer, f, 0)),
                  pl.BlockSpec((None, 1, d), lambda b, i, f: (b, 0, 0)),
                  pl.BlockSpec((1, d), lambda b, i, f: (0, 0))],
        out_specs=tok(),
        compiler_params=_params(("parallel", "parallel", "arbitrary"), MLP_VMEM_LIMIT),
        name="relu2_mlp",
    )(h2, x1, w1, w2, gt, fg)


def _pad_cols(w, n):
    return jnp.pad(w, ((0, 0), (0, n - w.shape[1])))


def _rotate_half_cols(w):
    half = w.shape[-1] // 2
    return jnp.concatenate([-w[..., half:], w[..., :half]], axis=-1)


def _layer_weights(l, w_in, w_in_vres, rwkv_mu, vres_mu, decay_up, iclr_up, gate_up, vres_up,
                   w_qb, w_kvb):
    d = D_MODEL
    wl = w_in[l]
    o_lora = 3 * RWKV_DIM
    o_conv = RWKV_IN
    o_mla = RWKV_IN + CONV_IN
    has_vres = l > 0
    lora_cols = [wl[:, o_lora:o_conv]]
    mu_cols = [rwkv_mu[l, o_lora:o_conv]]
    if has_vres:
        lora_cols.append(w_in_vres[l - 1])
        mu_cols.append(vres_mu[l - 1])
    w_lora = _pad_cols(jnp.concatenate(lora_cols, axis=1), LORA_PAD)
    mu_lora = jnp.pad(jnp.concatenate(mu_cols), (0, LORA_PAD - sum(m.shape[0] for m in mu_cols)))
    mla = wl[:, o_mla:N_IN]
    kr = mla[:, Q_LORA + KV_LORA:]
    zpad = jnp.zeros((d, LANE - QK_ROPE), F32)
    w_mla = jnp.concatenate([mla[:, :Q_LORA + KV_LORA], kr, zpad, _rotate_half_cols(kr), zpad], axis=1)

    up = jnp.zeros((LORA_PAD, 4 * RWKV_DIM), F32)
    o_a, o_g = DECAY_LORA, DECAY_LORA + ICLR_LORA
    o_v = o_g + GATE_LORA
    up = up.at[0:o_a, 0:RWKV_DIM].set(decay_up[l])
    up = up.at[o_a:o_g, RWKV_DIM:2 * RWKV_DIM].set(iclr_up[l])
    up = up.at[o_g:o_v, 2 * RWKV_DIM:3 * RWKV_DIM].set(gate_up[l])
    if has_vres:
        up = up.at[o_v:o_v + VRES_LORA, 3 * RWKV_DIM:].set(vres_up[l - 1])
    up_hi = up.astype(BF16)

    scale = (QK_NOPE + QK_ROPE) ** -0.5 * LOG2_E
    wq = (w_qb[l] * scale).reshape(Q_LORA, MLA_HEADS, QK_NOPE + QK_ROPE)
    nope, rope = wq[:, :, :QK_NOPE], wq[:, :, QK_NOPE:]
    flat = lambda w: w.reshape(Q_LORA, -1)
    wq_all = jnp.concatenate([flat(nope), flat(rope), flat(_rotate_half_cols(rope))], axis=1)
    return dict(
        w_rkv=wl[:, :o_lora].astype(BF16), w_lora=w_lora.astype(BF16),
        w_conv=wl[:, o_conv:o_mla].astype(BF16), w_mla=w_mla.astype(BF16),
        mu_rkv=rwkv_mu[l, :o_lora].reshape(1, -1), mu_lora=mu_lora.reshape(1, -1),
        w_lora_up=jnp.concatenate([up_hi, up_hi, (up - up_hi.astype(F32)).astype(BF16)], axis=0),
        wq=wq_all.astype(BF16),
        wkv=w_kvb[l].astype(BF16))


def kernel(x, c, positions, ada_w, ada_b, norm1_g, norm2_g, final_g, w_in, w_in_vres, rwkv_mu,
           vres_mu, decay_w0, decay_up, iclr_a0, iclr_up, gate_up, vres_v0, vres_up, k_k, k_a, r_k,
           lnx_g, lnx_b, conv_w, conv_b, conv_ln_g, conv_ln_b, q_a_norm_g, w_qb, kv_a_norm_g, w_kvb,
           w_out, mlp_w1, mlp_w2):
    bsz, s, d = x.shape
    mod = _ada_call(c, ada_w, ada_b).reshape(DEPTH, bsz, 6, 1, d)
    cos, sin = _rope_call(positions)
    w1_bf16, w2_bf16, wo_bf16 = mlp_w1.astype(BF16), mlp_w2.astype(BF16), w_out.astype(BF16)
    row = lambda a: a.reshape(1, -1)
    v_first = None
    for l in range(DEPTH):
        sh1, sc1, gt1, sh2, sc2, gt2 = (mod[l, :, j] for j in range(6))
        lw = _layer_weights(l, w_in, w_in_vres, rwkv_mu, vres_mu, decay_up, iclr_up, gate_up,
                            vres_up, w_qb, w_kvb)
        p_rkv, p_lora, y_b, p_mla = _inproj_call(
            x, row(norm1_g[l]), sc1, sh1, lw["w_rkv"], lw["w_lora"], lw["w_conv"], lw["w_mla"],
            conv_w[l], conv_b[l], conv_ln_g[l], conv_ln_b[l])
        prm = dict(mu_rkv=lw["mu_rkv"], mu_lora=lw["mu_lora"], w_lora_up=lw["w_lora_up"],
                   w0=row(decay_w0[l]), a0=row(iclr_a0[l]), k_k=row(k_k[l]), k_a=row(k_a[l]),
                   r_k=row(r_k[l]), lnx_g=row(lnx_g[l]), lnx_b=row(lnx_b[l]))
        if l == 0:
            y_a, v_first = _rwkv_call(p_rkv, p_lora, None, prm)
        else:
            prm["v0"] = row(vres_v0[l - 1])
            y_a = _rwkv_call(p_rkv, p_lora, v_first, prm)
        q, k, v = _mla_prep_call(p_mla, cos, sin, row(q_a_norm_g[l]), row(kv_a_norm_g[l]),
                                 lw["wq"], lw["wkv"])
        y_c = _attn_call(q, k, v)
        x1, h2 = _outproj_call(y_a, y_b, y_c, x, wo_bf16, l, gt1,
                               row(norm2_g[l]), sc2, sh2)
        x = _mlp_call(h2, x1, w1_bf16, w2_bf16, l, gt2,
                      row(final_g), final=(l == DEPTH - 1))
    return x
```
